```python
import jax, jax.numpy as jnp
from jax import lax
import numpy as np

D_MODEL = 2048
BATCH = 8
SEQ = 4096
DEPTH = 1

N_ATTN_HEADS = 8
HEAD_DIM = 128
ATTN_WIDTH = N_ATTN_HEADS * HEAD_DIM
POOL_WINDOWS = (2, 4, 8, 16)
N_POOL_GROUPS = len(POOL_WINDOWS)
POOL_GROUP_WIDTH = 256
POOL_WIDTH = N_POOL_GROUPS * POOL_GROUP_WIDTH
N_BRANCHES = 2
IN_WIDTH = 3 * ATTN_WIDTH + POOL_WIDTH + N_BRANCHES * D_MODEL
D_FF = 5632
CONV_WIDTH = 3
PLE_DIM = 256
Q_BLOCK = 128
EPS = 1e-6

kernel_name = "hybrid_stickbreak_pool_convffn_layer"


def rmsnorm(x, gain):
    xf = x.astype(jnp.float32)
    y = xf * lax.rsqrt(jnp.mean(xf * xf, axis=-1, keepdims=True) + EPS)
    return (y * gain.astype(jnp.float32)).astype(x.dtype)


def stick_breaking_attention(q, k, v):
    B, S, H, Dh = q.shape
    nb = S // Q_BLOCK
    scale = Dh ** -0.5
    qb = q.reshape(B, nb, Q_BLOCK, H, Dh).transpose(1, 0, 3, 2, 4)
    starts = jnp.arange(nb, dtype=jnp.int32) * Q_BLOCK
    key_pos = jnp.arange(S, dtype=jnp.int32)

    def block(args):
        q_i, start = args
        z = jnp.einsum('bhqd,bkhd->bhqk', q_i, k).astype(jnp.float32) * scale
        q_pos = start + jnp.arange(Q_BLOCK, dtype=jnp.int32)
        causal = key_pos[None, :] < q_pos[:, None]
        log_1m_beta = jnp.where(causal, jax.nn.log_sigmoid(-z), 0.0)
        suffix = lax.cumsum(log_1m_beta, axis=3, reverse=True) - log_1m_beta
        log_a = jax.nn.log_sigmoid(z) + suffix
        a = jnp.where(causal, jnp.exp(log_a), 0.0)
        return jnp.einsum('bhqk,bkhd->bqhd', a.astype(v.dtype), v)

    out = lax.map(block, (qb, starts))
    return out.transpose(1, 0, 2, 3, 4).reshape(B, S, H * Dh)


def multiscale_causal_pool(u):
    B, S, _ = u.shape
    groups = u.astype(jnp.float32).reshape(B, S, N_POOL_GROUPS, POOL_GROUP_WIDTH)
    csum = jnp.pad(jnp.cumsum(groups, axis=1), ((0, 0), (1, 0), (0, 0), (0, 0)))
    pos = jnp.arange(S, dtype=jnp.int32)
    means = []
    for g, w in enumerate(POOL_WINDOWS):
        upper = csum[:, 1:, g]
        lower = jnp.pad(csum[:, :S + 1 - w, g], ((0, 0), (w - 1, 0), (0, 0)))
        count = jnp.minimum(pos + 1, w).astype(jnp.float32)[None, :, None]
        means.append((upper - lower) / count)
    return jnp.stack(means, axis=2) - groups


def causal_depthwise_conv(h, w, b):
    S = h.shape[1]
    hp = jnp.pad(h, ((0, 0), (CONV_WIDTH - 1, 0), (0, 0)))
    out = b
    for j in range(CONV_WIDTH):
        out = out + hp[:, j:j + S] * w[j]
    return out


def _fwd_setup_inputs(seed: int = 0) -> dict:
    key = jax.random.key(seed)
    ks = jax.random.split(key, 20)
    f32 = jnp.float32

    def w(k, shape, fan_in):
        return jax.random.normal(k, shape, f32) * (fan_in ** -0.5)

    def gain(k, shape):
        return 1.0 + 0.02 * jax.random.normal(k, shape, f32)

    return {
        "x": jax.random.normal(ks[0], (BATCH, SEQ, D_MODEL), f32),
        "p": jax.random.normal(ks[1], (DEPTH, BATCH, SEQ, PLE_DIM), f32),
        "norm_mix_pre": gain(ks[2], (DEPTH, D_MODEL)),
        "w_in": w(ks[3], (DEPTH, D_MODEL, IN_WIDTH), D_MODEL),
        "w_attn_branch": w(ks[4], (DEPTH, ATTN_WIDTH, D_MODEL), ATTN_WIDTH),
        "w_pool_group": w(ks[5], (DEPTH, N_POOL_GROUPS, POOL_GROUP_WIDTH, POOL_GROUP_WIDTH), POOL_GROUP_WIDTH),
        "pool_scale": gain(ks[6], (DEPTH, POOL_WIDTH)),
        "w_pool_branch": w(ks[7], (DEPTH, POOL_WIDTH, D_MODEL), POOL_WIDTH),
        "w_out": w(ks[8], (DEPTH, D_MODEL, D_MODEL), D_MODEL),
        "norm_mix_post": gain(ks[9], (DEPTH, D_MODEL)),
        "norm_ffn_pre": gain(ks[10], (DEPTH, D_MODEL)),
        "w_up": w(ks[11], (DEPTH, D_MODEL, 2 * D_FF), D_MODEL),
        "conv_w": w(ks[12], (DEPTH, CONV_WIDTH, 2 * D_FF), CONV_WIDTH),
        "conv_b": 0.01 * jax.random.normal(ks[13], (DEPTH, 2 * D_FF), f32),
        "w_down": w(ks[14], (DEPTH, D_FF, D_MODEL), D_FF),
        "norm_ffn_post": gain(ks[15], (DEPTH, D_MODEL)),
        "w_ple": w(ks[16], (DEPTH, PLE_DIM, D_MODEL), PLE_DIM),
        "w_ple_gate": w(ks[17], (DEPTH, D_MODEL, D_MODEL), D_MODEL),
        "norm_ple_post": gain(ks[18], (DEPTH, D_MODEL)),
    }


def _fwd_reference(x, p, norm_mix_pre, w_in, w_attn_branch, w_pool_group, pool_scale, w_pool_branch, w_out,
              norm_mix_post, norm_ffn_pre, w_up, conv_w, conv_b, w_down, norm_ffn_post, w_ple, w_ple_gate,
              norm_ple_post):
    B, S, _ = x.shape
    splits = [ATTN_WIDTH, 2 * ATTN_WIDTH, 3 * ATTN_WIDTH, 3 * ATTN_WIDTH + POOL_WIDTH,
              3 * ATTN_WIDTH + POOL_WIDTH + D_MODEL]
    for i in range(DEPTH):
        h = rmsnorm(x, norm_mix_pre[i])
        proj = h @ w_in[i]
        q, k, v, u, g_attn, g_pool = jnp.split(proj, splits, axis=-1)
        q = q.reshape(B, S, N_ATTN_HEADS, HEAD_DIM)
        k = k.reshape(B, S, N_ATTN_HEADS, HEAD_DIM)
        v = v.reshape(B, S, N_ATTN_HEADS, HEAD_DIM)
        y_attn = stick_breaking_attention(q, k, v) @ w_attn_branch[i]

        pooled = multiscale_causal_pool(u).astype(u.dtype)
        pooled = jnp.einsum('bsgc,gcd->bsgd', pooled, w_pool_group[i]).reshape(B, S, POOL_WIDTH)
        y_pool = (pooled * pool_scale[i]) @ w_pool_branch[i]

        mixed = jax.nn.sigmoid(g_attn) * y_attn + jax.nn.sigmoid(g_pool) * y_pool
        x = x + rmsnorm(mixed @ w_out[i], norm_mix_post[i])

        h = rmsnorm(x, norm_ffn_pre[i])
        up = causal_depthwise_conv(h @ w_up[i], conv_w[i], conv_b[i])
        gate, val = jnp.split(up, 2, axis=-1)
        y_ffn = (jax.nn.gelu(gate, approximate=True) * val) @ w_down[i]
        x = x + rmsnorm(y_ffn, norm_ffn_post[i])

        e = p[i] @ w_ple[i]
        x = x + rmsnorm(jax.nn.sigmoid(x @ w_ple_gate[i]) * e, norm_ple_post[i])
    return x


import jax as _jax
import jax.numpy as _jnp

TWIN_FORMAT = 'train_step'
FWD_PARAMS = ['x', 'p', 'norm_mix_pre', 'w_in', 'w_attn_branch', 'w_pool_group', 'pool_scale', 'w_pool_branch', 'w_out', 'norm_mix_post', 'norm_ffn_pre', 'w_up', 'conv_w', 'conv_b', 'w_down', 'norm_ffn_post', 'w_ple', 'w_ple_gate', 'norm_ple_post']
TWIN_WEIGHTS = ['norm_mix_pre', 'w_in', 'w_attn_branch', 'w_pool_group', 'pool_scale', 'w_pool_branch', 'w_out', 'norm_mix_post', 'norm_ffn_pre', 'w_up', 'conv_w', 'conv_b', 'w_down', 'norm_ffn_post', 'w_ple', 'w_ple_gate', 'norm_ple_post']
TWIN_DIFF_INPUT = 'x'
TWIN_INPUTS = ['x', 'p', 'norm_mix_pre', 'w_in', 'w_attn_branch', 'w_pool_group', 'pool_scale', 'w_pool_branch', 'w_out', 'norm_mix_post', 'norm_ffn_pre', 'w_up', 'conv_w', 'conv_b', 'w_down', 'norm_ffn_post', 'w_ple', 'w_ple_gate', 'norm_ple_post', 'loss_target', 'm_norm_mix_pre', 'm_w_in', 'm_w_attn_branch', 'm_w_pool_group', 'm_pool_scale', 'm_w_pool_branch', 'm_w_out', 'm_norm_mix_post', 'm_norm_ffn_pre', 'm_w_up', 'm_conv_w', 'm_conv_b', 'm_w_down', 'm_norm_ffn_post', 'm_w_ple', 'm_w_ple_gate', 'm_norm_ple_post', 'v_norm_mix_pre', 'v_w_in', 'v_w_attn_branch', 'v_w_pool_group', 'v_pool_scale', 'v_w_pool_branch', 'v_w_out', 'v_norm_mix_post', 'v_norm_ffn_pre', 'v_w_up', 'v_conv_w', 'v_conv_b', 'v_w_down', 'v_norm_ffn_post', 'v_w_ple', 'v_w_ple_gate', 'v_norm_ple_post']
TWIN_OUTPUTS = ['loss', 'grad_x', 'grad_norm_mix_pre', 'grad_w_in', 'grad_w_attn_branch', 'grad_w_pool_group', 'grad_pool_scale', 'grad_w_pool_branch', 'grad_w_out', 'grad_norm_mix_post', 'grad_norm_ffn_pre', 'grad_w_up', 'grad_conv_w', 'grad_conv_b', 'grad_w_down', 'grad_norm_ffn_post', 'grad_w_ple', 'grad_w_ple_gate', 'grad_norm_ple_post', 'delta_norm_mix_pre', 'delta_w_in', 'delta_w_attn_branch', 'delta_w_pool_group', 'delta_pool_scale', 'delta_w_pool_branch', 'delta_w_out', 'delta_norm_mix_post', 'delta_norm_ffn_pre', 'delta_w_up', 'delta_conv_w', 'delta_conv_b', 'delta_w_down', 'delta_norm_ffn_post', 'delta_w_ple', 'delta_w_ple_gate', 'delta_norm_ple_post', 'new_m_norm_mix_pre', 'new_m_w_in', 'new_m_w_attn_branch', 'new_m_w_pool_group', 'new_m_pool_scale', 'new_m_w_pool_branch', 'new_m_w_out', 'new_m_norm_mix_post', 'new_m_norm_ffn_pre', 'new_m_w_up', 'new_m_conv_w', 'new_m_conv_b', 'new_m_w_down', 'new_m_norm_ffn_post', 'new_m_w_ple', 'new_m_w_ple_gate', 'new_m_norm_ple_post', 'new_v_norm_mix_pre', 'new_v_w_in', 'new_v_w_attn_branch', 'new_v_w_pool_group', 'new_v_pool_scale', 'new_v_w_pool_branch', 'new_v_w_out', 'new_v_norm_mix_post', 'new_v_norm_ffn_pre', 'new_v_w_up', 'new_v_conv_w', 'new_v_conv_b', 'new_v_w_down', 'new_v_norm_ffn_post', 'new_v_w_ple', 'new_v_w_ple_gate', 'new_v_norm_ple_post']
TWIN_LEAF_KINDS = {'loss': 'loss', 'grad_x': 'grad_x', 'grad_norm_mix_pre': 'grad_w', 'grad_w_in': 'grad_w', 'grad_w_attn_branch': 'grad_w', 'grad_w_pool_group': 'grad_w', 'grad_pool_scale': 'grad_w', 'grad_w_pool_branch': 'grad_w', 'grad_w_out': 'grad_w', 'grad_norm_mix_post': 'grad_w', 'grad_norm_ffn_pre': 'grad_w', 'grad_w_up': 'grad_w', 'grad_conv_w': 'grad_w', 'grad_conv_b': 'grad_w', 'grad_w_down': 'grad_w', 'grad_norm_ffn_post': 'grad_w', 'grad_w_ple': 'grad_w', 'grad_w_ple_gate': 'grad_w', 'grad_norm_ple_post': 'grad_w', 'delta_norm_mix_pre': 'delta_w', 'delta_w_in': 'delta_w', 'delta_w_attn_branch': 'delta_w', 'delta_w_pool_group': 'delta_w', 'delta_pool_scale': 'delta_w', 'delta_w_pool_branch': 'delta_w', 'delta_w_out': 'delta_w', 'delta_norm_mix_post': 'delta_w', 'delta_norm_ffn_pre': 'delta_w', 'delta_w_up': 'delta_w', 'delta_conv_w': 'delta_w', 'delta_conv_b': 'delta_w', 'delta_w_down': 'delta_w', 'delta_norm_ffn_post': 'delta_w', 'delta_w_ple': 'delta_w', 'delta_w_ple_gate': 'delta_w', 'delta_norm_ple_post': 'delta_w', 'new_m_norm_mix_pre': 'new_m', 'new_m_w_in': 'new_m', 'new_m_w_attn_branch': 'new_m', 'new_m_w_pool_group': 'new_m', 'new_m_pool_scale': 'new_m', 'new_m_w_pool_branch': 'new_m', 'new_m_w_out': 'new_m', 'new_m_norm_mix_post': 'new_m', 'new_m_norm_ffn_pre': 'new_m', 'new_m_w_up': 'new_m', 'new_m_conv_w': 'new_m', 'new_m_conv_b': 'new_m', 'new_m_w_down': 'new_m', 'new_m_norm_ffn_post': 'new_m', 'new_m_w_ple': 'new_m', 'new_m_w_ple_gate': 'new_m', 'new_m_norm_ple_post': 'new_m', 'new_v_norm_mix_pre': 'new_v', 'new_v_w_in': 'new_v', 'new_v_w_attn_branch': 'new_v', 'new_v_w_pool_group': 'new_v', 'new_v_pool_scale': 'new_v', 'new_v_w_pool_branch': 'new_v', 'new_v_w_out': 'new_v', 'new_v_norm_mix_post': 'new_v', 'new_v_norm_ffn_pre': 'new_v', 'new_v_w_up': 'new_v', 'new_v_conv_w': 'new_v', 'new_v_conv_b': 'new_v', 'new_v_w_down': 'new_v', 'new_v_norm_ffn_post': 'new_v', 'new_v_w_ple': 'new_v', 'new_v_w_ple_gate': 'new_v', 'new_v_norm_ple_post': 'new_v'}


def _forward(args):
    return _fwd_reference(*[args[k] for k in FWD_PARAMS])


def _output_shape():
    def fwd():
        inp = _fwd_setup_inputs(0)
        return _fwd_reference(*[inp[k] for k in FWD_PARAMS])
    out = _jax.eval_shape(fwd)
    return out.shape, out.dtype

N_MICROBATCH = 1
ADAM_LR = 0.001
ADAM_B1 = 0.9
ADAM_B2 = 0.999
ADAM_EPS = 1e-08
ADAM_WD = 0.01
ADAM_STEP = 10
PER_EXAMPLE_BATCH_AXIS = {'x': 0, 'p': 1, 'loss_target': 0}
SHARED_INPUTS = []
_WEIGHT_DTYPES = {'norm_mix_pre': _jnp.float32, 'w_in': _jnp.float32, 'w_attn_branch': _jnp.float32, 'w_pool_group': _jnp.float32, 'pool_scale': _jnp.float32, 'w_pool_branch': _jnp.float32, 'w_out': _jnp.float32, 'norm_mix_post': _jnp.float32, 'norm_ffn_pre': _jnp.float32, 'w_up': _jnp.float32, 'conv_w': _jnp.float32, 'conv_b': _jnp.float32, 'w_down': _jnp.float32, 'norm_ffn_post': _jnp.float32, 'w_ple': _jnp.float32, 'w_ple_gate': _jnp.float32, 'norm_ple_post': _jnp.float32}
MOMENT_SCALE = {'norm_mix_pre': 3.696446e-01, 'w_in': 1.740225e-01, 'w_attn_branch': 1.629865e-01, 'w_pool_group': 3.940019e-01, 'pool_scale': 4.086880e-01, 'w_pool_branch': 2.847715e-01, 'w_out': 3.342885e-01, 'norm_mix_post': 1.598804e+01, 'norm_ffn_pre': 3.005071e-01, 'w_up': 1.261157e-01, 'conv_w': 1.342247e-01, 'conv_b': 1.755939e-01, 'w_down': 2.254433e-01, 'norm_ffn_post': 1.599538e+01, 'w_ple': 1.789301e-01, 'w_ple_gate': 9.522313e-02, 'norm_ple_post': 1.607253e+01}


def _to_microbatches(a, axis):
    t = _jnp.moveaxis(a, axis, 0)
    t = t.reshape((N_MICROBATCH, t.shape[0] // N_MICROBATCH) + t.shape[1:])
    return _jnp.moveaxis(t, 1, axis + 1)


def setup_inputs(seed: int = 0) -> dict:
    inp = _fwd_setup_inputs(seed)
    key = _jax.random.fold_in(_jax.random.key(seed), 7919)
    shape, _ = _output_shape()
    out = dict(inp)
    out["loss_target"] = _jax.random.normal(_jax.random.fold_in(key, 0), shape, _jnp.float32)
    for i, name in enumerate(TWIN_WEIGHTS):
        w = inp[name].astype(_jnp.float32)
        if MOMENT_SCALE is None:
            s = _jnp.sqrt(_jnp.mean(_jnp.square(w)) + 1e-30)
        else:
            s = MOMENT_SCALE[name]
        km, kv = _jax.random.split(_jax.random.fold_in(key, i + 1))
        out[name] = w
        out["m_" + name] = s * _jax.random.normal(km, w.shape, _jnp.float32)
        out["v_" + name] = (s * s) * _jax.random.uniform(kv, w.shape, _jnp.float32, 0.5, 1.5)
    if N_MICROBATCH > 1:
        for name, axis in PER_EXAMPLE_BATCH_AXIS.items():
            out[name] = _to_microbatches(out[name], axis)
    return {'x': out['x'], 'p': out['p'], 'norm_mix_pre': out['norm_mix_pre'], 'w_in': out['w_in'], 'w_attn_branch': out['w_attn_branch'], 'w_pool_group': out['w_pool_group'], 'pool_scale': out['pool_scale'], 'w_pool_branch': out['w_pool_branch'], 'w_out': out['w_out'], 'norm_mix_post': out['norm_mix_post'], 'norm_ffn_pre': out['norm_ffn_pre'], 'w_up': out['w_up'], 'conv_w': out['conv_w'], 'conv_b': out['conv_b'], 'w_down': out['w_down'], 'norm_ffn_post': out['norm_ffn_post'], 'w_ple': out['w_ple'], 'w_ple_gate': out['w_ple_gate'], 'norm_ple_post': out['norm_ple_post'], 'loss_target': out['loss_target'], 'm_norm_mix_pre': out['m_norm_mix_pre'], 'm_w_in': out['m_w_in'], 'm_w_attn_branch': out['m_w_attn_branch'], 'm_w_pool_group': out['m_w_pool_group'], 'm_pool_scale': out['m_pool_scale'], 'm_w_pool_branch': out['m_w_pool_branch'], 'm_w_out': out['m_w_out'], 'm_norm_mix_post': out['m_norm_mix_post'], 'm_norm_ffn_pre': out['m_norm_ffn_pre'], 'm_w_up': out['m_w_up'], 'm_conv_w': out['m_conv_w'], 'm_conv_b': out['m_conv_b'], 'm_w_down': out['m_w_down'], 'm_norm_ffn_post': out['m_norm_ffn_post'], 'm_w_ple': out['m_w_ple'], 'm_w_ple_gate': out['m_w_ple_gate'], 'm_norm_ple_post': out['m_norm_ple_post'], 'v_norm_mix_pre': out['v_norm_mix_pre'], 'v_w_in': out['v_w_in'], 'v_w_attn_branch': out['v_w_attn_branch'], 'v_w_pool_group': out['v_w_pool_group'], 'v_pool_scale': out['v_pool_scale'], 'v_w_pool_branch': out['v_w_pool_branch'], 'v_w_out': out['v_w_out'], 'v_norm_mix_post': out['v_norm_mix_post'], 'v_norm_ffn_pre': out['v_norm_ffn_pre'], 'v_w_up': out['v_w_up'], 'v_conv_w': out['v_conv_w'], 'v_conv_b': out['v_conv_b'], 'v_w_down': out['v_w_down'], 'v_norm_ffn_post': out['v_norm_ffn_post'], 'v_w_ple': out['v_w_ple'], 'v_w_ple_gate': out['v_w_ple_gate'], 'v_norm_ple_post': out['v_norm_ple_post']}


def _loss(weights, diff, rest, loss_target):
    with _jax.named_scope("forward"):
        args = {**rest, TWIN_DIFF_INPUT: diff, **{k: w.astype(_WEIGHT_DTYPES[k]) for k, w in weights.items()}}
        y = _forward(args)
    with _jax.named_scope("loss_head"):
        err = _jnp.square(y.astype(_jnp.float32) - loss_target)
        return 0.5 * _jnp.sum(_jnp.mean(err, axis=-1)) if err.ndim else 0.5 * err


def _adamw(w, g, m, v):
    m = ADAM_B1 * m + (1.0 - ADAM_B1) * g
    v = ADAM_B2 * v + (1.0 - ADAM_B2) * _jnp.square(g)
    m_hat = m / (1.0 - ADAM_B1 ** ADAM_STEP)
    v_hat = v / (1.0 - ADAM_B2 ** ADAM_STEP)
    delta = -ADAM_LR * (m_hat / (_jnp.sqrt(v_hat) + ADAM_EPS) + ADAM_WD * w)
    return delta, m, v


def reference(x, p, norm_mix_pre, w_in, w_attn_branch, w_pool_group, pool_scale, w_pool_branch, w_out, norm_mix_post, norm_ffn_pre, w_up, conv_w, conv_b, w_down, norm_ffn_post, w_ple, w_ple_gate, norm_ple_post, loss_target, m_norm_mix_pre, m_w_in, m_w_attn_branch, m_w_pool_group, m_pool_scale, m_w_pool_branch, m_w_out, m_norm_mix_post, m_norm_ffn_pre, m_w_up, m_conv_w, m_conv_b, m_w_down, m_norm_ffn_post, m_w_ple, m_w_ple_gate, m_norm_ple_post, v_norm_mix_pre, v_w_in, v_w_attn_branch, v_w_pool_group, v_pool_scale, v_w_pool_branch, v_w_out, v_norm_mix_post, v_norm_ffn_pre, v_w_up, v_conv_w, v_conv_b, v_w_down, v_norm_ffn_post, v_w_ple, v_w_ple_gate, v_norm_ple_post):
    given = dict(x=x, p=p, norm_mix_pre=norm_mix_pre, w_in=w_in, w_attn_branch=w_attn_branch, w_pool_group=w_pool_group, pool_scale=pool_scale, w_pool_branch=w_pool_branch, w_out=w_out, norm_mix_post=norm_mix_post, norm_ffn_pre=norm_ffn_pre, w_up=w_up, conv_w=conv_w, conv_b=conv_b, w_down=w_down, norm_ffn_post=norm_ffn_post, w_ple=w_ple, w_ple_gate=w_ple_gate, norm_ple_post=norm_ple_post, loss_target=loss_target, m_norm_mix_pre=m_norm_mix_pre, m_w_in=m_w_in, m_w_attn_branch=m_w_attn_branch, m_w_pool_group=m_w_pool_group, m_pool_scale=m_pool_scale, m_w_pool_branch=m_w_pool_branch, m_w_out=m_w_out, m_norm_mix_post=m_norm_mix_post, m_norm_ffn_pre=m_norm_ffn_pre, m_w_up=m_w_up, m_conv_w=m_conv_w, m_conv_b=m_conv_b, m_w_down=m_w_down, m_norm_ffn_post=m_norm_ffn_post, m_w_ple=m_w_ple, m_w_ple_gate=m_w_ple_gate, m_norm_ple_post=m_norm_ple_post, v_norm_mix_pre=v_norm_mix_pre, v_w_in=v_w_in, v_w_attn_branch=v_w_attn_branch, v_w_pool_group=v_w_pool_group, v_pool_scale=v_pool_scale, v_w_pool_branch=v_w_pool_branch, v_w_out=v_w_out, v_norm_mix_post=v_norm_mix_post, v_norm_ffn_pre=v_norm_ffn_pre, v_w_up=v_w_up, v_conv_w=v_conv_w, v_conv_b=v_conv_b, v_w_down=v_w_down, v_norm_ffn_post=v_norm_ffn_post, v_w_ple=v_w_ple, v_w_ple_gate=v_w_ple_gate, v_norm_ple_post=v_norm_ple_post)
    weights = {n: given[n] for n in TWIN_WEIGHTS}
    shared = {n: given[n] for n in SHARED_INPUTS}
    per_example = {n: given[n] for n in ['x', 'p']}
    grad_fn = _jax.value_and_grad(_loss, argnums=(0, 1))

    def one_microbatch(ex, loss_target):
        ex = dict(ex)
        diff = ex.pop(TWIN_DIFF_INPUT)
        return grad_fn(weights, diff, {**shared, **ex}, loss_target)

    if N_MICROBATCH == 1:
        loss, (grad_w, grad_x) = one_microbatch(per_example, given["loss_target"])
    else:
        def body(carry, xs):
            loss_sum, grad_sum = carry
            l_k, (gw_k, gx_k) = one_microbatch(xs[0], xs[1])
            with _jax.named_scope("update"):
                return (loss_sum + l_k, _jax.tree.map(_jnp.add, grad_sum, gw_k)), gx_k

        init = (_jnp.zeros((), _jnp.float32), _jax.tree.map(_jnp.zeros_like, weights))
        (loss, grad_w), grad_x = _jax.lax.scan(body, init, (per_example, given["loss_target"]))
    with _jax.named_scope("update"):
        delta_w, new_m, new_v = {}, {}, {}
        for n in TWIN_WEIGHTS:
            delta_w[n], new_m[n], new_v[n] = _adamw(weights[n], grad_w[n], given["m_" + n], given["v_" + n])
    return (loss, grad_x, *[grad_w[n] for n in TWIN_WEIGHTS], *[delta_w[n] for n in TWIN_WEIGHTS],
            *[new_m[n] for n in TWIN_WEIGHTS], *[new_v[n] for n in TWIN_WEIGHTS])
```

```python
import functools

import jax
import jax.numpy as jnp
from jax import lax
from jax.experimental import pallas as pl
from jax.experimental.pallas import tpu as pltpu

F32 = jnp.float32
BF16 = jnp.bfloat16

HEAD_DIM = 128
POOL_WINDOWS = (2, 4, 8, 16)
EPS = 1e-6
GELU_C = 0.7978845608028654
GELU_A = 0.044715

ADAM_LR = 0.001
ADAM_B1 = 0.9
ADAM_B2 = 0.999
ADAM_EPS = 1e-08
ADAM_WD = 0.01
ADAM_STEP = 10

N_CHIPS = 4
N_DEVICES = 8
VMEM_LIMIT_BYTES = 52 * 1024 * 1024
ATTN_BLOCK = 256
POOL_TILE = 128
MESH = pl.DeviceIdType.MESH
ANY = pl.BlockSpec(memory_space=pl.ANY)


def _sds(shape, dtype):
    return jax.ShapeDtypeStruct(tuple(shape), dtype)


def _tile(dim, pref, mult=128):
    t = min(pref, dim) // mult * mult
    while t >= mult:
        if dim % t == 0:
            return t
        t -= mult
    return dim


def _params(*semantics):
    return pltpu.CompilerParams(dimension_semantics=semantics or None, vmem_limit_bytes=VMEM_LIMIT_BYTES)


def _sigmoid(v):
    return 1.0 / (1.0 + jnp.exp(-v))


def _rstd(v):
    return lax.rsqrt(jnp.mean(v * v, axis=-1, keepdims=True) + EPS)


def _rms_bwd(dy, gain, v):
    r = _rstd(v)
    vh = v * r
    gy = dy * gain
    return r * (gy - vh * jnp.mean(gy * vh, axis=-1, keepdims=True)), dy * vh


def _matmul(name, a, b, grid, a_spec, b_spec, o_spec, out, dims, acc_shape, add=None, add_spec=None):
    nk = grid[2]

    def body(*refs):
        if add is None:
            a_ref, b_ref, o_ref, acc = refs
            c_ref = None
        else:
            a_ref, b_ref, c_ref, o_ref, acc = refs
        kk = pl.program_id(2)
        part = lax.dot_general(a_ref[...].astype(BF16), b_ref[...].astype(BF16), (dims, ((), ())),
                               preferred_element_type=F32)

        @pl.when(kk == 0)
        def _():
            acc[...] = part

        @pl.when(kk > 0)
        def _():
            acc[...] += part

        @pl.when(kk == nk - 1)
        def _():
            r = acc[...]
            if c_ref is not None:
                r = r + c_ref[...]
            o_ref[...] = r.astype(o_ref.dtype)

    operands = (a, b) if add is None else (a, b, add)
    in_specs = [a_spec, b_spec] if add is None else [a_spec, b_spec, add_spec]
    return pl.pallas_call(
        body, name=name, grid=grid, in_specs=in_specs, out_specs=o_spec, out_shape=out,
        scratch_shapes=[pltpu.VMEM(acc_shape, F32)],
        compiler_params=_params("parallel", "parallel", "arbitrary"),
    )(*operands)


def _mm_nn(name, a, b3, out_dtype, j0=0, nj=None, c0=0, cn=None, tm=1024, tn=1024, tk=1024):
    m, k = a.shape
    nj_all, kb, width = b3.shape
    assert kb == k
    if cn is None:
        cn = width
        nj = nj_all - j0 if nj is None else nj
    else:
        nj = 1
    n = nj * cn
    tm, tn, tk = _tile(m, tm, 8), _tile(cn, tn), _tile(k, tk)
    assert c0 % tn == 0
    nb, cb = cn // tn, c0 // tn
    return _matmul(
        name, a, b3, (m // tm, n // tn, k // tk),
        pl.BlockSpec((tm, tk), lambda i, j, kk: (i, kk)),
        pl.BlockSpec((None, tk, tn), lambda i, j, kk: (j0 + j // nb, kk, cb + j % nb)),
        pl.BlockSpec((tm, tn), lambda i, j, kk: (i, j)),
        _sds((m, n), out_dtype), ((1,), (0,)), (tm, tn))


def _mm_nt(name, a, b3, out_dtype, add=None, tm=1024, tn=1024, tk=1024):
    m, kc = a.shape
    nj, n, kj = b3.shape
    assert kc == nj * kj
    tm, tn, tk = _tile(m, tm, 8), _tile(n, tn), _tile(kj, tk)
    kb = kj // tk
    o_spec = pl.BlockSpec((tm, tn), lambda i, j, kk: (i, j))
    return _matmul(
        name, a, b3, (m // tm, n // tn, kc // tk),
        pl.BlockSpec((tm, tk), lambda i, j, kk: (i, kk)),
        pl.BlockSpec((None, tn, tk), lambda i, j, kk: (kk // kb, j, kk % kb)),
        o_spec, _sds((m, n), out_dtype), ((1,), (1,)), (tm, tn), add=add, add_spec=o_spec)


def _mm_tn(name, a, b, nj, out_dtype, tm=1024, tn=1024, ts=1024):
    s, m = a.shape
    s2, n = b.shape
    assert s == s2 and n % nj == 0
    width = n // nj
    tm, tn, ts = _tile(m, tm), _tile(width, tn), _tile(s, ts)
    nb = width // tn
    return _matmul(
        name, a, b, (m // tm, n // tn, s // ts),
        pl.BlockSpec((ts, tm), lambda i, j, kk: (kk, i)),
        pl.BlockSpec((ts, tn), lambda i, j, kk: (kk, j)),
        pl.BlockSpec((None, tm, tn), lambda i, j, kk: (j // nb, i, j % nb)),
        _sds((nj, m, width), out_dtype), ((0,), (0,)), (tm, tn))


def _group_nn(name, a, w, out_dtype, transpose_w=False, tm=1024):
    s, gc = a.shape
    g, c, _ = w.shape
    tm = _tile(s, tm, 8)
    return _matmul(
        name, a, w, (s // tm, g, 1),
        pl.BlockSpec((tm, c), lambda i, j, kk: (i, j)),
        pl.BlockSpec((None, c, c), lambda i, j, kk: (j, 0, 0)),
        pl.BlockSpec((tm, c), lambda i, j, kk: (i, j)),
        _sds((s, gc), out_dtype), ((1,), (1,)) if transpose_w else ((1,), (0,)), (tm, c))


def _group_tn(name, a, b, g, out_dtype, ts=1024):
    s, gc = a.shape
    c = gc // g
    ts = _tile(s, ts, 8)
    return _matmul(
        name, a, b, (g, 1, s // ts),
        pl.BlockSpec((ts, c), lambda i, j, kk: (kk, i)),
        pl.BlockSpec((ts, c), lambda i, j, kk: (kk, i)),
        pl.BlockSpec((None, c, c), lambda i, j, kk: (i, 0, 0)),
        _sds((g, c, c), out_dtype), ((0,), (0,)), (c, c))


def _row(tm, d):
    return pl.BlockSpec((tm, d), lambda i: (i, 0))


def _vec(d):
    return pl.BlockSpec((1, d), lambda i: (0, 0))


def _rows_call(name, body, ins, in_specs, outs, out_specs, steps, accumulates):
    return pl.pallas_call(
        body, name=name, grid=(steps,), in_specs=in_specs, out_specs=out_specs, out_shape=outs,
        compiler_params=_params("arbitrary" if accumulates else "parallel"),
    )(*ins)


def _accumulate(ref, value):
    @pl.when(pl.program_id(0) == 0)
    def _():
        ref[...] = value

    @pl.when(pl.program_id(0) > 0)
    def _():
        ref[...] += value


def _colsum(v):
    return jnp.sum(v, axis=0, keepdims=True)


def _rms_fwd(x, gain, tm=256):
    s, d = x.shape

    def body(x_ref, g_ref, h_ref):
        v = x_ref[...]
        h_ref[...] = (v * _rstd(v) * g_ref[...]).astype(BF16)

    return _rows_call("rms_fwd", body, (x, gain), [_row(tm, d), _vec(d)], _sds((s, d), BF16), _row(tm, d),
                      s // tm, False)


def _mix_fwd(ga, gp, ya, yp, tm=256):
    s, d = ga.shape

    def body(ga_ref, gp_ref, ya_ref, yp_ref, o_ref):
        o_ref[...] = (_sigmoid(ga_ref[...]) * ya_ref[...] + _sigmoid(gp_ref[...]) * yp_ref[...]).astype(BF16)

    return _rows_call("mix_fwd", body, (ga, gp, ya, yp), [_row(tm, d)] * 4, _sds((s, d), BF16), _row(tm, d),
                      s // tm, False)


def _scale_cols(v, scale, tm=256):
    s, d = v.shape

    def body(v_ref, s_ref, o_ref):
        o_ref[...] = (v_ref[...] * s_ref[...]).astype(BF16)

    return _rows_call("pool_scale_fwd", body, (v, scale), [_row(tm, d), _vec(d)], _sds((s, d), BF16), _row(tm, d),
                      s // tm, False)


def _post_pre(x, y, gain_post, gain_pre, tm=256):
    s, d = x.shape

    def body(x_ref, y_ref, gp_ref, gn_ref, x2_ref, h_ref):
        y = y_ref[...]
        x2 = x_ref[...] + y * _rstd(y) * gp_ref[...]
        x2_ref[...] = x2
        h_ref[...] = (x2 * _rstd(x2) * gn_ref[...]).astype(BF16)

    return _rows_call("post_pre", body, (x, y, gain_post, gain_pre), [_row(tm, d), _row(tm, d), _vec(d), _vec(d)],
                      (_sds((s, d), F32), _sds((s, d), BF16)), (_row(tm, d), _row(tm, d)), s // tm, False)


def _post(x, y, gain_post, tm=256):
    s, d = x.shape

    def body(x_ref, y_ref, gp_ref, o_ref):
        y = y_ref[...]
        o_ref[...] = x_ref[...] + y * _rstd(y) * gp_ref[...]

    return _rows_call("post", body, (x, y, gain_post), [_row(tm, d), _row(tm, d), _vec(d)], _sds((s, d), F32),
                      _row(tm, d), s // tm, False)


def _final(x3, t, e, gain, target, tm=256):
    s, d = x3.shape

    def body(x_ref, t_ref, e_ref, g_ref, y_ref, dx_ref, loss_ref):
        pe = _sigmoid(t_ref[...]) * e_ref[...]
        diff = x_ref[...] + pe * _rstd(pe) * g_ref[...] - y_ref[...]
        dx_ref[...] = diff * (1.0 / d)
        part = 0.5 * jnp.sum(jnp.mean(diff * diff, axis=-1, keepdims=True), axis=0, keepdims=True)
        _accumulate(loss_ref, jnp.broadcast_to(part, loss_ref.shape))

    return _rows_call("final", body, (x3, t, e, gain, target),
                      [_row(tm, d), _row(tm, d), _row(tm, d), _vec(d), _row(tm, d)],
                      (_sds((s, d), F32), _sds((8, 128), F32)),
                      (_row(tm, d), pl.BlockSpec((8, 128), lambda i: (0, 0))), s // tm, True)


def _bwd_ple(dx4, t, e, gain, tm=256):
    s, d = dx4.shape

    def body(dx_ref, t_ref, e_ref, g_ref, dt_ref, de_ref, dg_ref):
        sg = _sigmoid(t_ref[...])
        ev = e_ref[...]
        dpe, dgain = _rms_bwd(dx_ref[...], g_ref[...], sg * ev)
        de_ref[...] = (dpe * sg).astype(BF16)
        dt_ref[...] = (dpe * ev * sg * (1.0 - sg)).astype(BF16)
        _accumulate(dg_ref, _colsum(dgain))

    return _rows_call("bwd_ple", body, (dx4, t, e, gain), [_row(tm, d), _row(tm, d), _row(tm, d), _vec(d)],
                      (_sds((s, d), BF16), _sds((s, d), BF16), _sds((1, d), F32)),
                      (_row(tm, d), _row(tm, d), _vec(d)), s // tm, True)


def _bwd_post(dx, y, gain, tm=256):
    s, d = dx.shape

    def body(dx_ref, y_ref, g_ref, dy_ref, dg_ref):
        dy, dgain = _rms_bwd(dx_ref[...], g_ref[...], y_ref[...])
        dy_ref[...] = dy.astype(BF16)
        _accumulate(dg_ref, _colsum(dgain))

    return _rows_call("bwd_post", body, (dx, y, gain), [_row(tm, d), _row(tm, d), _vec(d)],
                      (_sds((s, d), BF16), _sds((1, d), F32)), (_row(tm, d), _vec(d)), s // tm, True)


def _bwd_mid(dx3, dh2, x2, gain_pre, mo, gain_post, tm=128):
    s, d = dx3.shape

    def body(dx3_ref, dh_ref, x2_ref, gn_ref, mo_ref, gp_ref, dx2_ref, dmo_ref, dgn_ref, dgp_ref):
        dv, dgn = _rms_bwd(dh_ref[...], gn_ref[...], x2_ref[...])
        dx2 = dx3_ref[...] + dv
        dx2_ref[...] = dx2
        dmo, dgp = _rms_bwd(dx2, gp_ref[...], mo_ref[...])
        dmo_ref[...] = dmo.astype(BF16)
        _accumulate(dgn_ref, _colsum(dgn))
        _accumulate(dgp_ref, _colsum(dgp))

    return _rows_call("bwd_mid", body, (dx3, dh2, x2, gain_pre, mo, gain_post),
                      [_row(tm, d), _row(tm, d), _row(tm, d), _vec(d), _row(tm, d), _vec(d)],
                      (_sds((s, d), F32), _sds((s, d), BF16), _sds((1, d), F32), _sds((1, d), F32)),
                      (_row(tm, d), _row(tm, d), _vec(d), _vec(d)), s // tm, True)


def _bwd_first(dx2, dh1, x, gain, tm=256):
    s, d = dx2.shape

    def body(dx2_ref, dh_ref, x_ref, g_ref, dx_ref, dg_ref):
        dv, dgain = _rms_bwd(dh_ref[...], g_ref[...], x_ref[...])
        dx_ref[...] = dx2_ref[...] + dv
        _accumulate(dg_ref, _colsum(dgain))

    return _rows_call("bwd_first", body, (dx2, dh1, x, gain), [_row(tm, d), _row(tm, d), _row(tm, d), _vec(d)],
                      (_sds((s, d), F32), _sds((1, d), F32)), (_row(tm, d), _vec(d)), s // tm, True)


def _bwd_mix(dmixed, ga, gp, ya, yp, tm=128):
    s, d = dmixed.shape

    def body(dm_ref, ga_ref, gp_ref, ya_ref, yp_ref, dya_ref, dyp_ref, dga_ref, dgp_ref):
        dm = dm_ref[...]
        sa, sp = _sigmoid(ga_ref[...]), _sigmoid(gp_ref[...])
        dya_ref[...] = (dm * sa).astype(BF16)
        dyp_ref[...] = (dm * sp).astype(BF16)
        dga_ref[...] = (dm * ya_ref[...] * sa * (1.0 - sa)).astype(BF16)
        dgp_ref[...] = (dm * yp_ref[...] * sp * (1.0 - sp)).astype(BF16)

    return _rows_call("bwd_mix", body, (dmixed, ga, gp, ya, yp), [_row(tm, d)] * 5,
                      (_sds((s, d), BF16),) * 4, (_row(tm, d),) * 4, s // tm, False)


def _bwd_pool_scale(dps, pg, scale, tm=256):
    s, d = dps.shape

    def body(d_ref, pg_ref, s_ref, dpg_ref, ds_ref):
        dv = d_ref[...]
        dpg_ref[...] = (dv * s_ref[...]).astype(BF16)
        _accumulate(ds_ref, _colsum(dv * pg_ref[...]))

    return _rows_call("bwd_pool_scale", body, (dps, pg, scale), [_row(tm, d), _row(tm, d), _vec(d)],
                      (_sds((s, d), BF16), _sds((1, d), F32)), (_row(tm, d), _vec(d)), s // tm, True)


def _shift_down(v, k, rows):
    return jnp.where(rows >= k, pltpu.roll(v, k, 0), 0.0)


def _shift_up(v, k, rows):
    s = v.shape[0]
    return jnp.where(rows < s - k, pltpu.roll(v, s - k, 0), 0.0)


def _window_pick(group, sums, rows):
    total = sums[-1]
    width = jnp.full((), POOL_WINDOWS[-1], jnp.int32)
    for g in range(len(POOL_WINDOWS) - 2, -1, -1):
        total = jnp.where(group == g, sums[g], total)
        width = jnp.where(group == g, POOL_WINDOWS[g], width)
    return total, jnp.minimum(rows + 1, width).astype(F32)


def _doubling(v, shift, rows):
    sums, k = [], 1
    for _ in POOL_WINDOWS:
        v = v + shift(v, k, rows)
        sums.append(v)
        k *= 2
    return sums


def _pool_fwd(u):
    s, width = u.shape
    per_group = width // len(POOL_WINDOWS) // POOL_TILE

    def body(u_ref, o_ref):
        v = u_ref[...]
        rows = lax.broadcasted_iota(jnp.int32, (s, 1), 0)
        total, count = _window_pick(pl.program_id(0), _doubling(v, _shift_down, rows), rows)
        o_ref[...] = (total / count - v).astype(BF16)

    spec = pl.BlockSpec((s, POOL_TILE), lambda g, j: (0, g * per_group + j))
    return pl.pallas_call(body, name="pool_fwd", grid=(len(POOL_WINDOWS), per_group), in_specs=[spec], out_specs=spec,
                          out_shape=_sds((s, width), BF16), compiler_params=_params("parallel", "parallel"))(u)


def _pool_bwd(dpooled):
    s, width = dpooled.shape
    per_group = width // len(POOL_WINDOWS) // POOL_TILE

    def body(d_ref, o_ref):
        dv = d_ref[...]
        rows = lax.broadcasted_iota(jnp.int32, (s, 1), 0)
        group = pl.program_id(0)
        _, count = _window_pick(group, [dv] * len(POOL_WINDOWS), rows)
        total, _ = _window_pick(group, _doubling(dv / count, _shift_up, rows), rows)
        o_ref[...] = (total - dv).astype(BF16)

    spec = pl.BlockSpec((s, POOL_TILE), lambda g, j: (0, g * per_group + j))
    return pl.pallas_call(body, name="pool_bwd", grid=(len(POOL_WINDOWS), per_group), in_specs=[spec], out_specs=spec,
                          out_shape=_sds((s, width), BF16), compiler_params=_params("parallel", "parallel"))(dpooled)


def _conv(v, w_ref, b_ref, rows):
    out = b_ref[...] + _shift_down(v, 2, rows) * w_ref[0:1, :]
    out = out + _shift_down(v, 1, rows) * w_ref[1:2, :]
    return out + v * w_ref[2:3, :]


def _gelu_parts(v):
    th = jnp.tanh(GELU_C * (v + GELU_A * v * v * v))
    return th, 0.5 * v * (1.0 + th)


def _conv_gelu_fwd(up, conv_w, conv_b, tc=128):
    s, f2 = up.shape
    f = f2 // 2
    nb = f // tc

    def body(g_ref, v_ref, wg_ref, wv_ref, bg_ref, bv_ref, o_ref):
        rows = lax.broadcasted_iota(jnp.int32, (s, 1), 0)
        _, gl = _gelu_parts(_conv(g_ref[...], wg_ref, bg_ref, rows))
        o_ref[...] = (gl * _conv(v_ref[...], wv_ref, bv_ref, rows)).astype(BF16)

    lo = lambda r: pl.BlockSpec((r, tc), lambda j: (0, j))
    hi = lambda r: pl.BlockSpec((r, tc), lambda j: (0, j + nb))
    return pl.pallas_call(
        body, name="conv_gelu_fwd", grid=(nb,), in_specs=[lo(s), hi(s), lo(3), hi(3), lo(1), hi(1)], out_specs=lo(s),
        out_shape=_sds((s, f), BF16), compiler_params=_params("parallel"))(up, up, conv_w, conv_w, conv_b, conv_b)


def _conv_gelu_bwd(up, dact, conv_w, conv_b, tc=128):
    s, f2 = up.shape
    f = f2 // 2
    nb = f // tc

    def body(g_ref, v_ref, d_ref, wg_ref, wv_ref, bg_ref, bv_ref, dg_ref, dv_ref, dwg_ref, dwv_ref, dbg_ref, dbv_ref):
        rows = lax.broadcasted_iota(jnp.int32, (s, 1), 0)
        dact = d_ref[...].astype(F32)

        def back(pre, dc, w_ref, dpre_ref, dw_ref, db_ref):
            db_ref[...] = _colsum(dc)
            dw_ref[0:1, :] = _colsum(dc * _shift_down(pre, 2, rows))
            dw_ref[1:2, :] = _colsum(dc * _shift_down(pre, 1, rows))
            dw_ref[2:3, :] = _colsum(dc * pre)
            dpre = dc * w_ref[2:3, :] + _shift_up(dc, 1, rows) * w_ref[1:2, :] + _shift_up(dc, 2, rows) * w_ref[0:1, :]
            dpre_ref[...] = dpre.astype(BF16)

        gate, val = g_ref[...], v_ref[...]
        cg = _conv(gate, wg_ref, bg_ref, rows)
        cv = _conv(val, wv_ref, bv_ref, rows)
        th, gl = _gelu_parts(cg)
        dgl = 0.5 * (1.0 + th) + 0.5 * cg * (1.0 - th * th) * GELU_C * (1.0 + 3.0 * GELU_A * cg * cg)
        back(val, dact * gl, wv_ref, dv_ref, dwv_ref, dbv_ref)
        back(gate, dact * cv * dgl, wg_ref, dg_ref, dwg_ref, dbg_ref)

    lo = lambda r: pl.BlockSpec((r, tc), lambda j: (0, j))
    hi = lambda r: pl.BlockSpec((r, tc), lambda j: (0, j + nb))
    return pl.pallas_call(
        body, name="conv_gelu_bwd", grid=(nb,),
        in_specs=[lo(s), hi(s), lo(s), lo(3), hi(3), lo(1), hi(1)],
        out_specs=(lo(s), lo(s), lo(3), lo(3), lo(1), lo(1)),
        out_shape=(_sds((s, f), BF16), _sds((s, f), BF16), _sds((3, f), F32), _sds((3, f), F32),
                   _sds((1, f), F32), _sds((1, f), F32)),
        compiler_params=_params("parallel"))(up, up, dact, conv_w, conv_w, conv_b, conv_b)


def _dot(a, b, dims):
    return lax.dot_general(a, b, (dims, ((), ())), preferred_element_type=F32)


def _running_sums(v, tri):
    hi = v.astype(BF16)
    lo = (v - hi.astype(F32)).astype(BF16)
    return _dot(hi, tri, ((1,), (0,))) + _dot(lo, tri, ((1,), (0,)))


def _attn_scores(q, kt, diagonal, causal):
    z = _dot(q, kt, ((1,), (1,))) * (HEAD_DIM ** -0.5)
    mask = jnp.logical_or(jnp.logical_not(diagonal), causal)
    e = jnp.exp(-jnp.abs(z))
    return z, e, mask, jnp.where(mask, -(jnp.maximum(z, 0.0) + jnp.log(1.0 + e)), 0.0)


def _attn_consts(t):
    rows = lax.broadcasted_iota(jnp.int32, (t, t), 0)
    cols = lax.broadcasted_iota(jnp.int32, (t, t), 1)
    return (rows >= cols).astype(BF16), (rows <= cols).astype(BF16), cols < rows


def _attn_specs(s, t, heads):
    q = pl.BlockSpec((t, HEAD_DIM), lambda h, i: (i, h))
    k = pl.BlockSpec((s, HEAD_DIM), lambda h, i: (0, heads + h))
    whole = pl.BlockSpec((s, HEAD_DIM), lambda h, i: (0, h))
    return q, k, whole


def _attn_fwd(qk, v, t=ATTN_BLOCK):
    s, width = v.shape
    heads = width // HEAD_DIM
    t = min(t, s)

    def body(q_ref, k_ref, v_ref, o_ref, lsum_ref):
        i = pl.program_id(1)
        q = q_ref[...]
        tri, _, causal = _attn_consts(t)

        def step(n, carry):
            acc, log_rest = carry
            start = pl.multiple_of((i - n) * t, t)
            z, _, mask, log_1m_beta = _attn_scores(q, k_ref[pl.ds(start, t), :], n == 0, causal)
            total = _running_sums(log_1m_beta, tri) + log_rest
            a = jnp.where(mask, jnp.exp(z + total), 0.0)
            return acc + _dot(a.astype(BF16), v_ref[pl.ds(start, t), :], ((1,), (0,))), total[:, 0:1]

        acc, log_all = lax.fori_loop(0, i + 1, step, (jnp.zeros((t, HEAD_DIM), F32), jnp.zeros((t, 1), F32)))
        o_ref[...] = acc.astype(BF16)
        lsum_ref[...] = log_all

    q_spec, k_spec, whole = _attn_specs(s, t, heads)
    return pl.pallas_call(
        body, name="attn_fwd", grid=(heads, s // t), in_specs=[q_spec, k_spec, whole],
        out_specs=(q_spec, _attn_row_spec(t)), out_shape=(_sds((s, width), BF16), _sds((heads, s, 1), F32)),
        compiler_params=_params("parallel", "parallel"))(qk, qk, v)


def _attn_row_spec(t):
    return pl.BlockSpec((None, t, 1), lambda h, i: (h, i, 0))


def _attn_bwd(qk, v, lsum, do, t=ATTN_BLOCK):
    s, width = v.shape
    heads = width // HEAD_DIM
    t = min(t, s)

    def body(q_ref, k_ref, v_ref, lsum_ref, do_ref, dq_ref, dk_ref, dv_ref):
        i = pl.program_id(1)

        @pl.when(i == 0)
        def _():
            dk_ref[...] = jnp.zeros_like(dk_ref)
            dv_ref[...] = jnp.zeros_like(dv_ref)

        q = q_ref[...]
        do_blk = do_ref[...]
        _, triu, causal = _attn_consts(t)

        def step(n, carry):
            dq, log_rest, g_before = carry
            start = pl.multiple_of(n * t, t)
            kt, vt = k_ref[pl.ds(start, t), :], v_ref[pl.ds(start, t), :]
            z, e, mask, log_1m_beta = _attn_scores(q, kt, n == i, causal)
            upto = _running_sums(log_1m_beta, triu)
            a = jnp.where(mask, jnp.exp(z + log_rest - upto + log_1m_beta), 0.0)
            g = a * _dot(do_blk, vt, ((1,), (1,)))
            g_upto = _running_sums(g, triu) + g_before
            sig = jnp.where(z >= 0.0, 1.0, e) / (1.0 + e)
            dz = (jnp.where(mask, g - sig * g_upto, 0.0) * (HEAD_DIM ** -0.5)).astype(BF16)
            dk_ref[pl.ds(start, t), :] += _dot(dz, q, ((0,), (0,)))
            dv_ref[pl.ds(start, t), :] += _dot(a.astype(BF16), do_blk, ((0,), (0,)))
            return dq + _dot(dz, kt, ((1,), (0,))), log_rest - upto[:, t - 1:t], g_upto[:, t - 1:t]

        dq, _, _ = lax.fori_loop(0, i + 1, step,
                                 (jnp.zeros((t, HEAD_DIM), F32), lsum_ref[...], jnp.zeros((t, 1), F32)))
        dq_ref[...] = dq.astype(BF16)

    q_spec, k_spec, whole = _attn_specs(s, t, heads)
    return pl.pallas_call(
        body, name="attn_bwd", grid=(heads, s // t), in_specs=[q_spec, k_spec, whole, _attn_row_spec(t), q_spec],
        out_specs=(q_spec, whole, whole),
        out_shape=(_sds((s, width), BF16), _sds((s, width), F32), _sds((s, width), F32)),
        compiler_params=_params("parallel", "arbitrary"))(qk, qk, v, lsum, do)


def _as_rows(a):
    return a.reshape(-1, a.shape[-1])


def _flat_call(name, body, ins, out_dtypes, block_bytes=1 << 20):
    shape = ins[0].shape
    rows, cols = _as_rows(ins[0]).shape
    tr = _tile(rows, max(8, block_bytes // (4 * cols)), 8)
    spec = pl.BlockSpec((tr, cols), lambda i: (i, 0))
    outs = pl.pallas_call(
        body, name=name, grid=(rows // tr,), in_specs=[spec] * len(ins), out_specs=tuple([spec] * len(out_dtypes)),
        out_shape=tuple(_sds((rows, cols), dt) for dt in out_dtypes), compiler_params=_params("parallel"),
    )(*[_as_rows(a) for a in ins])
    return [o.reshape(shape) for o in outs]


def _add_pair(a, b):
    def body(a_ref, b_ref, o_ref):
        o_ref[...] = (a_ref[...].astype(F32) + b_ref[...].astype(F32)).astype(BF16)

    return _flat_call("add_pair", body, (a, b), (BF16,))[0]


def _sum_leading(a, name):
    n = a.shape[0]
    shape = a.shape[1:]
    cols = shape[-1]
    rows = a.size // (n * cols)
    tr = _tile(rows, max(8, (1 << 20) // (4 * cols)), 8)

    def body(a_ref, o_ref):
        total = a_ref[0].astype(F32)
        for j in range(1, n):
            total = total + a_ref[j].astype(F32)
        o_ref[...] = total

    out = pl.pallas_call(
        body, name=name, grid=(rows // tr,), in_specs=[pl.BlockSpec((n, tr, cols), lambda i: (0, i, 0))],
        out_specs=pl.BlockSpec((tr, cols), lambda i: (i, 0)), out_shape=_sds((rows, cols), F32),
        compiler_params=_params("parallel"))(a.reshape(n, rows, cols))
    return out.reshape(shape)


def _adamw(w, g, m, v):
    def body(w_ref, g_ref, m_ref, v_ref, d_ref, nm_ref, nv_ref):
        gv = g_ref[...]
        nm = ADAM_B1 * m_ref[...] + (1.0 - ADAM_B1) * gv
        nv = ADAM_B2 * v_ref[...] + (1.0 - ADAM_B2) * jnp.square(gv)
        nm_ref[...] = nm
        nv_ref[...] = nv
        m_hat = nm / (1.0 - ADAM_B1 ** ADAM_STEP)
        v_hat = nv / (1.0 - ADAM_B2 ** ADAM_STEP)
        d_ref[...] = -ADAM_LR * (m_hat / (jnp.sqrt(v_hat) + ADAM_EPS) + ADAM_WD * w_ref[...])

    return _flat_call("adamw", body, (w, g, m, v), (F32, F32, F32), block_bytes=1 << 19)


def _mesh_position():
    return lax.axis_index("x"), lax.axis_index("y"), lax.axis_index("c")


def _other_chips(x, y):
    return [(1 - x, y), (x, 1 - y), (1 - x, 1 - y)]


def _chip_index(chip):
    return 2 * chip[0] + chip[1]


def _remote(src, dst, send_sem, recv_sem, device):
    return pltpu.make_async_remote_copy(src_ref=src, dst_ref=dst, send_sem=send_sem, recv_sem=recv_sem,
                                        device_id=device, device_id_type=MESH)


def _gather_weights(shards):
    n = len(shards)
    halves = [a.shape[0] // 2 for a in shards]
    assert all(a.shape[0] % 2 == 0 for a in shards)

    def body(*refs):
        ins, outs = refs[:n], refs[n:2 * n]
        ici_send, ici_recv, d2d_send, d2d_recv, local_sem = refs[2 * n:]
        x, y, c = _mesh_position()
        me = _chip_index((x, y))
        sibling = (x, y, 1 - c)
        chips = _other_chips(x, y)

        def half(w, chip, which):
            return outs[w].at[chip, pl.ds(which * halves[w], halves[w])]

        def from_chip(w, j):
            k = 3 * w + j
            return _remote(half(w, _chip_index(chips[j]), c), half(w, _chip_index(chips[j]), c),
                           ici_send.at[k], ici_recv.at[k], (*chips[j], c))

        def pass_on(w, j, which):
            k = 3 * w + j
            return _remote(half(w, _chip_index(chips[j]), which), half(w, _chip_index(chips[j]), which),
                           d2d_send.at[k], d2d_recv.at[k], sibling)

        local = [pltpu.make_async_copy(ins[w], outs[w].at[me], local_sem.at[w]) for w in range(n)]
        sends = []
        for w in range(n):
            local[w].start()
            for j in range(3):
                k = 3 * w + j
                sends.append(_remote(ins[w].at[pl.ds(c * halves[w], halves[w])], half(w, me, c),
                                     ici_send.at[k], ici_recv.at[k], (*chips[j], c)))
                sends[-1].start()
        passed = []
        for w in range(n):
            for j in range(3):
                from_chip(w, j).wait_recv()
                passed.append(pass_on(w, j, c))
                passed[-1].start()
        for w in range(n):
            for j in range(3):
                pass_on(w, j, 1 - c).wait_recv()
        for cp in sends + passed:
            cp.wait_send()
        for cp in local:
            cp.wait()

    return pl.pallas_call(
        body, name="gather_weights", in_specs=[ANY] * n, out_specs=tuple([ANY] * n),
        out_shape=tuple(_sds((N_CHIPS,) + a.shape, a.dtype) for a in shards),
        scratch_shapes=[pltpu.SemaphoreType.DMA((3 * n,))] * 4 + [pltpu.SemaphoreType.DMA((n,))],
    )(*shards)


def _swap_halves(grads):
    n = len(grads)
    halves = [a.shape[1] // 2 for a in grads]
    assert all(a.shape[1] % 2 == 0 for a in grads)

    def body(*refs):
        ins, kept, got = refs[:n], refs[n:2 * n], refs[2 * n:3 * n]
        send_sem, recv_sem, local_sem = refs[3 * n:]
        x, y, c = _mesh_position()
        sibling = (x, y, 1 - c)

        def part(w, which):
            return ins[w].at[pl.ds(0, N_CHIPS), pl.ds(which * halves[w], halves[w])]

        local = [pltpu.make_async_copy(part(w, c), kept[w], local_sem.at[w]) for w in range(n)]
        swaps = [_remote(part(w, 1 - c), got[w], send_sem.at[w], recv_sem.at[w], sibling) for w in range(n)]
        for w in range(n):
            local[w].start()
            swaps[w].start()
        for w in range(n):
            swaps[w].wait()
            local[w].wait()

    half_shapes = [(N_CHIPS, h) + a.shape[2:] for a, h in zip(grads, halves)]
    outs = pl.pallas_call(
        body, name="swap_halves", in_specs=[ANY] * n, out_specs=tuple([ANY] * (2 * n)),
        out_shape=tuple(_sds(sh, a.dtype) for sh, a in zip(half_shapes, grads)) * 2,
        scratch_shapes=[pltpu.SemaphoreType.DMA((n,))] * 3,
    )(*grads)
    return outs[:n], outs[n:]


def _scatter_to_chips(parts):
    n = len(parts)

    def body(*refs):
        ins, outs = refs[:n], refs[n:2 * n]
        send_sem, recv_sem, local_sem = refs[2 * n:]
        x, y, c = _mesh_position()
        me = _chip_index((x, y))
        chips = _other_chips(x, y)
        local = [pltpu.make_async_copy(ins[w].at[me], outs[w].at[me], local_sem.at[w]) for w in range(n)]
        sends = []
        for w in range(n):
            local[w].start()
            for j in range(3):
                k = 3 * w + j
                sends.append(_remote(ins[w].at[_chip_index(chips[j])], outs[w].at[me], send_sem.at[k], recv_sem.at[k],
                                     (*chips[j], c)))
                sends[-1].start()
        for w in range(n):
            for j in range(3):
                k = 3 * w + j
                there = outs[w].at[_chip_index(chips[j])]
                _remote(there, there, send_sem.at[k], recv_sem.at[k], (*chips[j], c)).wait_recv()
        for cp in sends:
            cp.wait_send()
        for cp in local:
            cp.wait()

    return pl.pallas_call(
        body, name="scatter_to_chips", in_specs=[ANY] * n, out_specs=tuple([ANY] * n),
        out_shape=tuple(_sds(a.shape, a.dtype) for a in parts),
        scratch_shapes=[pltpu.SemaphoreType.DMA((3 * n,))] * 2 + [pltpu.SemaphoreType.DMA((n,))],
    )(*parts)


def _join_halves(mine):
    n = len(mine)

    def body(*refs):
        ins, outs = refs[:n], refs[n:2 * n]
        send_sem, recv_sem, local_sem = refs[2 * n:]
        x, y, c = _mesh_position()
        sibling = (x, y, 1 - c)
        local = [pltpu.make_async_copy(ins[w], outs[w].at[c], local_sem.at[w]) for w in range(n)]
        sends = [_remote(ins[w], outs[w].at[c], send_sem.at[w], recv_sem.at[w], sibling) for w in range(n)]
        for w in range(n):
            local[w].start()
            sends[w].start()
        for w in range(n):
            there = outs[w].at[1 - c]
            _remote(there, there, send_sem.at[w], recv_sem.at[w], sibling).wait_recv()
            sends[w].wait_send()
            local[w].wait()

    return pl.pallas_call(
        body, name="join_halves", in_specs=[ANY] * n, out_specs=tuple([ANY] * n),
        out_shape=tuple(_sds((2,) + a.shape, a.dtype) for a in mine),
        scratch_shapes=[pltpu.SemaphoreType.DMA((n,))] * 3,
    )(*mine)


def _gather_small(vec):
    length = vec.shape[1]
    flips = [(fx, fy, fc) for fx in (0, 1) for fy in (0, 1) for fc in (0, 1)][1:]

    def body(v_ref, o_ref, send_sem, recv_sem, local_sem):
        x, y, c = _mesh_position()
        me = 4 * x + 2 * y + c
        local = pltpu.make_async_copy(v_ref, o_ref.at[me], local_sem)
        local.start()
        peers = [(x ^ fx, y ^ fy, c ^ fc) for fx, fy, fc in flips]
        sends = [_remote(v_ref, o_ref.at[me], send_sem.at[k], recv_sem.at[k], peer) for k, peer in enumerate(peers)]
        for cp in sends:
            cp.start()
        for k, (px, py, pc) in enumerate(peers):
            there = o_ref.at[4 * px + 2 * py + pc]
            _remote(there, there, send_sem.at[k], recv_sem.at[k], peers[k]).wait_recv()
        for cp in sends:
            cp.wait_send()
        local.wait()

    return pl.pallas_call(
        body, name="gather_small", in_specs=[ANY], out_specs=ANY, out_shape=_sds((N_DEVICES, 1, length), F32),
        scratch_shapes=[pltpu.SemaphoreType.DMA((N_DEVICES - 1,))] * 2 + [pltpu.SemaphoreType.DMA],
    )(vec)


MATRICES = ("w_in", "w_attn_branch", "w_pool_group", "w_pool_branch", "w_out", "w_up", "w_down", "w_ple", "w_ple_gate")
VECTORS = ("norm_mix_pre", "pool_scale", "norm_mix_post", "norm_ffn_pre", "conv_b", "norm_ffn_post", "norm_ple_post")
WEIGHTS = ("norm_mix_pre", "w_in", "w_attn_branch", "w_pool_group", "pool_scale", "w_pool_branch", "w_out",
           "norm_mix_post", "norm_ffn_pre", "w_up", "conv_w", "conv_b", "w_down", "norm_ffn_post", "w_ple",
           "w_ple_gate", "norm_ple_post")


def _local_step(x, p, target, full, vec, conv_w):
    s, d = x.shape
    w_in, w_ab, w_pb, w_up, w_ple = (full[k] for k in ("w_in", "w_attn_branch", "w_pool_branch", "w_up", "w_ple"))
    groups = len(POOL_WINDOWS)
    w_pg = full["w_pool_group"].transpose(1, 0, 2, 3)
    w_pg = w_pg.reshape(groups, -1, w_pg.shape[-1])
    w_out, w_down, w_pleg = (full[k].reshape(1, -1, full[k].shape[-1]) for k in ("w_out", "w_down", "w_ple_gate"))
    aw = w_ab.shape[1]
    blk = w_in.shape[2]
    assert blk == 2 * aw == d and w_pb.shape[1] == aw

    h1 = _rms_fwd(x, vec["norm_mix_pre"])
    qk = _mm_nn("proj_qk", h1, w_in, BF16, j0=0, nj=1)
    v = _mm_nn("proj_v", h1, w_in, BF16, j0=1, c0=0, cn=aw)
    u = _mm_nn("proj_u", h1, w_in, F32, j0=1, c0=aw, cn=aw)
    ga = _mm_nn("proj_ga", h1, w_in, F32, j0=2, nj=1)
    gp = _mm_nn("proj_gp", h1, w_in, F32, j0=3, nj=1)
    attn, lsum = _attn_fwd(qk, v)
    ya = _mm_nn("attn_branch", attn, w_ab, F32)
    pooled = _pool_fwd(u)
    pg = _group_nn("pool_group", pooled, w_pg, F32)
    ps = _scale_cols(pg, vec["pool_scale"])
    yp = _mm_nn("pool_branch", ps, w_pb, F32)
    mixed = _mix_fwd(ga, gp, ya, yp)
    mo = _mm_nn("mix_out", mixed, w_out, F32)
    x2, h2 = _post_pre(x, mo, vec["norm_mix_post"], vec["norm_ffn_pre"])
    up = _mm_nn("ffn_up", h2, w_up, F32, tn=1408)
    act = _conv_gelu_fwd(up, conv_w, vec["conv_b"])
    yf = _mm_nn("ffn_down", act, w_down, F32)
    x3 = _post(x2, yf, vec["norm_ffn_post"])
    e = _mm_nn("ple_embed", p, w_ple, F32)
    t = _mm_nn("ple_gate", x3, w_pleg, F32)
    dx4, loss_rows = _final(x3, t, e, vec["norm_ple_post"], target)

    gvec, gmat = {}, {}
    dt, de, gvec["norm_ple_post"] = _bwd_ple(dx4, t, e, vec["norm_ple_post"])
    gmat["w_ple"] = _mm_tn("d_w_ple", p, de, N_CHIPS, BF16)
    gmat["w_ple_gate"] = _mm_tn("d_w_ple_gate", x3, dt, 1, BF16)
    dx3 = _mm_nt("d_x3", dt, w_pleg, F32, add=dx4)
    dyf, gvec["norm_ffn_post"] = _bwd_post(dx3, yf, vec["norm_ffn_post"])
    gmat["w_down"] = _mm_tn("d_w_down", act, dyf, 1, BF16)
    dact = _mm_nt("d_act", dyf, w_down, BF16)
    dup_g, dup_v, dcw_g, dcw_v, dcb_g, dcb_v = _conv_gelu_bwd(up, dact, conv_w, vec["conv_b"])
    dup = jnp.concatenate([dup_g, dup_v], axis=1)
    gconv_w = jnp.concatenate([dcw_g, dcw_v], axis=1)
    gvec["conv_b"] = jnp.concatenate([dcb_g, dcb_v], axis=1)
    gmat["w_up"] = _mm_tn("d_w_up", h2, dup, N_CHIPS, BF16, tn=1408)
    dh2 = _mm_nt("d_h2", dup, w_up, F32, tk=1408)
    dx2, dmo, gvec["norm_ffn_pre"], gvec["norm_mix_post"] = _bwd_mid(
        dx3, dh2, x2, vec["norm_ffn_pre"], mo, vec["norm_mix_post"])
    gmat["w_out"] = _mm_tn("d_w_out", mixed, dmo, 1, BF16)
    dmixed = _mm_nt("d_mixed", dmo, w_out, F32)
    dya, dyp, dga, dgp = _bwd_mix(dmixed, ga, gp, ya, yp)
    gmat["w_attn_branch"] = _mm_tn("d_w_attn_branch", attn, dya, N_CHIPS, BF16)
    dattn = _mm_nt("d_attn", dya, w_ab, BF16)
    gmat["w_pool_branch"] = _mm_tn("d_w_pool_branch", ps, dyp, N_CHIPS, BF16)
    dps = _mm_nt("d_ps", dyp, w_pb, F32)
    dpg, gvec["pool_scale"] = _bwd_pool_scale(dps, pg, vec["pool_scale"])
    g_pg = _group_tn("d_w_pool_group", pooled, dpg, groups, BF16)
    gmat["w_pool_group"] = g_pg.reshape(groups, N_CHIPS, -1, g_pg.shape[-1]).transpose(1, 0, 2, 3)
    dpooled = _group_nn("d_pooled", dpg, w_pg, F32, transpose_w=True)
    du = _pool_bwd(dpooled)
    dq, dk, dv = _attn_bwd(qk, v, lsum, dattn)
    dproj = jnp.concatenate([dq, dk.astype(BF16), dv.astype(BF16), du, dga, dgp], axis=1)
    gmat["w_in"] = _mm_tn("d_w_in", h1, dproj, N_CHIPS, BF16)
    dh1 = _mm_nt("d_h1", dproj, w_in, F32)
    grad_x, gvec["norm_mix_pre"] = _bwd_first(dx2, dh1, x, vec["norm_mix_pre"])
    for k in ("w_out", "w_down", "w_ple_gate"):
        gmat[k] = gmat[k].reshape(full[k].shape)
    return loss_rows[0, 0], grad_x, gmat, gvec, gconv_w


def _reduce_matrices(gmat):
    kept, got = _swap_halves([gmat[k] for k in MATRICES])
    pair = [_add_pair(a, b) for a, b in zip(kept, got)]
    from_chips = _scatter_to_chips(pair)
    halves = [_sum_leading(a, "sum_chips") for a in from_chips]
    joined = _join_halves(halves)
    return {k: a.reshape((-1,) + a.shape[2:]) for k, a in zip(MATRICES, joined)}


def kernel(x, p, norm_mix_pre, w_in, w_attn_branch, w_pool_group, pool_scale, w_pool_branch, w_out, norm_mix_post, norm_ffn_pre, w_up, conv_w, conv_b, w_down, norm_ffn_post, w_ple, w_ple_gate, norm_ple_post, loss_target, m_norm_mix_pre, m_w_in, m_w_attn_branch, m_w_pool_group, m_pool_scale, m_w_pool_branch, m_w_out, m_norm_mix_post, m_norm_ffn_pre, m_w_up, m_conv_w, m_conv_b, m_w_down, m_norm_ffn_post, m_w_ple, m_w_ple_gate, m_norm_ple_post, v_norm_mix_pre, v_w_in, v_w_attn_branch, v_w_pool_group, v_pool_scale, v_w_pool_branch, v_w_out, v_norm_mix_post, v_norm_ffn_pre, v_w_up, v_conv_w, v_conv_b, v_w_down, v_norm_ffn_post, v_w_ple, v_w_ple_gate, v_norm_ple_post):
    given = dict(locals())
    w = {k: given[k][0] for k in WEIGHTS}
    m = {k: given["m_" + k][0] for k in WEIGHTS}
    v = {k: given["v_" + k][0] for k in WEIGHTS}
    chip = 2 * lax.axis_index("x") + lax.axis_index("y")

    full = dict(zip(MATRICES, _gather_weights([w[k].astype(BF16) for k in MATRICES])))
    vec = {k: w[k].reshape(1, -1) for k in VECTORS}
    taps = _gather_small(w["conv_w"].reshape(1, -1))[:, 0][0::2]
    f2q = w["conv_w"].shape[1]
    conv_w_full = taps.reshape(N_CHIPS, 3, f2q).transpose(1, 0, 2).reshape(3, N_CHIPS * f2q)

    loss_rows, grad_x, gmat, gvec, gconv_w = _local_step(x[0], p[0, 0], loss_target[0], full, vec, conv_w_full)
    loss = lax.psum(loss_rows, ("x", "y", "c"))

    gw = _reduce_matrices(gmat)
    sizes = [gvec[k].shape[1] for k in VECTORS]
    small = jnp.concatenate([gvec[k] for k in VECTORS] + [gconv_w.reshape(1, -1)], axis=1)
    small = _sum_leading(_gather_small(small), "sum_small")
    offset = 0
    for k, n in zip(VECTORS, sizes):
        gw[k] = small[0, offset:offset + n]
        offset += n
    gconv_w = small[0, offset:].reshape(3, N_CHIPS, f2q)
    gw["conv_w"] = lax.dynamic_index_in_dim(gconv_w, chip, axis=1, keepdims=False)

    delta, new_m, new_v = {}, {}, {}
    for k in MATRICES:
        delta[k], new_m[k], new_v[k] = _adamw(w[k], gw[k], m[k], v[k])
    tiny = VECTORS + ("conv_w",)
    pack = lambda tree: jnp.concatenate([tree[k].reshape(1, -1) for k in tiny], axis=1)
    d_s, m_s, v_s = _adamw(pack(w), pack(gw), pack(m), pack(v))
    offset = 0
    for k in tiny:
        n = w[k].size
        delta[k], new_m[k], new_v[k] = (a[0, offset:offset + n].reshape(w[k].shape) for a in (d_s, m_s, v_s))
        offset += n

    lead = lambda tree: [tree[k].reshape((1,) + w[k].shape) for k in WEIGHTS]
    return (loss, grad_x[None], *lead(gw), *lead(delta), *lead(new_m), *lead(new_v))
```

```python
import functools

import jax
import jax.numpy as jnp
from jax import lax
from jax.experimental import pallas as pl
from jax.experimental.pallas import tpu as pltpu

F32 = jnp.float32
BF16 = jnp.bfloat16

HEAD_DIM = 128
POOL_WINDOWS = (2, 4, 8, 16)
EPS = 1e-6
GELU_C = 0.7978845608028654
GELU_A = 0.044715

ADAM_LR = 0.001
ADAM_B1 = 0.9
ADAM_B2 = 0.999
ADAM_EPS = 1e-08
ADAM_WD = 0.01
ADAM_STEP = 10

N_CHIPS = 4
N_DEVICES = 8
VMEM_LIMIT_BYTES = 52 * 1024 * 1024
ATTN_BLOCK = 256
ATTN_HEADS = 2
POOL_TILE = 128
DMA_PIECE_BYTES = 512 * 1024
ROWS_PER_TILE = 16
MESH = pl.DeviceIdType.MESH
ANY = pl.BlockSpec(memory_space=pl.ANY)


def _sds(shape, dtype):
    return jax.ShapeDtypeStruct(tuple(shape), dtype)


def _tile(dim, pref, mult=128):
    t = min(pref, dim) // mult * mult
    while t >= mult:
        if dim % t == 0:
            return t
        t -= mult
    return dim


def _params(*semantics):
    return pltpu.CompilerParams(dimension_semantics=semantics or None, vmem_limit_bytes=VMEM_LIMIT_BYTES)


def _sigmoid(v):
    return 1.0 / (1.0 + jnp.exp(-v))


def _rstd(v):
    return lax.rsqrt(jnp.mean(v * v, axis=-1, keepdims=True) + EPS)


def _rms_bwd(dy, gain, v):
    r = _rstd(v)
    vh = v * r
    gy = dy * gain
    return r * (gy - vh * jnp.mean(gy * vh, axis=-1, keepdims=True)), dy * vh


def _matmul(name, a, b, grid, a_spec, b_spec, o_spec, out, dims, acc_shape, add=None, add_spec=None):
    nk = grid[2]

    def body(*refs):
        if add is None:
            a_ref, b_ref, o_ref, acc = refs
            c_ref = None
        else:
            a_ref, b_ref, c_ref, o_ref, acc = refs
        kk = pl.program_id(2)
        part = lax.dot_general(a_ref[...].astype(BF16), b_ref[...].astype(BF16), (dims, ((), ())),
                               preferred_element_type=F32)

        @pl.when(kk == 0)
        def _():
            acc[...] = part

        @pl.when(kk > 0)
        def _():
            acc[...] += part

        @pl.when(kk == nk - 1)
        def _():
            r = acc[...]
            if c_ref is not None:
                r = r + c_ref[...]
            o_ref[...] = r.astype(o_ref.dtype)

    operands = (a, b) if add is None else (a, b, add)
    in_specs = [a_spec, b_spec] if add is None else [a_spec, b_spec, add_spec]
    return pl.pallas_call(
        body, name=name, grid=grid, in_specs=in_specs, out_specs=o_spec, out_shape=out,
        scratch_shapes=[pltpu.VMEM(acc_shape, F32)],
        compiler_params=_params("parallel", "parallel", "arbitrary"),
    )(*operands)


def _mm_nn(name, a, b3, out_dtype, j0=0, nj=None, c0=0, cn=None, tm=1024, tn=1024, tk=1024):
    m, k = a.shape
    nj_all, kb, width = b3.shape
    assert kb == k
    if cn is None:
        cn = width
        nj = nj_all - j0 if nj is None else nj
    else:
        nj = 1
    n = nj * cn
    tm, tn, tk = _tile(m, tm, 8), _tile(cn, tn), _tile(k, tk)
    assert c0 % tn == 0
    nb, cb = cn // tn, c0 // tn
    return _matmul(
        name, a, b3, (m // tm, n // tn, k // tk),
        pl.BlockSpec((tm, tk), lambda i, j, kk: (i, kk)),
        pl.BlockSpec((None, tk, tn), lambda i, j, kk: (j0 + j // nb, kk, cb + j % nb)),
        pl.BlockSpec((tm, tn), lambda i, j, kk: (i, j)),
        _sds((m, n), out_dtype), ((1,), (0,)), (tm, tn))


def _mm_nt(name, a, b3, out_dtype, add=None, tm=1024, tn=1024, tk=1024):
    m, kc = a.shape
    nj, n, kj = b3.shape
    assert kc == nj * kj
    tm, tn, tk = _tile(m, tm, 8), _tile(n, tn), _tile(kj, tk)
    kb = kj // tk
    o_spec = pl.BlockSpec((tm, tn), lambda i, j, kk: (i, j))
    return _matmul(
        name, a, b3, (m // tm, n // tn, kc // tk),
        pl.BlockSpec((tm, tk), lambda i, j, kk: (i, kk)),
        pl.BlockSpec((None, tn, tk), lambda i, j, kk: (kk // kb, j, kk % kb)),
        o_spec, _sds((m, n), out_dtype), ((1,), (1,)), (tm, tn), add=add, add_spec=o_spec)


def _mm_tn(name, a, b, nj, out_dtype, tm=1024, tn=1024, ts=1024):
    s, m = a.shape
    s2, n = b.shape
    assert s == s2 and n % nj == 0
    width = n // nj
    tm, tn, ts = _tile(m, tm), _tile(width, tn), _tile(s, ts)
    nb = width // tn
    return _matmul(
        name, a, b, (m // tm, n // tn, s // ts),
        pl.BlockSpec((ts, tm), lambda i, j, kk: (kk, i)),
        pl.BlockSpec((ts, tn), lambda i, j, kk: (kk, j)),
        pl.BlockSpec((None, tm, tn), lambda i, j, kk: (j // nb, i, j % nb)),
        _sds((nj, m, width), out_dtype), ((0,), (0,)), (tm, tn))


def _group_nn(name, a, w, out_dtype, transpose_w=False, tm=1024):
    s, gc = a.shape
    g, c, _ = w.shape
    tm = _tile(s, tm, 8)
    return _matmul(
        name, a, w, (s // tm, g, 1),
        pl.BlockSpec((tm, c), lambda i, j, kk: (i, j)),
        pl.BlockSpec((None, c, c), lambda i, j, kk: (j, 0, 0)),
        pl.BlockSpec((tm, c), lambda i, j, kk: (i, j)),
        _sds((s, gc), out_dtype), ((1,), (1,)) if transpose_w else ((1,), (0,)), (tm, c))


def _group_tn(name, a, b, g, out_dtype, ts=1024):
    s, gc = a.shape
    c = gc // g
    ts = _tile(s, ts, 8)
    return _matmul(
        name, a, b, (g, 1, s // ts),
        pl.BlockSpec((ts, c), lambda i, j, kk: (kk, i)),
        pl.BlockSpec((ts, c), lambda i, j, kk: (kk, i)),
        pl.BlockSpec((None, c, c), lambda i, j, kk: (i, 0, 0)),
        _sds((g, c, c), out_dtype), ((0,), (0,)), (c, c))


def _row(tm, d):
    return pl.BlockSpec((tm, d), lambda i: (i, 0))


def _vec(d):
    return pl.BlockSpec((1, d), lambda i: (0, 0))


def _rows_call(name, body, ins, in_specs, outs, out_specs, steps, accumulates):
    return pl.pallas_call(
        body, name=name, grid=(steps,), in_specs=in_specs, out_specs=out_specs, out_shape=outs,
        compiler_params=_params("arbitrary" if accumulates else "parallel"),
    )(*ins)


def _accumulate(ref, value):
    @pl.when(pl.program_id(0) == 0)
    def _():
        ref[...] = value

    @pl.when(pl.program_id(0) > 0)
    def _():
        ref[...] += value


def _colsum(v):
    return jnp.sum(v, axis=0, keepdims=True)


def _rms_fwd(x, gain, tm=256):
    s, d = x.shape

    def body(x_ref, g_ref, h_ref):
        v = x_ref[...]
        h_ref[...] = (v * _rstd(v) * g_ref[...]).astype(BF16)

    return _rows_call("rms_fwd", body, (x, gain), [_row(tm, d), _vec(d)], _sds((s, d), BF16), _row(tm, d),
                      s // tm, False)


def _mix_fwd(ga, gp, ya, yp, tm=256):
    s, d = ga.shape

    def body(ga_ref, gp_ref, ya_ref, yp_ref, o_ref):
        o_ref[...] = (_sigmoid(ga_ref[...]) * ya_ref[...] + _sigmoid(gp_ref[...]) * yp_ref[...]).astype(BF16)

    return _rows_call("mix_fwd", body, (ga, gp, ya, yp), [_row(tm, d)] * 4, _sds((s, d), BF16), _row(tm, d),
                      s // tm, False)


def _scale_cols(v, scale, tm=256):
    s, d = v.shape

    def body(v_ref, s_ref, o_ref):
        o_ref[...] = (v_ref[...] * s_ref[...]).astype(BF16)

    return _rows_call("pool_scale_fwd", body, (v, scale), [_row(tm, d), _vec(d)], _sds((s, d), BF16), _row(tm, d),
                      s // tm, False)


def _post_pre(x, y, gain_post, gain_pre, tm=256):
    s, d = x.shape

    def body(x_ref, y_ref, gp_ref, gn_ref, x2_ref, h_ref):
        y = y_ref[...]
        x2 = x_ref[...] + y * _rstd(y) * gp_ref[...]
        x2_ref[...] = x2
        h_ref[...] = (x2 * _rstd(x2) * gn_ref[...]).astype(BF16)

    return _rows_call("post_pre", body, (x, y, gain_post, gain_pre), [_row(tm, d), _row(tm, d), _vec(d), _vec(d)],
                      (_sds((s, d), F32), _sds((s, d), BF16)), (_row(tm, d), _row(tm, d)), s // tm, False)


def _post(x, y, gain_post, tm=256):
    s, d = x.shape

    def body(x_ref, y_ref, gp_ref, o_ref):
        y = y_ref[...]
        o_ref[...] = x_ref[...] + y * _rstd(y) * gp_ref[...]

    return _rows_call("post", body, (x, y, gain_post), [_row(tm, d), _row(tm, d), _vec(d)], _sds((s, d), F32),
                      _row(tm, d), s // tm, False)


def _final(x3, t, e, gain, target, tm=256):
    s, d = x3.shape

    def body(x_ref, t_ref, e_ref, g_ref, y_ref, dx_ref, loss_ref):
        pe = _sigmoid(t_ref[...]) * e_ref[...]
        diff = x_ref[...] + pe * _rstd(pe) * g_ref[...] - y_ref[...]
        dx_ref[...] = diff * (1.0 / d)
        part = 0.5 * jnp.sum(jnp.mean(diff * diff, axis=-1, keepdims=True), axis=0, keepdims=True)
        _accumulate(loss_ref, jnp.broadcast_to(part, loss_ref.shape))

    return _rows_call("final", body, (x3, t, e, gain, target),
                      [_row(tm, d), _row(tm, d), _row(tm, d), _vec(d), _row(tm, d)],
                      (_sds((s, d), F32), _sds((8, 128), F32)),
                      (_row(tm, d), pl.BlockSpec((8, 128), lambda i: (0, 0))), s // tm, True)


def _bwd_ple(dx4, t, e, gain, tm=256):
    s, d = dx4.shape

    def body(dx_ref, t_ref, e_ref, g_ref, dt_ref, de_ref, dg_ref):
        sg = _sigmoid(t_ref[...])
        ev = e_ref[...]
        dpe, dgain = _rms_bwd(dx_ref[...], g_ref[...], sg * ev)
        de_ref[...] = (dpe * sg).astype(BF16)
        dt_ref[...] = (dpe * ev * sg * (1.0 - sg)).astype(BF16)
        _accumulate(dg_ref, _colsum(dgain))

    return _rows_call("bwd_ple", body, (dx4, t, e, gain), [_row(tm, d), _row(tm, d), _row(tm, d), _vec(d)],
                      (_sds((s, d), BF16), _sds((s, d), BF16), _sds((1, d), F32)),
                      (_row(tm, d), _row(tm, d), _vec(d)), s // tm, True)


def _bwd_post(dx, y, gain, tm=256):
    s, d = dx.shape

    def body(dx_ref, y_ref, g_ref, dy_ref, dg_ref):
        dy, dgain = _rms_bwd(dx_ref[...], g_ref[...], y_ref[...])
        dy_ref[...] = dy.astype(BF16)
        _accumulate(dg_ref, _colsum(dgain))

    return _rows_call("bwd_post", body, (dx, y, gain), [_row(tm, d), _row(tm, d), _vec(d)],
                      (_sds((s, d), BF16), _sds((1, d), F32)), (_row(tm, d), _vec(d)), s // tm, True)


def _bwd_mid(dx3, dh2, x2, gain_pre, mo, gain_post, tm=128):
    s, d = dx3.shape

    def body(dx3_ref, dh_ref, x2_ref, gn_ref, mo_ref, gp_ref, dx2_ref, dmo_ref, dgn_ref, dgp_ref):
        dv, dgn = _rms_bwd(dh_ref[...], gn_ref[...], x2_ref[...])
        dx2 = dx3_ref[...] + dv
        dx2_ref[...] = dx2
        dmo, dgp = _rms_bwd(dx2, gp_ref[...], mo_ref[...])
        dmo_ref[...] = dmo.astype(BF16)
        _accumulate(dgn_ref, _colsum(dgn))
        _accumulate(dgp_ref, _colsum(dgp))

    return _rows_call("bwd_mid", body, (dx3, dh2, x2, gain_pre, mo, gain_post),
                      [_row(tm, d), _row(tm, d), _row(tm, d), _vec(d), _row(tm, d), _vec(d)],
                      (_sds((s, d), F32), _sds((s, d), BF16), _sds((1, d), F32), _sds((1, d), F32)),
                      (_row(tm, d), _row(tm, d), _vec(d), _vec(d)), s // tm, True)


def _bwd_first(dx2, dh1, x, gain, tm=256):
    s, d = dx2.shape

    def body(dx2_ref, dh_ref, x_ref, g_ref, dx_ref, dg_ref):
        dv, dgain = _rms_bwd(dh_ref[...], g_ref[...], x_ref[...])
        dx_ref[...] = dx2_ref[...] + dv
        _accumulate(dg_ref, _colsum(dgain))

    return _rows_call("bwd_first", body, (dx2, dh1, x, gain), [_row(tm, d), _row(tm, d), _row(tm, d), _vec(d)],
                      (_sds((s, d), F32), _sds((1, d), F32)), (_row(tm, d), _vec(d)), s // tm, True)


def _bwd_mix(dmixed, ga, gp, ya, yp, tm=128):
    s, d = dmixed.shape

    def body(dm_ref, ga_ref, gp_ref, ya_ref, yp_ref, dya_ref, dyp_ref, dga_ref, dgp_ref):
        dm = dm_ref[...]
        sa, sp = _sigmoid(ga_ref[...]), _sigmoid(gp_ref[...])
        dya_ref[...] = (dm * sa).astype(BF16)
        dyp_ref[...] = (dm * sp).astype(BF16)
        dga_ref[...] = (dm * ya_ref[...] * sa * (1.0 - sa)).astype(BF16)
        dgp_ref[...] = (dm * yp_ref[...] * sp * (1.0 - sp)).astype(BF16)

    return _rows_call("bwd_mix", body, (dmixed, ga, gp, ya, yp), [_row(tm, d)] * 5,
                      (_sds((s, d), BF16),) * 4, (_row(tm, d),) * 4, s // tm, False)


def _bwd_pool_scale(dps, pg, scale, tm=256):
    s, d = dps.shape

    def body(d_ref, pg_ref, s_ref, dpg_ref, ds_ref):
        dv = d_ref[...]
        dpg_ref[...] = (dv * s_ref[...]).astype(BF16)
        _accumulate(ds_ref, _colsum(dv * pg_ref[...]))

    return _rows_call("bwd_pool_scale", body, (dps, pg, scale), [_row(tm, d), _row(tm, d), _vec(d)],
                      (_sds((s, d), BF16), _sds((1, d), F32)), (_row(tm, d), _vec(d)), s // tm, True)


def _shift_down(v, k, rows):
    return jnp.where(rows >= k, pltpu.roll(v, k, 0), 0.0)


def _shift_up(v, k, rows):
    s = v.shape[0]
    return jnp.where(rows < s - k, pltpu.roll(v, s - k, 0), 0.0)


def _window_pick(group, sums, rows):
    total = sums[-1]
    width = jnp.full((), POOL_WINDOWS[-1], jnp.int32)
    for g in range(len(POOL_WINDOWS) - 2, -1, -1):
        total = jnp.where(group == g, sums[g], total)
        width = jnp.where(group == g, POOL_WINDOWS[g], width)
    return total, jnp.minimum(rows + 1, width).astype(F32)


def _doubling(v, shift, rows):
    sums, k = [], 1
    for _ in POOL_WINDOWS:
        v = v + shift(v, k, rows)
        sums.append(v)
        k *= 2
    return sums


def _pool_fwd(u):
    s, width = u.shape
    per_group = width // len(POOL_WINDOWS) // POOL_TILE

    def body(u_ref, o_ref):
        v = u_ref[...]
        rows = lax.broadcasted_iota(jnp.int32, (s, 1), 0)
        total, count = _window_pick(pl.program_id(0), _doubling(v, _shift_down, rows), rows)
        o_ref[...] = (total / count - v).astype(BF16)

    spec = pl.BlockSpec((s, POOL_TILE), lambda g, j: (0, g * per_group + j))
    return pl.pallas_call(body, name="pool_fwd", grid=(len(POOL_WINDOWS), per_group), in_specs=[spec], out_specs=spec,
                          out_shape=_sds((s, width), BF16), compiler_params=_params("parallel", "parallel"))(u)


def _pool_bwd(dpooled):
    s, width = dpooled.shape
    per_group = width // len(POOL_WINDOWS) // POOL_TILE

    def body(d_ref, o_ref):
        dv = d_ref[...]
        rows = lax.broadcasted_iota(jnp.int32, (s, 1), 0)
        group = pl.program_id(0)
        _, count = _window_pick(group, [dv] * len(POOL_WINDOWS), rows)
        total, _ = _window_pick(group, _doubling(dv / count, _shift_up, rows), rows)
        o_ref[...] = (total - dv).astype(BF16)

    spec = pl.BlockSpec((s, POOL_TILE), lambda g, j: (0, g * per_group + j))
    return pl.pallas_call(body, name="pool_bwd", grid=(len(POOL_WINDOWS), per_group), in_specs=[spec], out_specs=spec,
                          out_shape=_sds((s, width), BF16), compiler_params=_params("parallel", "parallel"))(dpooled)


def _conv(v, w_ref, b_ref, rows):
    out = b_ref[...] + _shift_down(v, 2, rows) * w_ref[0:1, :]
    out = out + _shift_down(v, 1, rows) * w_ref[1:2, :]
    return out + v * w_ref[2:3, :]


def _gelu_parts(v):
    th = jnp.tanh(GELU_C * (v + GELU_A * v * v * v))
    return th, 0.5 * v * (1.0 + th)


def _conv_gelu_fwd(up, conv_w, conv_b, tc=128):
    s, f2 = up.shape
    f = f2 // 2
    nb = f // tc

    def body(g_ref, v_ref, wg_ref, wv_ref, bg_ref, bv_ref, o_ref):
        rows = lax.broadcasted_iota(jnp.int32, (s, 1), 0)
        _, gl = _gelu_parts(_conv(g_ref[...], wg_ref, bg_ref, rows))
        o_ref[...] = (gl * _conv(v_ref[...], wv_ref, bv_ref, rows)).astype(BF16)

    lo = lambda r: pl.BlockSpec((r, tc), lambda j: (0, j))
    hi = lambda r: pl.BlockSpec((r, tc), lambda j: (0, j + nb))
    return pl.pallas_call(
        body, name="conv_gelu_fwd", grid=(nb,), in_specs=[lo(s), hi(s), lo(3), hi(3), lo(1), hi(1)], out_specs=lo(s),
        out_shape=_sds((s, f), BF16), compiler_params=_params("parallel"))(up, up, conv_w, conv_w, conv_b, conv_b)


def _conv_gelu_bwd(up, dact, conv_w, conv_b, tc=128):
    s, f2 = up.shape
    f = f2 // 2
    nb = f // tc

    def body(g_ref, v_ref, d_ref, wg_ref, wv_ref, bg_ref, bv_ref, dg_ref, dv_ref, dwg_ref, dwv_ref, dbg_ref, dbv_ref):
        rows = lax.broadcasted_iota(jnp.int32, (s, 1), 0)
        dact = d_ref[...].astype(F32)

        def back(pre, dc, w_ref, dpre_ref, dw_ref, db_ref):
            db_ref[...] = _colsum(dc)
            dw_ref[0:1, :] = _colsum(dc * _shift_down(pre, 2, rows))
            dw_ref[1:2, :] = _colsum(dc * _shift_down(pre, 1, rows))
            dw_ref[2:3, :] = _colsum(dc * pre)
            dpre = dc * w_ref[2:3, :] + _shift_up(dc, 1, rows) * w_ref[1:2, :] + _shift_up(dc, 2, rows) * w_ref[0:1, :]
            dpre_ref[...] = dpre.astype(BF16)

        gate, val = g_ref[...], v_ref[...]
        cg = _conv(gate, wg_ref, bg_ref, rows)
        cv = _conv(val, wv_ref, bv_ref, rows)
        th, gl = _gelu_parts(cg)
        dgl = 0.5 * (1.0 + th) + 0.5 * cg * (1.0 - th * th) * GELU_C * (1.0 + 3.0 * GELU_A * cg * cg)
        back(val, dact * gl, wv_ref, dv_ref, dwv_ref, dbv_ref)
        back(gate, dact * cv * dgl, wg_ref, dg_ref, dwg_ref, dbg_ref)

    lo = lambda r: pl.BlockSpec((r, tc), lambda j: (0, j))
    hi = lambda r: pl.BlockSpec((r, tc), lambda j: (0, j + nb))
    return pl.pallas_call(
        body, name="conv_gelu_bwd", grid=(nb,),
        in_specs=[lo(s), hi(s), lo(s), lo(3), hi(3), lo(1), hi(1)],
        out_specs=(lo(s), lo(s), lo(3), lo(3), lo(1), lo(1)),
        out_shape=(_sds((s, f), BF16), _sds((s, f), BF16), _sds((3, f), F32), _sds((3, f), F32),
                   _sds((1, f), F32), _sds((1, f), F32)),
        compiler_params=_params("parallel"))(up, up, dact, conv_w, conv_w, conv_b, conv_b)


def _dot(a, b, dims):
    return lax.dot_general(a, b, (dims, ((), ())), preferred_element_type=F32)


def _running_sums(v, tri):
    hi = v.astype(BF16)
    lo = (v - hi.astype(F32)).astype(BF16)
    return _dot(hi, tri, ((1,), (0,))) + _dot(lo, tri, ((1,), (0,)))


def _attn_scores(q, kt, mask):
    z = _dot(q, kt, ((1,), (1,))) * (HEAD_DIM ** -0.5)
    e = jnp.exp(-jnp.abs(z))
    log_1m_beta = -(jnp.maximum(z, 0.0) + jnp.log(1.0 + e))
    return z, e, log_1m_beta if mask is None else jnp.where(mask, log_1m_beta, 0.0)


def _masked(v, mask):
    return v if mask is None else jnp.where(mask, v, 0.0)


def _attn_consts(t):
    rows = lax.broadcasted_iota(jnp.int32, (t, t), 0)
    cols = lax.broadcasted_iota(jnp.int32, (t, t), 1)
    return (rows >= cols).astype(BF16), (rows <= cols).astype(BF16), cols < rows


def _attn_specs(s, t, heads):
    lanes = ATTN_HEADS * HEAD_DIM
    steps = heads // ATTN_HEADS
    q = pl.BlockSpec((t, lanes), lambda h, i: (i, h))
    k = pl.BlockSpec((s, lanes), lambda h, i: (0, steps + h))
    whole = pl.BlockSpec((s, lanes), lambda h, i: (0, h))
    rows = pl.BlockSpec((ATTN_HEADS, t, 1), lambda h, i: (h, i, 0))
    return q, k, whole, rows


def _head(p):
    return pl.ds(p * HEAD_DIM, HEAD_DIM)


def _attn_fwd(qk, v, t=ATTN_BLOCK):
    s, width = v.shape
    heads = width // HEAD_DIM
    t = min(t, s)
    assert heads % ATTN_HEADS == 0

    def body(q_ref, k_ref, v_ref, o_ref, lsum_ref):
        i = pl.program_id(1)
        tri, _, causal = _attn_consts(t)

        def block(p, start, carry, mask):
            acc, log_rest = carry
            z, _, log_1m_beta = _attn_scores(q_ref[:, _head(p)], k_ref[pl.ds(start, t), _head(p)], mask)
            total = _running_sums(log_1m_beta, tri) + log_rest
            a = _masked(jnp.exp(z + total), mask)
            return acc + _dot(a.astype(BF16), v_ref[pl.ds(start, t), _head(p)], ((1,), (0,))), total[:, 0:1]

        def step(n, carry):
            start = pl.multiple_of((i - 1 - n) * t, t)
            return tuple(block(p, start, carry[p], None) for p in range(ATTN_HEADS))

        zero = (jnp.zeros((t, HEAD_DIM), F32), jnp.zeros((t, 1), F32))
        carry = tuple(block(p, pl.multiple_of(i * t, t), zero, causal) for p in range(ATTN_HEADS))
        carry = lax.fori_loop(0, i, step, carry)
        for p in range(ATTN_HEADS):
            o_ref[:, _head(p)] = carry[p][0].astype(BF16)
            lsum_ref[p] = carry[p][1]

    q_spec, k_spec, whole, rows = _attn_specs(s, t, heads)
    return pl.pallas_call(
        body, name="attn_fwd", grid=(heads // ATTN_HEADS, s // t), in_specs=[q_spec, k_spec, whole],
        out_specs=(q_spec, rows), out_shape=(_sds((s, width), BF16), _sds((heads, s, 1), F32)),
        compiler_params=_params("parallel", "parallel"))(qk, qk, v)


def _attn_bwd(qk, v, lsum, do, t=ATTN_BLOCK):
    s, width = v.shape
    heads = width // HEAD_DIM
    t = min(t, s)

    def body(q_ref, k_ref, v_ref, lsum_ref, do_ref, dq_ref, dk_ref, dv_ref):
        i = pl.program_id(1)

        @pl.when(i == 0)
        def _():
            dk_ref[...] = jnp.zeros_like(dk_ref)
            dv_ref[...] = jnp.zeros_like(dv_ref)

        _, triu, causal = _attn_consts(t)

        def block(p, start, carry, mask):
            dq, log_rest, g_before = carry
            q, do_blk = q_ref[:, _head(p)], do_ref[:, _head(p)]
            kt, vt = k_ref[pl.ds(start, t), _head(p)], v_ref[pl.ds(start, t), _head(p)]
            z, e, log_1m_beta = _attn_scores(q, kt, mask)
            upto = _running_sums(log_1m_beta, triu)
            a = _masked(jnp.exp(z + log_rest - upto + log_1m_beta), mask)
            g = a * _dot(do_blk, vt, ((1,), (1,)))
            g_upto = _running_sums(g, triu) + g_before
            sig = jnp.where(z >= 0.0, 1.0, e) / (1.0 + e)
            dz = (_masked(g - sig * g_upto, mask) * (HEAD_DIM ** -0.5)).astype(BF16)
            dk_ref[pl.ds(start, t), _head(p)] += _dot(dz, q, ((0,), (0,)))
            dv_ref[pl.ds(start, t), _head(p)] += _dot(a.astype(BF16), do_blk, ((0,), (0,)))
            return dq + _dot(dz, kt, ((1,), (0,))), log_rest - upto[:, t - 1:t], g_upto[:, t - 1:t]

        def step(n, carry):
            start = pl.multiple_of(n * t, t)
            return tuple(block(p, start, carry[p], None) for p in range(ATTN_HEADS))

        carry = tuple((jnp.zeros((t, HEAD_DIM), F32), lsum_ref[p], jnp.zeros((t, 1), F32)) for p in range(ATTN_HEADS))
        carry = lax.fori_loop(0, i, step, carry)
        for p in range(ATTN_HEADS):
            dq, _, _ = block(p, pl.multiple_of(i * t, t), carry[p], causal)
            dq_ref[:, _head(p)] = dq.astype(BF16)

    q_spec, k_spec, whole, rows = _attn_specs(s, t, heads)
    return pl.pallas_call(
        body, name="attn_bwd", grid=(heads // ATTN_HEADS, s // t), in_specs=[q_spec, k_spec, whole, rows, q_spec],
        out_specs=(q_spec, whole, whole),
        out_shape=(_sds((s, width), BF16), _sds((s, width), F32), _sds((s, width), F32)),
        compiler_params=_params("parallel", "arbitrary"))(qk, qk, v, lsum, do)


def _as_rows(a):
    return a.reshape(-1, a.shape[-1])


def _flat_call(name, body, ins, out_dtypes, block_bytes=1 << 20):
    shape = ins[0].shape
    rows, cols = _as_rows(ins[0]).shape
    tr = _tile(rows, max(8, block_bytes // (4 * cols)), 8)
    spec = pl.BlockSpec((tr, cols), lambda i: (i, 0))
    outs = pl.pallas_call(
        body, name=name, grid=(rows // tr,), in_specs=[spec] * len(ins), out_specs=tuple([spec] * len(out_dtypes)),
        out_shape=tuple(_sds((rows, cols), dt) for dt in out_dtypes), compiler_params=_params("parallel"),
    )(*[_as_rows(a) for a in ins])
    return [o.reshape(shape) for o in outs]


def _add_pair(a, b):
    def body(a_ref, b_ref, o_ref):
        o_ref[...] = (a_ref[...].astype(F32) + b_ref[...].astype(F32)).astype(BF16)

    return _flat_call("add_pair", body, (a, b), (BF16,))[0]


def _sum_leading(a, name):
    n = a.shape[0]
    shape = a.shape[1:]
    cols = shape[-1]
    rows = a.size // (n * cols)
    tr = _tile(rows, max(8, (1 << 20) // (4 * cols)), 8)

    def body(a_ref, o_ref):
        total = a_ref[0].astype(F32)
        for j in range(1, n):
            total = total + a_ref[j].astype(F32)
        o_ref[...] = total

    out = pl.pallas_call(
        body, name=name, grid=(rows // tr,), in_specs=[pl.BlockSpec((n, tr, cols), lambda i: (0, i, 0))],
        out_specs=pl.BlockSpec((tr, cols), lambda i: (i, 0)), out_shape=_sds((rows, cols), F32),
        compiler_params=_params("parallel"))(a.reshape(n, rows, cols))
    return out.reshape(shape)


def _adamw(w, g, m, v):
    def body(w_ref, g_ref, m_ref, v_ref, d_ref, nm_ref, nv_ref):
        gv = g_ref[...]
        nm = ADAM_B1 * m_ref[...] + (1.0 - ADAM_B1) * gv
        nv = ADAM_B2 * v_ref[...] + (1.0 - ADAM_B2) * jnp.square(gv)
        nm_ref[...] = nm
        nv_ref[...] = nv
        m_hat = nm / (1.0 - ADAM_B1 ** ADAM_STEP)
        v_hat = nv / (1.0 - ADAM_B2 ** ADAM_STEP)
        d_ref[...] = -ADAM_LR * (m_hat / (jnp.sqrt(v_hat) + ADAM_EPS) + ADAM_WD * w_ref[...])

    return _flat_call("adamw", body, (w, g, m, v), (F32, F32, F32), block_bytes=1 << 19)


def _mesh_position():
    return lax.axis_index("x"), lax.axis_index("y"), lax.axis_index("c")


def _other_chips(x, y):
    return [(1 - x, y), (x, 1 - y), (1 - x, 1 - y)]


def _chip_index(chip):
    return 2 * chip[0] + chip[1]


def _remote(src, dst, send_sem, recv_sem, device):
    return pltpu.make_async_remote_copy(src_ref=src, dst_ref=dst, send_sem=send_sem, recv_sem=recv_sem,
                                        device_id=device, device_id_type=MESH)


def _pieces(view):
    shape = list(view.shape)
    counts = []
    for axis in range(len(shape) - 1):
        want = -(-(view.dtype.itemsize * functools.reduce(lambda a, b: a * b, shape)) // DMA_PIECE_BYTES)
        unit = 1 if axis < len(shape) - 2 else ROWS_PER_TILE
        n, k = shape[axis], 1
        for cand in range(1, n + 1):
            if n % cand == 0 and (n // cand) % unit == 0:
                k = cand
                if cand >= want:
                    break
        counts.append(k)
        shape[axis] = n // k
    grid = [()]
    for axis, k in enumerate(counts):
        grid = [idx + (pl.ds(i * shape[axis], shape[axis]),) for idx in grid for i in range(k)]
    return grid


def _start_remote(src, dst, send_sem, recv_sem, device):
    for idx in _pieces(src):
        _remote(src.at[idx], dst.at[idx], send_sem, recv_sem, device).start()
    return _remote(src, dst, send_sem, recv_sem, device)


def _start_local(src, dst, sem):
    for idx in _pieces(src):
        pltpu.make_async_copy(src.at[idx], dst.at[idx], sem).start()
    return pltpu.make_async_copy(src, dst, sem)


def _gather_weights(shards):
    n = len(shards)
    halves = [a.shape[0] // 2 for a in shards]
    assert all(a.shape[0] % 2 == 0 for a in shards)

    def body(*refs):
        ins, outs = refs[:n], refs[n:2 * n]
        ici_send, ici_recv, d2d_send, d2d_recv, local_sem = refs[2 * n:]
        x, y, c = _mesh_position()
        me = _chip_index((x, y))
        sibling = (x, y, 1 - c)
        chips = _other_chips(x, y)

        def half(w, chip, which):
            return outs[w].at[chip, pl.ds(which * halves[w], halves[w])]

        def from_chip(w, j):
            k = 3 * w + j
            return _remote(half(w, _chip_index(chips[j]), c), half(w, _chip_index(chips[j]), c),
                           ici_send.at[k], ici_recv.at[k], (*chips[j], c))

        def pass_on(w, j, which):
            k = 3 * w + j
            return _remote(half(w, _chip_index(chips[j]), which), half(w, _chip_index(chips[j]), which),
                           d2d_send.at[k], d2d_recv.at[k], sibling)

        local, sends, passed = [], [], []
        for w in range(n):
            local.append(_start_local(ins[w], outs[w].at[me], local_sem.at[w]))
            for j in range(3):
                k = 3 * w + j
                sends.append(_start_remote(ins[w].at[pl.ds(c * halves[w], halves[w])], half(w, me, c),
                                           ici_send.at[k], ici_recv.at[k], (*chips[j], c)))
        for w in range(n):
            for j in range(3):
                from_chip(w, j).wait_recv()
                there = half(w, _chip_index(chips[j]), c)
                passed.append(_start_remote(there, there, d2d_send.at[3 * w + j], d2d_recv.at[3 * w + j], sibling))
        for w in range(n):
            for j in range(3):
                pass_on(w, j, 1 - c).wait_recv()
        for cp in sends + passed:
            cp.wait_send()
        for cp in local:
            cp.wait()

    return pl.pallas_call(
        body, name="gather_weights", in_specs=[ANY] * n, out_specs=tuple([ANY] * n),
        out_shape=tuple(_sds((N_CHIPS,) + a.shape, a.dtype) for a in shards),
        scratch_shapes=[pltpu.SemaphoreType.DMA((3 * n,))] * 4 + [pltpu.SemaphoreType.DMA((n,))],
    )(*shards)


def _swap_halves(grads):
    n = len(grads)
    halves = [a.shape[1] // 2 for a in grads]
    assert all(a.shape[1] % 2 == 0 for a in grads)

    def body(*refs):
        ins, kept, got = refs[:n], refs[n:2 * n], refs[2 * n:3 * n]
        send_sem, recv_sem, local_sem = refs[3 * n:]
        x, y, c = _mesh_position()
        sibling = (x, y, 1 - c)

        def part(w, which):
            return ins[w].at[pl.ds(0, N_CHIPS), pl.ds(which * halves[w], halves[w])]

        local = [_start_local(part(w, c), kept[w], local_sem.at[w]) for w in range(n)]
        swaps = [_start_remote(part(w, 1 - c), got[w], send_sem.at[w], recv_sem.at[w], sibling) for w in range(n)]
        for w in range(n):
            swaps[w].wait()
            local[w].wait()

    half_shapes = [(N_CHIPS, h) + a.shape[2:] for a, h in zip(grads, halves)]
    outs = pl.pallas_call(
        body, name="swap_halves", in_specs=[ANY] * n, out_specs=tuple([ANY] * (2 * n)),
        out_shape=tuple(_sds(sh, a.dtype) for sh, a in zip(half_shapes, grads)) * 2,
        scratch_shapes=[pltpu.SemaphoreType.DMA((n,))] * 3,
    )(*grads)
    return outs[:n], outs[n:]


def _scatter_to_chips(parts):
    n = len(parts)

    def body(*refs):
        ins, outs = refs[:n], refs[n:2 * n]
        send_sem, recv_sem, local_sem = refs[2 * n:]
        x, y, c = _mesh_position()
        me = _chip_index((x, y))
        chips = _other_chips(x, y)
        local, sends = [], []
        for w in range(n):
            local.append(_start_local(ins[w].at[me], outs[w].at[me], local_sem.at[w]))
            for j in range(3):
                k = 3 * w + j
                sends.append(_start_remote(ins[w].at[_chip_index(chips[j])], outs[w].at[me], send_sem.at[k],
                                           recv_sem.at[k], (*chips[j], c)))
        for w in range(n):
            for j in range(3):
                k = 3 * w + j
                there = outs[w].at[_chip_index(chips[j])]
                _remote(there, there, send_sem.at[k], recv_sem.at[k], (*chips[j], c)).wait_recv()
        for cp in sends:
            cp.wait_send()
        for cp in local:
            cp.wait()

    return pl.pallas_call(
        body, name="scatter_to_chips", in_specs=[ANY] * n, out_specs=tuple([ANY] * n),
        out_shape=tuple(_sds(a.shape, a.dtype) for a in parts),
        scratch_shapes=[pltpu.SemaphoreType.DMA((3 * n,))] * 2 + [pltpu.SemaphoreType.DMA((n,))],
    )(*parts)


def _join_halves(mine):
    n = len(mine)

    def body(*refs):
        ins, outs = refs[:n], refs[n:2 * n]
        send_sem, recv_sem, local_sem = refs[2 * n:]
        x, y, c = _mesh_position()
        sibling = (x, y, 1 - c)
        local = [_start_local(ins[w], outs[w].at[c], local_sem.at[w]) for w in range(n)]
        sends = [_start_remote(ins[w], outs[w].at[c], send_sem.at[w], recv_sem.at[w], sibling) for w in range(n)]
        for w in range(n):
            there = outs[w].at[1 - c]
            _remote(there, there, send_sem.at[w], recv_sem.at[w], sibling).wait_recv()
            sends[w].wait_send()
            local[w].wait()

    return pl.pallas_call(
        body, name="join_halves", in_specs=[ANY] * n, out_specs=tuple([ANY] * n),
        out_shape=tuple(_sds((2,) + a.shape, a.dtype) for a in mine),
        scratch_shapes=[pltpu.SemaphoreType.DMA((n,))] * 3,
    )(*mine)


def _gather_small(vec):
    length = vec.shape[1]
    flips = [(fx, fy, fc) for fx in (0, 1) for fy in (0, 1) for fc in (0, 1)][1:]

    def body(v_ref, o_ref, send_sem, recv_sem, local_sem):
        x, y, c = _mesh_position()
        me = 4 * x + 2 * y + c
        local = pltpu.make_async_copy(v_ref, o_ref.at[me], local_sem)
        local.start()
        peers = [(x ^ fx, y ^ fy, c ^ fc) for fx, fy, fc in flips]
        sends = [_remote(v_ref, o_ref.at[me], send_sem.at[k], recv_sem.at[k], peer) for k, peer in enumerate(peers)]
        for cp in sends:
            cp.start()
        for k, (px, py, pc) in enumerate(peers):
            there = o_ref.at[4 * px + 2 * py + pc]
            _remote(there, there, send_sem.at[k], recv_sem.at[k], peers[k]).wait_recv()
        for cp in sends:
            cp.wait_send()
        local.wait()

    return pl.pallas_call(
        body, name="gather_small", in_specs=[ANY], out_specs=ANY, out_shape=_sds((N_DEVICES, 1, length), F32),
        scratch_shapes=[pltpu.SemaphoreType.DMA((N_DEVICES - 1,))] * 2 + [pltpu.SemaphoreType.DMA],
    )(vec)


MATRICES = ("w_in", "w_attn_branch", "w_pool_group", "w_pool_branch", "w_out", "w_up", "w_down", "w_ple", "w_ple_gate")
VECTORS = ("norm_mix_pre", "pool_scale", "norm_mix_post", "norm_ffn_pre", "conv_b", "norm_ffn_post", "norm_ple_post")
WEIGHTS = ("norm_mix_pre", "w_in", "w_attn_branch", "w_pool_group", "pool_scale", "w_pool_branch", "w_out",
           "norm_mix_post", "norm_ffn_pre", "w_up", "conv_w", "conv_b", "w_down", "norm_ffn_post", "w_ple",
           "w_ple_gate", "norm_ple_post")


def _local_step(x, p, target, full, vec, conv_w):
    s, d = x.shape
    w_in, w_ab, w_pb, w_up, w_ple = (full[k] for k in ("w_in", "w_attn_branch", "w_pool_branch", "w_up", "w_ple"))
    groups = len(POOL_WINDOWS)
    w_pg = full["w_pool_group"].transpose(1, 0, 2, 3)
    w_pg = w_pg.reshape(groups, -1, w_pg.shape[-1])
    w_out, w_down, w_pleg = (full[k].reshape(1, -1, full[k].shape[-1]) for k in ("w_out", "w_down", "w_ple_gate"))
    aw = w_ab.shape[1]
    blk = w_in.shape[2]
    assert blk == 2 * aw == d and w_pb.shape[1] == aw

    h1 = _rms_fwd(x, vec["norm_mix_pre"])
    qk = _mm_nn("proj_qk", h1, w_in, BF16, j0=0, nj=1)
    v = _mm_nn("proj_v", h1, w_in, BF16, j0=1, c0=0, cn=aw)
    u = _mm_nn("proj_u", h1, w_in, F32, j0=1, c0=aw, cn=aw)
    ga = _mm_nn("proj_ga", h1, w_in, F32, j0=2, nj=1)
    gp = _mm_nn("proj_gp", h1, w_in, F32, j0=3, nj=1)
    attn, lsum = _attn_fwd(qk, v)
    ya = _mm_nn("attn_branch", attn, w_ab, F32)
    pooled = _pool_fwd(u)
    pg = _group_nn("pool_group", pooled, w_pg, F32)
    ps = _scale_cols(pg, vec["pool_scale"])
    yp = _mm_nn("pool_branch", ps, w_pb, F32)
    mixed = _mix_fwd(ga, gp, ya, yp)
    mo = _mm_nn("mix_out", mixed, w_out, F32)
    x2, h2 = _post_pre(x, mo, vec["norm_mix_post"], vec["norm_ffn_pre"])
    up = _mm_nn("ffn_up", h2, w_up, F32, tn=1408)
    act = _conv_gelu_fwd(up, conv_w, vec["conv_b"])
    yf = _mm_nn("ffn_down", act, w_down, F32)
    x3 = _post(x2, yf, vec["norm_ffn_post"])
    e = _mm_nn("ple_embed", p, w_ple, F32)
    t = _mm_nn("ple_gate", x3, w_pleg, F32)
    dx4, loss_rows = _final(x3, t, e, vec["norm_ple_post"], target)

    gvec, gmat = {}, {}
    dt, de, gvec["norm_ple_post"] = _bwd_ple(dx4, t, e, vec["norm_ple_post"])
    gmat["w_ple"] = _mm_tn("d_w_ple", p, de, N_CHIPS, BF16)
    gmat["w_ple_gate"] = _mm_tn("d_w_ple_gate", x3, dt, 1, BF16)
    dx3 = _mm_nt("d_x3", dt, w_pleg, F32, add=dx4)
    dyf, gvec["norm_ffn_post"] = _bwd_post(dx3, yf, vec["norm_ffn_post"])
    gmat["w_down"] = _mm_tn("d_w_down", act, dyf, 1, BF16)
    dact = _mm_nt("d_act", dyf, w_down, BF16)
    dup_g, dup_v, dcw_g, dcw_v, dcb_g, dcb_v = _conv_gelu_bwd(up, dact, conv_w, vec["conv_b"])
    dup = jnp.concatenate([dup_g, dup_v], axis=1)
    gconv_w = jnp.concatenate([dcw_g, dcw_v], axis=1)
    gvec["conv_b"] = jnp.concatenate([dcb_g, dcb_v], axis=1)
    gmat["w_up"] = _mm_tn("d_w_up", h2, dup, N_CHIPS, BF16, tn=1408)
    dh2 = _mm_nt("d_h2", dup, w_up, F32, tk=1408)
    dx2, dmo, gvec["norm_ffn_pre"], gvec["norm_mix_post"] = _bwd_mid(
        dx3, dh2, x2, vec["norm_ffn_pre"], mo, vec["norm_mix_post"])
    gmat["w_out"] = _mm_tn("d_w_out", mixed, dmo, 1, BF16)
    dmixed = _mm_nt("d_mixed", dmo, w_out, F32)
    dya, dyp, dga, dgp = _bwd_mix(dmixed, ga, gp, ya, yp)
    gmat["w_attn_branch"] = _mm_tn("d_w_attn_branch", attn, dya, N_CHIPS, BF16)
    dattn = _mm_nt("d_attn", dya, w_ab, BF16)
    gmat["w_pool_branch"] = _mm_tn("d_w_pool_branch", ps, dyp, N_CHIPS, BF16)
    dps = _mm_nt("d_ps", dyp, w_pb, F32)
    dpg, gvec["pool_scale"] = _bwd_pool_scale(dps, pg, vec["pool_scale"])
    g_pg = _group_tn("d_w_pool_group", pooled, dpg, groups, BF16)
    gmat["w_pool_group"] = g_pg.reshape(groups, N_CHIPS, -1, g_pg.shape[-1]).transpose(1, 0, 2, 3)
    dpooled = _group_nn("d_pooled", dpg, w_pg, F32, transpose_w=True)
    du = _pool_bwd(dpooled)
    dq, dk, dv = _attn_bwd(qk, v, lsum, dattn)
    dproj = jnp.concatenate([dq, dk.astype(BF16), dv.astype(BF16), du, dga, dgp], axis=1)
    gmat["w_in"] = _mm_tn("d_w_in", h1, dproj, N_CHIPS, BF16)
    dh1 = _mm_nt("d_h1", dproj, w_in, F32)
    grad_x, gvec["norm_mix_pre"] = _bwd_first(dx2, dh1, x, vec["norm_mix_pre"])
    for k in ("w_out", "w_down", "w_ple_gate"):
        gmat[k] = gmat[k].reshape(full[k].shape)
    return loss_rows[0, 0], grad_x, gmat, gvec, gconv_w


def _reduce_matrices(gmat):
    kept, got = _swap_halves([gmat[k] for k in MATRICES])
    pair = [_add_pair(a, b) for a, b in zip(kept, got)]
    from_chips = _scatter_to_chips(pair)
    halves = [_sum_leading(a, "sum_chips") for a in from_chips]
    joined = _join_halves(halves)
    return {k: a.reshape((-1,) + a.shape[2:]) for k, a in zip(MATRICES, joined)}


def kernel(x, p, norm_mix_pre, w_in, w_attn_branch, w_pool_group, pool_scale, w_pool_branch, w_out, norm_mix_post, norm_ffn_pre, w_up, conv_w, conv_b, w_down, norm_ffn_post, w_ple, w_ple_gate, norm_ple_post, loss_target, m_norm_mix_pre, m_w_in, m_w_attn_branch, m_w_pool_group, m_pool_scale, m_w_pool_branch, m_w_out, m_norm_mix_post, m_norm_ffn_pre, m_w_up, m_conv_w, m_conv_b, m_w_down, m_norm_ffn_post, m_w_ple, m_w_ple_gate, m_norm_ple_post, v_norm_mix_pre, v_w_in, v_w_attn_branch, v_w_pool_group, v_pool_scale, v_w_pool_branch, v_w_out, v_norm_mix_post, v_norm_ffn_pre, v_w_up, v_conv_w, v_conv_b, v_w_down, v_norm_ffn_post, v_w_ple, v_w_ple_gate, v_norm_ple_post):
    given = dict(locals())
    w = {k: given[k][0] for k in WEIGHTS}
    m = {k: given["m_" + k][0] for k in WEIGHTS}
    v = {k: given["v_" + k][0] for k in WEIGHTS}
    chip = 2 * lax.axis_index("x") + lax.axis_index("y")

    full = dict(zip(MATRICES, _gather_weights([w[k].astype(BF16) for k in MATRICES])))
    vec = {k: w[k].reshape(1, -1) for k in VECTORS}
    taps = _gather_small(w["conv_w"].reshape(1, -1))[:, 0][0::2]
    f2q = w["conv_w"].shape[1]
    conv_w_full = taps.reshape(N_CHIPS, 3, f2q).transpose(1, 0, 2).reshape(3, N_CHIPS * f2q)

    loss_rows, grad_x, gmat, gvec, gconv_w = _local_step(x[0], p[0, 0], loss_target[0], full, vec, conv_w_full)
    loss = lax.psum(loss_rows, ("x", "y", "c"))

    gw = _reduce_matrices(gmat)
    sizes = [gvec[k].shape[1] for k in VECTORS]
    small = jnp.concatenate([gvec[k] for k in VECTORS] + [gconv_w.reshape(1, -1)], axis=1)
    small = _sum_leading(_gather_small(small), "sum_small")
    offset = 0
    for k, n in zip(VECTORS, sizes):
        gw[k] = small[0, offset:offset + n]
        offset += n
    gconv_w = small[0, offset:].reshape(3, N_CHIPS, f2q)
    gw["conv_w"] = lax.dynamic_index_in_dim(gconv_w, chip, axis=1, keepdims=False)

    delta, new_m, new_v = {}, {}, {}
    for k in MATRICES:
        delta[k], new_m[k], new_v[k] = _adamw(w[k], gw[k], m[k], v[k])
    tiny = VECTORS + ("conv_w",)
    pack = lambda tree: jnp.concatenate([tree[k].reshape(1, -1) for k in tiny], axis=1)
    d_s, m_s, v_s = _adamw(pack(w), pack(gw), pack(m), pack(v))
    offset = 0
    for k in tiny:
        n = w[k].size
        delta[k], new_m[k], new_v[k] = (a[0, offset:offset + n].reshape(w[k].shape) for a in (d_s, m_s, v_s))
        offset += n

    lead = lambda tree: [tree[k].reshape((1,) + w[k].shape) for k in WEIGHTS]
    return (loss, grad_x[None], *lead(gw), *lead(delta), *lead(new_m), *lead(new_v))
```

```python
import functools

import jax
import jax.numpy as jnp
from jax import lax
from jax.experimental import pallas as pl
from jax.experimental.pallas import tpu as pltpu

F32 = jnp.float32
BF16 = jnp.bfloat16

HEAD_DIM = 128
POOL_WINDOWS = (2, 4, 8, 16)
EPS = 1e-6
GELU_C = 0.7978845608028654
GELU_A = 0.044715

ADAM_LR = 0.001
ADAM_B1 = 0.9
ADAM_B2 = 0.999
ADAM_EPS = 1e-08
ADAM_WD = 0.01
ADAM_STEP = 10

N_CHIPS = 4
N_DEVICES = 8
VMEM_LIMIT_BYTES = 52 * 1024 * 1024
ATTN_BLOCK = 256
ATTN_HEADS = 2
POOL_TILE = 128
DMA_PIECE_BYTES = 512 * 1024
ROWS_PER_TILE = 16
MESH = pl.DeviceIdType.MESH
ANY = pl.BlockSpec(memory_space=pl.ANY)


def _sds(shape, dtype):
    return jax.ShapeDtypeStruct(tuple(shape), dtype)


def _tile(dim, pref, mult=128):
    t = min(pref, dim) // mult * mult
    while t >= mult:
        if dim % t == 0:
            return t
        t -= mult
    return dim


def _params(*semantics):
    return pltpu.CompilerParams(dimension_semantics=semantics or None, vmem_limit_bytes=VMEM_LIMIT_BYTES)


def _sigmoid(v):
    return 1.0 / (1.0 + jnp.exp(-v))


def _rstd(v):
    return lax.rsqrt(jnp.mean(v * v, axis=-1, keepdims=True) + EPS)


def _rms_bwd(dy, gain, v):
    r = _rstd(v)
    vh = v * r
    gy = dy * gain
    return r * (gy - vh * jnp.mean(gy * vh, axis=-1, keepdims=True)), dy * vh


def _matmul(name, a, b, grid, a_spec, b_spec, o_spec, out, dims, acc_shape, add=None, add_spec=None):
    nk = grid[2]

    def body(*refs):
        if add is None:
            a_ref, b_ref, o_ref, acc = refs
            c_ref = None
        else:
            a_ref, b_ref, c_ref, o_ref, acc = refs
        kk = pl.program_id(2)
        part = lax.dot_general(a_ref[...].astype(BF16), b_ref[...].astype(BF16), (dims, ((), ())),
                               preferred_element_type=F32)

        @pl.when(kk == 0)
        def _():
            acc[...] = part

        @pl.when(kk > 0)
        def _():
            acc[...] += part

        @pl.when(kk == nk - 1)
        def _():
            r = acc[...]
            if c_ref is not None:
                r = r + c_ref[...]
            o_ref[...] = r.astype(o_ref.dtype)

    operands = (a, b) if add is None else (a, b, add)
    in_specs = [a_spec, b_spec] if add is None else [a_spec, b_spec, add_spec]
    return pl.pallas_call(
        body, name=name, grid=grid, in_specs=in_specs, out_specs=o_spec, out_shape=out,
        scratch_shapes=[pltpu.VMEM(acc_shape, F32)],
        compiler_params=_params("parallel", "parallel", "arbitrary"),
    )(*operands)


def _mm_nn(name, a, b3, out_dtype, j0=0, nj=None, c0=0, cn=None, tm=1024, tn=1024, tk=1024):
    m, k = a.shape
    nj_all, kb, width = b3.shape
    assert kb == k
    if cn is None:
        cn = width
        nj = nj_all - j0 if nj is None else nj
    else:
        nj = 1
    n = nj * cn
    tm, tn, tk = _tile(m, tm, 8), _tile(cn, tn), _tile(k, tk)
    assert c0 % tn == 0
    nb, cb = cn // tn, c0 // tn
    return _matmul(
        name, a, b3, (m // tm, n // tn, k // tk),
        pl.BlockSpec((tm, tk), lambda i, j, kk: (i, kk)),
        pl.BlockSpec((None, tk, tn), lambda i, j, kk: (j0 + j // nb, kk, cb + j % nb)),
        pl.BlockSpec((tm, tn), lambda i, j, kk: (i, j)),
        _sds((m, n), out_dtype), ((1,), (0,)), (tm, tn))


def _mm_nt(name, a, b3, out_dtype, add=None, tm=1024, tn=1024, tk=1024):
    m, kc = a.shape
    nj, n, kj = b3.shape
    assert kc == nj * kj
    tm, tn, tk = _tile(m, tm, 8), _tile(n, tn), _tile(kj, tk)
    kb = kj // tk
    o_spec = pl.BlockSpec((tm, tn), lambda i, j, kk: (i, j))
    return _matmul(
        name, a, b3, (m // tm, n // tn, kc // tk),
        pl.BlockSpec((tm, tk), lambda i, j, kk: (i, kk)),
        pl.BlockSpec((None, tn, tk), lambda i, j, kk: (kk // kb, j, kk % kb)),
        o_spec, _sds((m, n), out_dtype), ((1,), (1,)), (tm, tn), add=add, add_spec=o_spec)


def _mm_tn(name, a, b, nj, out_dtype, tm=1024, tn=1024, ts=1024):
    s, m = a.shape
    s2, n = b.shape
    assert s == s2 and n % nj == 0
    width = n // nj
    tm, tn, ts = _tile(m, tm), _tile(width, tn), _tile(s, ts)
    nb = width // tn
    return _matmul(
        name, a, b, (m // tm, n // tn, s // ts),
        pl.BlockSpec((ts, tm), lambda i, j, kk: (kk, i)),
        pl.BlockSpec((ts, tn), lambda i, j, kk: (kk, j)),
        pl.BlockSpec((None, tm, tn), lambda i, j, kk: (j // nb, i, j % nb)),
        _sds((nj, m, width), out_dtype), ((0,), (0,)), (tm, tn))


def _group_nn(name, a, w, out_dtype, transpose_w=False, tm=1024):
    s, gc = a.shape
    g, c, _ = w.shape
    tm = _tile(s, tm, 8)
    return _matmul(
        name, a, w, (s // tm, g, 1),
        pl.BlockSpec((tm, c), lambda i, j, kk: (i, j)),
        pl.BlockSpec((None, c, c), lambda i, j, kk: (j, 0, 0)),
        pl.BlockSpec((tm, c), lambda i, j, kk: (i, j)),
        _sds((s, gc), out_dtype), ((1,), (1,)) if transpose_w else ((1,), (0,)), (tm, c))


def _group_tn(name, a, b, g, out_dtype, ts=1024):
    s, gc = a.shape
    c = gc // g
    ts = _tile(s, ts, 8)
    return _matmul(
        name, a, b, (g, 1, s // ts),
        pl.BlockSpec((ts, c), lambda i, j, kk: (kk, i)),
        pl.BlockSpec((ts, c), lambda i, j, kk: (kk, i)),
        pl.BlockSpec((None, c, c), lambda i, j, kk: (i, 0, 0)),
        _sds((g, c, c), out_dtype), ((0,), (0,)), (c, c))


def _row(tm, d):
    return pl.BlockSpec((tm, d), lambda i: (i, 0))


def _vec(d):
    return pl.BlockSpec((1, d), lambda i: (0, 0))


def _rows_call(name, body, ins, in_specs, outs, out_specs, steps, accumulates):
    return pl.pallas_call(
        body, name=name, grid=(steps,), in_specs=in_specs, out_specs=out_specs, out_shape=outs,
        compiler_params=_params("arbitrary" if accumulates else "parallel"),
    )(*ins)


def _accumulate(ref, value):
    @pl.when(pl.program_id(0) == 0)
    def _():
        ref[...] = value

    @pl.when(pl.program_id(0) > 0)
    def _():
        ref[...] += value


def _colsum(v):
    return jnp.sum(v, axis=0, keepdims=True)


def _rms_fwd(x, gain, tm=256):
    s, d = x.shape

    def body(x_ref, g_ref, h_ref):
        v = x_ref[...]
        h_ref[...] = (v * _rstd(v) * g_ref[...]).astype(BF16)

    return _rows_call("rms_fwd", body, (x, gain), [_row(tm, d), _vec(d)], _sds((s, d), BF16), _row(tm, d),
                      s // tm, False)


def _mix_fwd(ga, gp, ya, yp, tm=256):
    s, d = ga.shape

    def body(ga_ref, gp_ref, ya_ref, yp_ref, o_ref):
        o_ref[...] = (_sigmoid(ga_ref[...]) * ya_ref[...] + _sigmoid(gp_ref[...]) * yp_ref[...]).astype(BF16)

    return _rows_call("mix_fwd", body, (ga, gp, ya, yp), [_row(tm, d)] * 4, _sds((s, d), BF16), _row(tm, d),
                      s // tm, False)


def _scale_cols(v, scale, tm=256):
    s, d = v.shape

    def body(v_ref, s_ref, o_ref):
        o_ref[...] = (v_ref[...] * s_ref[...]).astype(BF16)

    return _rows_call("pool_scale_fwd", body, (v, scale), [_row(tm, d), _vec(d)], _sds((s, d), BF16), _row(tm, d),
                      s // tm, False)


def _post_pre(x, y, gain_post, gain_pre, tm=256):
    s, d = x.shape

    def body(x_ref, y_ref, gp_ref, gn_ref, x2_ref, h_ref):
        y = y_ref[...]
        x2 = x_ref[...] + y * _rstd(y) * gp_ref[...]
        x2_ref[...] = x2
        h_ref[...] = (x2 * _rstd(x2) * gn_ref[...]).astype(BF16)

    return _rows_call("post_pre", body, (x, y, gain_post, gain_pre), [_row(tm, d), _row(tm, d), _vec(d), _vec(d)],
                      (_sds((s, d), F32), _sds((s, d), BF16)), (_row(tm, d), _row(tm, d)), s // tm, False)


def _post(x, y, gain_post, tm=256):
    s, d = x.shape

    def body(x_ref, y_ref, gp_ref, o_ref):
        y = y_ref[...]
        o_ref[...] = x_ref[...] + y * _rstd(y) * gp_ref[...]

    return _rows_call("post", body, (x, y, gain_post), [_row(tm, d), _row(tm, d), _vec(d)], _sds((s, d), F32),
                      _row(tm, d), s // tm, False)


def _final(x3, t, e, gain, target, tm=256):
    s, d = x3.shape

    def body(x_ref, t_ref, e_ref, g_ref, y_ref, dx_ref, loss_ref):
        pe = _sigmoid(t_ref[...]) * e_ref[...]
        diff = x_ref[...] + pe * _rstd(pe) * g_ref[...] - y_ref[...]
        dx_ref[...] = diff * (1.0 / d)
        part = 0.5 * jnp.sum(jnp.mean(diff * diff, axis=-1, keepdims=True), axis=0, keepdims=True)
        _accumulate(loss_ref, jnp.broadcast_to(part, loss_ref.shape))

    return _rows_call("final", body, (x3, t, e, gain, target),
                      [_row(tm, d), _row(tm, d), _row(tm, d), _vec(d), _row(tm, d)],
                      (_sds((s, d), F32), _sds((8, 128), F32)),
                      (_row(tm, d), pl.BlockSpec((8, 128), lambda i: (0, 0))), s // tm, True)


def _bwd_ple(dx4, t, e, gain, tm=256):
    s, d = dx4.shape

    def body(dx_ref, t_ref, e_ref, g_ref, dt_ref, de_ref, dg_ref):
        sg = _sigmoid(t_ref[...])
        ev = e_ref[...]
        dpe, dgain = _rms_bwd(dx_ref[...], g_ref[...], sg * ev)
        de_ref[...] = (dpe * sg).astype(BF16)
        dt_ref[...] = (dpe * ev * sg * (1.0 - sg)).astype(BF16)
        _accumulate(dg_ref, _colsum(dgain))

    return _rows_call("bwd_ple", body, (dx4, t, e, gain), [_row(tm, d), _row(tm, d), _row(tm, d), _vec(d)],
                      (_sds((s, d), BF16), _sds((s, d), BF16), _sds((1, d), F32)),
                      (_row(tm, d), _row(tm, d), _vec(d)), s // tm, True)


def _bwd_post(dx, y, gain, tm=256):
    s, d = dx.shape

    def body(dx_ref, y_ref, g_ref, dy_ref, dg_ref):
        dy, dgain = _rms_bwd(dx_ref[...], g_ref[...], y_ref[...])
        dy_ref[...] = dy.astype(BF16)
        _accumulate(dg_ref, _colsum(dgain))

    return _rows_call("bwd_post", body, (dx, y, gain), [_row(tm, d), _row(tm, d), _vec(d)],
                      (_sds((s, d), BF16), _sds((1, d), F32)), (_row(tm, d), _vec(d)), s // tm, True)


def _bwd_mid(dx3, dh2, x2, gain_pre, mo, gain_post, tm=128):
    s, d = dx3.shape

    def body(dx3_ref, dh_ref, x2_ref, gn_ref, mo_ref, gp_ref, dx2_ref, dmo_ref, dgn_ref, dgp_ref):
        dv, dgn = _rms_bwd(dh_ref[...], gn_ref[...], x2_ref[...])
        dx2 = dx3_ref[...] + dv
        dx2_ref[...] = dx2
        dmo, dgp = _rms_bwd(dx2, gp_ref[...], mo_ref[...])
        dmo_ref[...] = dmo.astype(BF16)
        _accumulate(dgn_ref, _colsum(dgn))
        _accumulate(dgp_ref, _colsum(dgp))

    return _rows_call("bwd_mid", body, (dx3, dh2, x2, gain_pre, mo, gain_post),
                      [_row(tm, d), _row(tm, d), _row(tm, d), _vec(d), _row(tm, d), _vec(d)],
                      (_sds((s, d), F32), _sds((s, d), BF16), _sds((1, d), F32), _sds((1, d), F32)),
                      (_row(tm, d), _row(tm, d), _vec(d), _vec(d)), s // tm, True)


def _bwd_first(dx2, dh1, x, gain, tm=256):
    s, d = dx2.shape

    def body(dx2_ref, dh_ref, x_ref, g_ref, dx_ref, dg_ref):
        dv, dgain = _rms_bwd(dh_ref[...], g_ref[...], x_ref[...])
        dx_ref[...] = dx2_ref[...] + dv
        _accumulate(dg_ref, _colsum(dgain))

    return _rows_call("bwd_first", body, (dx2, dh1, x, gain), [_row(tm, d), _row(tm, d), _row(tm, d), _vec(d)],
                      (_sds((s, d), F32), _sds((1, d), F32)), (_row(tm, d), _vec(d)), s // tm, True)


def _bwd_mix(dmixed, ga, gp, ya, yp, tm=128):
    s, d = dmixed.shape

    def body(dm_ref, ga_ref, gp_ref, ya_ref, yp_ref, dya_ref, dyp_ref, dga_ref, dgp_ref):
        dm = dm_ref[...]
        sa, sp = _sigmoid(ga_ref[...]), _sigmoid(gp_ref[...])
        dya_ref[...] = (dm * sa).astype(BF16)
        dyp_ref[...] = (dm * sp).astype(BF16)
        dga_ref[...] = (dm * ya_ref[...] * sa * (1.0 - sa)).astype(BF16)
        dgp_ref[...] = (dm * yp_ref[...] * sp * (1.0 - sp)).astype(BF16)

    return _rows_call("bwd_mix", body, (dmixed, ga, gp, ya, yp), [_row(tm, d)] * 5,
                      (_sds((s, d), BF16),) * 4, (_row(tm, d),) * 4, s // tm, False)


def _bwd_pool_scale(dps, pg, scale, tm=256):
    s, d = dps.shape

    def body(d_ref, pg_ref, s_ref, dpg_ref, ds_ref):
        dv = d_ref[...]
        dpg_ref[...] = (dv * s_ref[...]).astype(BF16)
        _accumulate(ds_ref, _colsum(dv * pg_ref[...]))

    return _rows_call("bwd_pool_scale", body, (dps, pg, scale), [_row(tm, d), _row(tm, d), _vec(d)],
                      (_sds((s, d), BF16), _sds((1, d), F32)), (_row(tm, d), _vec(d)), s // tm, True)


def _shift_down(v, k, rows):
    return jnp.where(rows >= k, pltpu.roll(v, k, 0), 0.0)


def _shift_up(v, k, rows):
    s = v.shape[0]
    return jnp.where(rows < s - k, pltpu.roll(v, s - k, 0), 0.0)


def _window_pick(group, sums, rows):
    total = sums[-1]
    width = jnp.full((), POOL_WINDOWS[-1], jnp.int32)
    for g in range(len(POOL_WINDOWS) - 2, -1, -1):
        total = jnp.where(group == g, sums[g], total)
        width = jnp.where(group == g, POOL_WINDOWS[g], width)
    return total, jnp.minimum(rows + 1, width).astype(F32)


def _doubling(v, shift, rows):
    sums, k = [], 1
    for _ in POOL_WINDOWS:
        v = v + shift(v, k, rows)
        sums.append(v)
        k *= 2
    return sums


def _pool_fwd(u):
    s, width = u.shape
    per_group = width // len(POOL_WINDOWS) // POOL_TILE

    def body(u_ref, o_ref):
        v = u_ref[...]
        rows = lax.broadcasted_iota(jnp.int32, (s, 1), 0)
        total, count = _window_pick(pl.program_id(0), _doubling(v, _shift_down, rows), rows)
        o_ref[...] = (total / count - v).astype(BF16)

    spec = pl.BlockSpec((s, POOL_TILE), lambda g, j: (0, g * per_group + j))
    return pl.pallas_call(body, name="pool_fwd", grid=(len(POOL_WINDOWS), per_group), in_specs=[spec], out_specs=spec,
                          out_shape=_sds((s, width), BF16), compiler_params=_params("parallel", "parallel"))(u)


def _pool_bwd(dpooled):
    s, width = dpooled.shape
    per_group = width // len(POOL_WINDOWS) // POOL_TILE

    def body(d_ref, o_ref):
        dv = d_ref[...]
        rows = lax.broadcasted_iota(jnp.int32, (s, 1), 0)
        group = pl.program_id(0)
        _, count = _window_pick(group, [dv] * len(POOL_WINDOWS), rows)
        total, _ = _window_pick(group, _doubling(dv / count, _shift_up, rows), rows)
        o_ref[...] = (total - dv).astype(BF16)

    spec = pl.BlockSpec((s, POOL_TILE), lambda g, j: (0, g * per_group + j))
    return pl.pallas_call(body, name="pool_bwd", grid=(len(POOL_WINDOWS), per_group), in_specs=[spec], out_specs=spec,
                          out_shape=_sds((s, width), BF16), compiler_params=_params("parallel", "parallel"))(dpooled)


def _conv(v, w_ref, b_ref, rows):
    out = b_ref[...] + _shift_down(v, 2, rows) * w_ref[0:1, :]
    out = out + _shift_down(v, 1, rows) * w_ref[1:2, :]
    return out + v * w_ref[2:3, :]


def _gelu_parts(v):
    th = jnp.tanh(GELU_C * (v + GELU_A * v * v * v))
    return th, 0.5 * v * (1.0 + th)


def _conv_gelu_fwd(up, conv_w, conv_b, tc=128):
    s, f2 = up.shape
    f = f2 // 2
    nb = f // tc

    def body(g_ref, v_ref, wg_ref, wv_ref, bg_ref, bv_ref, o_ref):
        rows = lax.broadcasted_iota(jnp.int32, (s, 1), 0)
        _, gl = _gelu_parts(_conv(g_ref[...], wg_ref, bg_ref, rows))
        o_ref[...] = (gl * _conv(v_ref[...], wv_ref, bv_ref, rows)).astype(BF16)

    lo = lambda r: pl.BlockSpec((r, tc), lambda j: (0, j))
    hi = lambda r: pl.BlockSpec((r, tc), lambda j: (0, j + nb))
    return pl.pallas_call(
        body, name="conv_gelu_fwd", grid=(nb,), in_specs=[lo(s), hi(s), lo(3), hi(3), lo(1), hi(1)], out_specs=lo(s),
        out_shape=_sds((s, f), BF16), compiler_params=_params("parallel"))(up, up, conv_w, conv_w, conv_b, conv_b)


def _conv_gelu_bwd(up, dact, conv_w, conv_b, tc=128):
    s, f2 = up.shape
    f = f2 // 2
    nb = f // tc

    def body(g_ref, v_ref, d_ref, wg_ref, wv_ref, bg_ref, bv_ref, dg_ref, dv_ref, dwg_ref, dwv_ref, dbg_ref, dbv_ref):
        rows = lax.broadcasted_iota(jnp.int32, (s, 1), 0)
        dact = d_ref[...].astype(F32)

        def back(pre, dc, w_ref, dpre_ref, dw_ref, db_ref):
            db_ref[...] = _colsum(dc)
            dw_ref[0:1, :] = _colsum(dc * _shift_down(pre, 2, rows))
            dw_ref[1:2, :] = _colsum(dc * _shift_down(pre, 1, rows))
            dw_ref[2:3, :] = _colsum(dc * pre)
            dpre = dc * w_ref[2:3, :] + _shift_up(dc, 1, rows) * w_ref[1:2, :] + _shift_up(dc, 2, rows) * w_ref[0:1, :]
            dpre_ref[...] = dpre.astype(BF16)

        gate, val = g_ref[...], v_ref[...]
        cg = _conv(gate, wg_ref, bg_ref, rows)
        cv = _conv(val, wv_ref, bv_ref, rows)
        th, gl = _gelu_parts(cg)
        dgl = 0.5 * (1.0 + th) + 0.5 * cg * (1.0 - th * th) * GELU_C * (1.0 + 3.0 * GELU_A * cg * cg)
        back(val, dact * gl, wv_ref, dv_ref, dwv_ref, dbv_ref)
        back(gate, dact * cv * dgl, wg_ref, dg_ref, dwg_ref, dbg_ref)

    lo = lambda r: pl.BlockSpec((r, tc), lambda j: (0, j))
    hi = lambda r: pl.BlockSpec((r, tc), lambda j: (0, j + nb))
    return pl.pallas_call(
        body, name="conv_gelu_bwd", grid=(nb,),
        in_specs=[lo(s), hi(s), lo(s), lo(3), hi(3), lo(1), hi(1)],
        out_specs=(lo(s), lo(s), lo(3), lo(3), lo(1), lo(1)),
        out_shape=(_sds((s, f), BF16), _sds((s, f), BF16), _sds((3, f), F32), _sds((3, f), F32),
                   _sds((1, f), F32), _sds((1, f), F32)),
        compiler_params=_params("parallel"))(up, up, dact, conv_w, conv_w, conv_b, conv_b)


def _dot(a, b, dims):
    return lax.dot_general(a, b, (dims, ((), ())), preferred_element_type=F32)


def _running_sums(v, tri):
    hi = v.astype(BF16)
    lo = (v - hi.astype(F32)).astype(BF16)
    return _dot(hi, tri, ((1,), (0,))) + _dot(lo, tri, ((1,), (0,)))


def _attn_scores(q, kt, mask):
    z = _dot(q, kt, ((1,), (1,))) * (HEAD_DIM ** -0.5)
    e = jnp.exp(-jnp.abs(z))
    log_1m_beta = -(jnp.maximum(z, 0.0) + jnp.log(1.0 + e))
    return z, e, log_1m_beta if mask is None else jnp.where(mask, log_1m_beta, 0.0)


def _masked(v, mask):
    return v if mask is None else jnp.where(mask, v, 0.0)


def _attn_consts(t):
    rows = lax.broadcasted_iota(jnp.int32, (t, t), 0)
    cols = lax.broadcasted_iota(jnp.int32, (t, t), 1)
    return (rows >= cols).astype(BF16), (rows <= cols).astype(BF16), cols < rows


def _attn_specs(s, t, heads):
    lanes = ATTN_HEADS * HEAD_DIM
    steps = heads // ATTN_HEADS
    q = pl.BlockSpec((t, lanes), lambda h, i: (i, h))
    k = pl.BlockSpec((s, lanes), lambda h, i: (0, steps + h))
    whole = pl.BlockSpec((s, lanes), lambda h, i: (0, h))
    rows = pl.BlockSpec((ATTN_HEADS, t, 1), lambda h, i: (h, i, 0))
    return q, k, whole, rows


def _head(p):
    return pl.ds(p * HEAD_DIM, HEAD_DIM)


def _attn_fwd(qk, v, t=ATTN_BLOCK):
    s, width = v.shape
    heads = width // HEAD_DIM
    t = min(t, s)
    assert heads % ATTN_HEADS == 0

    def body(q_ref, k_ref, v_ref, o_ref, lsum_ref):
        i = pl.program_id(1)
        tri, _, causal = _attn_consts(t)

        def block(p, start, carry, mask):
            acc, log_rest = carry
            z, _, log_1m_beta = _attn_scores(q_ref[:, _head(p)], k_ref[pl.ds(start, t), _head(p)], mask)
            total = _running_sums(log_1m_beta, tri) + log_rest
            a = _masked(jnp.exp(z + total), mask)
            return acc + _dot(a.astype(BF16), v_ref[pl.ds(start, t), _head(p)], ((1,), (0,))), total[:, 0:1]

        def step(n, carry):
            start = pl.multiple_of((i - 1 - n) * t, t)
            return tuple(block(p, start, carry[p], None) for p in range(ATTN_HEADS))

        zero = (jnp.zeros((t, HEAD_DIM), F32), jnp.zeros((t, 1), F32))
        carry = tuple(block(p, pl.multiple_of(i * t, t), zero, causal) for p in range(ATTN_HEADS))
        carry = lax.fori_loop(0, i, step, carry)
        for p in range(ATTN_HEADS):
            o_ref[:, _head(p)] = carry[p][0].astype(BF16)
            lsum_ref[p] = carry[p][1]

    q_spec, k_spec, whole, rows = _attn_specs(s, t, heads)
    return pl.pallas_call(
        body, name="attn_fwd", grid=(heads // ATTN_HEADS, s // t), in_specs=[q_spec, k_spec, whole],
        out_specs=(q_spec, rows), out_shape=(_sds((s, width), BF16), _sds((heads, s, 1), F32)),
        compiler_params=_params("parallel", "parallel"))(qk, qk, v)


def _attn_bwd(qk, v, lsum, do, t=ATTN_BLOCK):
    s, width = v.shape
    heads = width // HEAD_DIM
    t = min(t, s)

    def body(q_ref, k_ref, v_ref, lsum_ref, do_ref, dq_ref, dk_ref, dv_ref):
        i = pl.program_id(1)

        @pl.when(i == 0)
        def _():
            dk_ref[...] = jnp.zeros_like(dk_ref)
            dv_ref[...] = jnp.zeros_like(dv_ref)

        _, triu, causal = _attn_consts(t)

        def block(p, start, carry, mask):
            dq, log_rest, g_before = carry
            q, do_blk = q_ref[:, _head(p)], do_ref[:, _head(p)]
            kt, vt = k_ref[pl.ds(start, t), _head(p)], v_ref[pl.ds(start, t), _head(p)]
            z, e, log_1m_beta = _attn_scores(q, kt, mask)
            upto = _running_sums(log_1m_beta, triu)
            a = _masked(jnp.exp(z + log_rest - upto + log_1m_beta), mask)
            g = a * _dot(do_blk, vt, ((1,), (1,)))
            g_upto = _running_sums(g, triu) + g_before
            sig = jnp.where(z >= 0.0, 1.0, e) / (1.0 + e)
            dz = (_masked(g - sig * g_upto, mask) * (HEAD_DIM ** -0.5)).astype(BF16)
            dk_ref[pl.ds(start, t), _head(p)] += _dot(dz, q, ((0,), (0,)))
            dv_ref[pl.ds(start, t), _head(p)] += _dot(a.astype(BF16), do_blk, ((0,), (0,)))
            return dq + _dot(dz, kt, ((1,), (0,))), log_rest - upto[:, t - 1:t], g_upto[:, t - 1:t]

        def step(n, carry):
            start = pl.multiple_of(n * t, t)
            return tuple(block(p, start, carry[p], None) for p in range(ATTN_HEADS))

        carry = tuple((jnp.zeros((t, HEAD_DIM), F32), lsum_ref[p], jnp.zeros((t, 1), F32)) for p in range(ATTN_HEADS))
        carry = lax.fori_loop(0, i, step, carry)
        for p in range(ATTN_HEADS):
            dq, _, _ = block(p, pl.multiple_of(i * t, t), carry[p], causal)
            dq_ref[:, _head(p)] = dq.astype(BF16)

    q_spec, k_spec, whole, rows = _attn_specs(s, t, heads)
    return pl.pallas_call(
        body, name="attn_bwd", grid=(heads // ATTN_HEADS, s // t), in_specs=[q_spec, k_spec, whole, rows, q_spec],
        out_specs=(q_spec, whole, whole),
        out_shape=(_sds((s, width), BF16), _sds((s, width), F32), _sds((s, width), F32)),
        compiler_params=_params("parallel", "arbitrary"))(qk, qk, v, lsum, do)


def _as_rows(a):
    return a.reshape(-1, a.shape[-1])


def _flat_call(name, body, ins, out_dtypes, block_bytes=1 << 20):
    shape = ins[0].shape
    rows, cols = _as_rows(ins[0]).shape
    tr = _tile(rows, max(8, block_bytes // (4 * cols)), 8)
    spec = pl.BlockSpec((tr, cols), lambda i: (i, 0))
    outs = pl.pallas_call(
        body, name=name, grid=(rows // tr,), in_specs=[spec] * len(ins), out_specs=tuple([spec] * len(out_dtypes)),
        out_shape=tuple(_sds((rows, cols), dt) for dt in out_dtypes), compiler_params=_params("parallel"),
    )(*[_as_rows(a) for a in ins])
    return [o.reshape(shape) for o in outs]


def _scalars(*values):
    return jnp.stack([jnp.asarray(v, jnp.int32) for v in values])


def _row_tile(rows, cols):
    return _tile(rows, max(ROWS_PER_TILE, (1 << 20) // (4 * cols)), ROWS_PER_TILE)


def _place_cast(w, chip):
    cols = w.shape[-1]
    rows = w.size // cols
    tr = _row_tile(rows, cols)

    def body(s_ref, w_ref, o_ref):
        o_ref[...] = w_ref[...].astype(BF16)

    out = pl.pallas_call(
        body, name="place_cast",
        grid_spec=pltpu.PrefetchScalarGridSpec(
            num_scalar_prefetch=1, grid=(rows // tr,),
            in_specs=[pl.BlockSpec((tr, cols), lambda i, s: (i, 0))],
            out_specs=pl.BlockSpec((None, tr, cols), lambda i, s: (s[0], i, 0))),
        out_shape=_sds((N_CHIPS, rows, cols), BF16), compiler_params=_params("parallel"),
    )(_scalars(chip), w.reshape(rows, cols))
    return out.reshape((N_CHIPS,) + w.shape)


def _add_halves(g, got, core):
    _, rows, cols = got.shape
    tr = _row_tile(rows, cols)

    def body(s_ref, a_ref, b_ref, o_ref):
        o_ref[...] = (a_ref[...].astype(F32) + b_ref[...].astype(F32)).astype(BF16)

    spec = pl.BlockSpec((None, tr, cols), lambda j, i, s: (j, i, 0))
    return pl.pallas_call(
        body, name="add_halves",
        grid_spec=pltpu.PrefetchScalarGridSpec(
            num_scalar_prefetch=1, grid=(N_CHIPS, rows // tr),
            in_specs=[pl.BlockSpec((None, None, tr, cols), lambda j, i, s: (j, s[0], i, 0)), spec], out_specs=spec),
        out_shape=_sds(got.shape, BF16), compiler_params=_params("parallel", "parallel"),
    )(_scalars(core), g, got)


def _sum_parts(own, others, chip, core):
    _, rows, cols = own.shape
    tr = _row_tile(rows, cols)

    def body(s_ref, a_ref, t_ref, o_ref):
        total = a_ref[...].astype(F32)
        for j in range(N_CHIPS - 1):
            total = total + t_ref[j].astype(F32)
        o_ref[...] = total

    return pl.pallas_call(
        body, name="sum_parts",
        grid_spec=pltpu.PrefetchScalarGridSpec(
            num_scalar_prefetch=1, grid=(rows // tr,),
            in_specs=[pl.BlockSpec((None, tr, cols), lambda i, s: (s[0], i, 0)),
                      pl.BlockSpec((N_CHIPS - 1, tr, cols), lambda i, s: (0, i, 0))],
            out_specs=pl.BlockSpec((None, tr, cols), lambda i, s: (s[1], i, 0))),
        out_shape=_sds((2, rows, cols), F32), compiler_params=_params("parallel"),
    )(_scalars(chip, core), own, others)


def _sum_leading(a, name):
    n = a.shape[0]
    shape = a.shape[1:]
    cols = shape[-1]
    rows = a.size // (n * cols)
    tr = _tile(rows, max(8, (1 << 20) // (4 * cols)), 8)

    def body(a_ref, o_ref):
        total = a_ref[0].astype(F32)
        for j in range(1, n):
            total = total + a_ref[j].astype(F32)
        o_ref[...] = total

    out = pl.pallas_call(
        body, name=name, grid=(rows // tr,), in_specs=[pl.BlockSpec((n, tr, cols), lambda i: (0, i, 0))],
        out_specs=pl.BlockSpec((tr, cols), lambda i: (i, 0)), out_shape=_sds((rows, cols), F32),
        compiler_params=_params("parallel"))(a.reshape(n, rows, cols))
    return out.reshape(shape)


def _adamw(w, g, m, v):
    def body(w_ref, g_ref, m_ref, v_ref, d_ref, nm_ref, nv_ref):
        gv = g_ref[...]
        nm = ADAM_B1 * m_ref[...] + (1.0 - ADAM_B1) * gv
        nv = ADAM_B2 * v_ref[...] + (1.0 - ADAM_B2) * jnp.square(gv)
        nm_ref[...] = nm
        nv_ref[...] = nv
        m_hat = nm / (1.0 - ADAM_B1 ** ADAM_STEP)
        v_hat = nv / (1.0 - ADAM_B2 ** ADAM_STEP)
        d_ref[...] = -ADAM_LR * (m_hat / (jnp.sqrt(v_hat) + ADAM_EPS) + ADAM_WD * w_ref[...])

    return _flat_call("adamw", body, (w, g, m, v), (F32, F32, F32), block_bytes=1 << 19)


def _mesh_position():
    return lax.axis_index("x"), lax.axis_index("y"), lax.axis_index("c")


def _other_chips(x, y):
    return [(1 - x, y), (x, 1 - y), (1 - x, 1 - y)]


def _chip_index(chip):
    return 2 * chip[0] + chip[1]


def _remote(src, dst, send_sem, recv_sem, device):
    return pltpu.make_async_remote_copy(src_ref=src, dst_ref=dst, send_sem=send_sem, recv_sem=recv_sem,
                                        device_id=device, device_id_type=MESH)


def _pieces(view):
    shape = list(view.shape)
    counts = []
    for axis in range(len(shape) - 1):
        want = -(-(view.dtype.itemsize * functools.reduce(lambda a, b: a * b, shape)) // DMA_PIECE_BYTES)
        unit = 1 if axis < len(shape) - 2 else ROWS_PER_TILE
        n, k = shape[axis], 1
        for cand in range(1, n + 1):
            if n % cand == 0 and (n // cand) % unit == 0:
                k = cand
                if cand >= want:
                    break
        counts.append(k)
        shape[axis] = n // k
    grid = [()]
    for axis, k in enumerate(counts):
        grid = [idx + (pl.ds(i * shape[axis], shape[axis]),) for idx in grid for i in range(k)]
    return grid


def _start_remote(src, dst, send_sem, recv_sem, device):
    for idx in _pieces(src):
        _remote(src.at[idx], dst.at[idx], send_sem, recv_sem, device).start()
    return _remote(src, dst, send_sem, recv_sem, device)


def _gather_weights(blocks):
    n = len(blocks)
    halves = [a.shape[1] // 2 for a in blocks]
    assert all(a.shape[1] % 2 == 0 for a in blocks)

    def body(*refs):
        outs = refs[n:2 * n]
        ici_send, ici_recv, d2d_send, d2d_recv = refs[2 * n:]
        x, y, c = _mesh_position()
        me = _chip_index((x, y))
        sibling = (x, y, 1 - c)
        chips = _other_chips(x, y)

        def half(w, chip, which):
            return outs[w].at[chip, pl.ds(which * halves[w], halves[w])]

        def from_chip(w, j):
            k = 3 * w + j
            return _remote(half(w, _chip_index(chips[j]), c), half(w, _chip_index(chips[j]), c),
                           ici_send.at[k], ici_recv.at[k], (*chips[j], c))

        def pass_on(w, j, which):
            k = 3 * w + j
            return _remote(half(w, _chip_index(chips[j]), which), half(w, _chip_index(chips[j]), which),
                           d2d_send.at[k], d2d_recv.at[k], sibling)

        sends, passed = [], []
        for w in range(n):
            for j in range(3):
                k = 3 * w + j
                sends.append(_start_remote(half(w, me, c), half(w, me, c), ici_send.at[k], ici_recv.at[k],
                                           (*chips[j], c)))
        for w in range(n):
            for j in range(3):
                from_chip(w, j).wait_recv()
                there = half(w, _chip_index(chips[j]), c)
                passed.append(_start_remote(there, there, d2d_send.at[3 * w + j], d2d_recv.at[3 * w + j], sibling))
        for w in range(n):
            for j in range(3):
                pass_on(w, j, 1 - c).wait_recv()
        for cp in sends + passed:
            cp.wait_send()

    return pl.pallas_call(
        body, name="gather_weights", in_specs=[ANY] * n, out_specs=tuple([ANY] * n),
        out_shape=tuple(_sds(a.shape, a.dtype) for a in blocks),
        input_output_aliases={w: w for w in range(n)},
        scratch_shapes=[pltpu.SemaphoreType.DMA((3 * n,))] * 4,
    )(*blocks)


def _swap_halves(grads):
    n = len(grads)

    def body(*refs):
        ins, got = refs[:n], refs[n:2 * n]
        send_sem, recv_sem = refs[2 * n:]
        x, y, c = _mesh_position()
        swaps = [_start_remote(ins[w].at[pl.ds(0, N_CHIPS), 1 - c], got[w], send_sem.at[w], recv_sem.at[w],
                               (x, y, 1 - c)) for w in range(n)]
        for cp in swaps:
            cp.wait()

    return pl.pallas_call(
        body, name="swap_halves", in_specs=[ANY] * n, out_specs=tuple([ANY] * n),
        out_shape=tuple(_sds((N_CHIPS,) + a.shape[2:], a.dtype) for a in grads),
        scratch_shapes=[pltpu.SemaphoreType.DMA((n,))] * 2,
    )(*grads)


def _scatter_to_chips(parts):
    n = len(parts)

    def body(*refs):
        ins, outs = refs[:n], refs[n:2 * n]
        send_sem, recv_sem = refs[2 * n:]
        x, y, c = _mesh_position()
        chips = _other_chips(x, y)
        sends = []
        for w in range(n):
            for j in range(3):
                k = 3 * w + j
                sends.append(_start_remote(ins[w].at[_chip_index(chips[j])], outs[w].at[j], send_sem.at[k],
                                           recv_sem.at[k], (*chips[j], c)))
        for w in range(n):
            for j in range(3):
                k = 3 * w + j
                _remote(outs[w].at[j], outs[w].at[j], send_sem.at[k], recv_sem.at[k], (*chips[j], c)).wait_recv()
        for cp in sends:
            cp.wait_send()

    return pl.pallas_call(
        body, name="scatter_to_chips", in_specs=[ANY] * n, out_specs=tuple([ANY] * n),
        out_shape=tuple(_sds((N_CHIPS - 1,) + a.shape[1:], a.dtype) for a in parts),
        scratch_shapes=[pltpu.SemaphoreType.DMA((3 * n,))] * 2,
    )(*parts)


def _join_halves(halves):
    n = len(halves)

    def body(*refs):
        outs = refs[n:2 * n]
        send_sem, recv_sem = refs[2 * n:]
        x, y, c = _mesh_position()
        sibling = (x, y, 1 - c)
        sends = [_start_remote(outs[w].at[c], outs[w].at[c], send_sem.at[w], recv_sem.at[w], sibling)
                 for w in range(n)]
        for w in range(n):
            there = outs[w].at[1 - c]
            _remote(there, there, send_sem.at[w], recv_sem.at[w], sibling).wait_recv()
            sends[w].wait_send()

    return pl.pallas_call(
        body, name="join_halves", in_specs=[ANY] * n, out_specs=tuple([ANY] * n),
        out_shape=tuple(_sds(a.shape, a.dtype) for a in halves),
        input_output_aliases={w: w for w in range(n)},
        scratch_shapes=[pltpu.SemaphoreType.DMA((n,))] * 2,
    )(*halves)


def _gather_small(vec):
    length = vec.shape[1]
    flips = [(fx, fy, fc) for fx in (0, 1) for fy in (0, 1) for fc in (0, 1)][1:]

    def body(v_ref, o_ref, send_sem, recv_sem, local_sem):
        x, y, c = _mesh_position()
        me = 4 * x + 2 * y + c
        local = pltpu.make_async_copy(v_ref, o_ref.at[me], local_sem)
        local.start()
        peers = [(x ^ fx, y ^ fy, c ^ fc) for fx, fy, fc in flips]
        sends = [_remote(v_ref, o_ref.at[me], send_sem.at[k], recv_sem.at[k], peer) for k, peer in enumerate(peers)]
        for cp in sends:
            cp.start()
        for k, (px, py, pc) in enumerate(peers):
            there = o_ref.at[4 * px + 2 * py + pc]
            _remote(there, there, send_sem.at[k], recv_sem.at[k], peers[k]).wait_recv()
        for cp in sends:
            cp.wait_send()
        local.wait()

    return pl.pallas_call(
        body, name="gather_small", in_specs=[ANY], out_specs=ANY, out_shape=_sds((N_DEVICES, 1, length), F32),
        scratch_shapes=[pltpu.SemaphoreType.DMA((N_DEVICES - 1,))] * 2 + [pltpu.SemaphoreType.DMA],
    )(vec)


MATRICES = ("w_in", "w_attn_branch", "w_pool_group", "w_pool_branch", "w_out", "w_up", "w_down", "w_ple", "w_ple_gate")
VECTORS = ("norm_mix_pre", "pool_scale", "norm_mix_post", "norm_ffn_pre", "conv_b", "norm_ffn_post", "norm_ple_post")
WEIGHTS = ("norm_mix_pre", "w_in", "w_attn_branch", "w_pool_group", "pool_scale", "w_pool_branch", "w_out",
           "norm_mix_post", "norm_ffn_pre", "w_up", "conv_w", "conv_b", "w_down", "norm_ffn_post", "w_ple",
           "w_ple_gate", "norm_ple_post")


def _local_step(x, p, target, full, vec, conv_w):
    s, d = x.shape
    w_in, w_ab, w_pb, w_up, w_ple = (full[k] for k in ("w_in", "w_attn_branch", "w_pool_branch", "w_up", "w_ple"))
    groups = len(POOL_WINDOWS)
    w_pg = full["w_pool_group"].transpose(1, 0, 2, 3)
    w_pg = w_pg.reshape(groups, -1, w_pg.shape[-1])
    w_out, w_down, w_pleg = (full[k].reshape(1, -1, full[k].shape[-1]) for k in ("w_out", "w_down", "w_ple_gate"))
    aw = w_ab.shape[1]
    blk = w_in.shape[2]
    assert blk == 2 * aw == d and w_pb.shape[1] == aw

    h1 = _rms_fwd(x, vec["norm_mix_pre"])
    qk = _mm_nn("proj_qk", h1, w_in, BF16, j0=0, nj=1)
    v = _mm_nn("proj_v", h1, w_in, BF16, j0=1, c0=0, cn=aw)
    u = _mm_nn("proj_u", h1, w_in, F32, j0=1, c0=aw, cn=aw)
    ga = _mm_nn("proj_ga", h1, w_in, F32, j0=2, nj=1)
    gp = _mm_nn("proj_gp", h1, w_in, F32, j0=3, nj=1)
    attn, lsum = _attn_fwd(qk, v)
    ya = _mm_nn("attn_branch", attn, w_ab, F32)
    pooled = _pool_fwd(u)
    pg = _group_nn("pool_group", pooled, w_pg, F32)
    ps = _scale_cols(pg, vec["pool_scale"])
    yp = _mm_nn("pool_branch", ps, w_pb, F32)
    mixed = _mix_fwd(ga, gp, ya, yp)
    mo = _mm_nn("mix_out", mixed, w_out, F32)
    x2, h2 = _post_pre(x, mo, vec["norm_mix_post"], vec["norm_ffn_pre"])
    up = _mm_nn("ffn_up", h2, w_up, F32, tn=1408)
    act = _conv_gelu_fwd(up, conv_w, vec["conv_b"])
    yf = _mm_nn("ffn_down", act, w_down, F32)
    x3 = _post(x2, yf, vec["norm_ffn_post"])
    e = _mm_nn("ple_embed", p, w_ple, F32)
    t = _mm_nn("ple_gate", x3, w_pleg, F32)
    dx4, loss_rows = _final(x3, t, e, vec["norm_ple_post"], target)

    gvec, gmat = {}, {}
    dt, de, gvec["norm_ple_post"] = _bwd_ple(dx4, t, e, vec["norm_ple_post"])
    gmat["w_ple"] = _mm_tn("d_w_ple", p, de, N_CHIPS, BF16)
    gmat["w_ple_gate"] = _mm_tn("d_w_ple_gate", x3, dt, 1, BF16)
    dx3 = _mm_nt("d_x3", dt, w_pleg, F32, add=dx4)
    dyf, gvec["norm_ffn_post"] = _bwd_post(dx3, yf, vec["norm_ffn_post"])
    gmat["w_down"] = _mm_tn("d_w_down", act, dyf, 1, BF16)
    dact = _mm_nt("d_act", dyf, w_down, BF16)
    dup_g, dup_v, dcw_g, dcw_v, dcb_g, dcb_v = _conv_gelu_bwd(up, dact, conv_w, vec["conv_b"])
    dup = jnp.concatenate([dup_g, dup_v], axis=1)
    gconv_w = jnp.concatenate([dcw_g, dcw_v], axis=1)
    gvec["conv_b"] = jnp.concatenate([dcb_g, dcb_v], axis=1)
    gmat["w_up"] = _mm_tn("d_w_up", h2, dup, N_CHIPS, BF16, tn=1408)
    dh2 = _mm_nt("d_h2", dup, w_up, F32, tk=1408)
    dx2, dmo, gvec["norm_ffn_pre"], gvec["norm_mix_post"] = _bwd_mid(
        dx3, dh2, x2, vec["norm_ffn_pre"], mo, vec["norm_mix_post"])
    gmat["w_out"] = _mm_tn("d_w_out", mixed, dmo, 1, BF16)
    dmixed = _mm_nt("d_mixed", dmo, w_out, F32)
    dya, dyp, dga, dgp = _bwd_mix(dmixed, ga, gp, ya, yp)
    gmat["w_attn_branch"] = _mm_tn("d_w_attn_branch", attn, dya, N_CHIPS, BF16)
    dattn = _mm_nt("d_attn", dya, w_ab, BF16)
    gmat["w_pool_branch"] = _mm_tn("d_w_pool_branch", ps, dyp, N_CHIPS, BF16)
    dps = _mm_nt("d_ps", dyp, w_pb, F32)
    dpg, gvec["pool_scale"] = _bwd_pool_scale(dps, pg, vec["pool_scale"])
    g_pg = _group_tn("d_w_pool_group", pooled, dpg, groups, BF16)
    gmat["w_pool_group"] = g_pg.reshape(groups, N_CHIPS, -1, g_pg.shape[-1]).transpose(1, 0, 2, 3)
    dpooled = _group_nn("d_pooled", dpg, w_pg, F32, transpose_w=True)
    du = _pool_bwd(dpooled)
    dq, dk, dv = _attn_bwd(qk, v, lsum, dattn)
    dproj = jnp.concatenate([dq, dk.astype(BF16), dv.astype(BF16), du, dga, dgp], axis=1)
    gmat["w_in"] = _mm_tn("d_w_in", h1, dproj, N_CHIPS, BF16)
    dh1 = _mm_nt("d_h1", dproj, w_in, F32)
    grad_x, gvec["norm_mix_pre"] = _bwd_first(dx2, dh1, x, vec["norm_mix_pre"])
    for k in ("w_out", "w_down", "w_ple_gate"):
        gmat[k] = gmat[k].reshape(full[k].shape)
    return loss_rows[0, 0], grad_x, gmat, gvec, gconv_w


def _reduce_matrices(gmat, shapes, chip, core):
    flat = []
    for k in MATRICES:
        g = gmat[k]
        flat.append(g.reshape(N_CHIPS, 2, g.size // (2 * N_CHIPS * g.shape[-1]), g.shape[-1]))
    got = _swap_halves(flat)
    pair = [_add_halves(a, b, core) for a, b in zip(flat, got)]
    others = _scatter_to_chips(pair)
    halves = [_sum_parts(a, b, chip, core) for a, b in zip(pair, others)]
    joined = _join_halves(halves)
    return {k: a.reshape(shapes[k]) for k, a in zip(MATRICES, joined)}


def kernel(x, p, norm_mix_pre, w_in, w_attn_branch, w_pool_group, pool_scale, w_pool_branch, w_out, norm_mix_post, norm_ffn_pre, w_up, conv_w, conv_b, w_down, norm_ffn_post, w_ple, w_ple_gate, norm_ple_post, loss_target, m_norm_mix_pre, m_w_in, m_w_attn_branch, m_w_pool_group, m_pool_scale, m_w_pool_branch, m_w_out, m_norm_mix_post, m_norm_ffn_pre, m_w_up, m_conv_w, m_conv_b, m_w_down, m_norm_ffn_post, m_w_ple, m_w_ple_gate, m_norm_ple_post, v_norm_mix_pre, v_w_in, v_w_attn_branch, v_w_pool_group, v_pool_scale, v_w_pool_branch, v_w_out, v_norm_mix_post, v_norm_ffn_pre, v_w_up, v_conv_w, v_conv_b, v_w_down, v_norm_ffn_post, v_w_ple, v_w_ple_gate, v_norm_ple_post):
    given = dict(locals())
    w = {k: given[k][0] for k in WEIGHTS}
    m = {k: given["m_" + k][0] for k in WEIGHTS}
    v = {k: given["v_" + k][0] for k in WEIGHTS}
    chip = 2 * lax.axis_index("x") + lax.axis_index("y")

    full = dict(zip(MATRICES, _gather_weights([_place_cast(w[k], chip) for k in MATRICES])))
    vec = {k: w[k].reshape(1, -1) for k in VECTORS}
    taps = _gather_small(w["conv_w"].reshape(1, -1))[:, 0][0::2]
    f2q = w["conv_w"].shape[1]
    conv_w_full = taps.reshape(N_CHIPS, 3, f2q).transpose(1, 0, 2).reshape(3, N_CHIPS * f2q)

    loss_rows, grad_x, gmat, gvec, gconv_w = _local_step(x[0], p[0, 0], loss_target[0], full, vec, conv_w_full)
    loss = lax.psum(loss_rows, ("x", "y", "c"))

    gw = _reduce_matrices(gmat, {k: w[k].shape for k in MATRICES}, chip, lax.axis_index("c"))
    sizes = [gvec[k].shape[1] for k in VECTORS]
    small = jnp.concatenate([gvec[k] for k in VECTORS] + [gconv_w.reshape(1, -1)], axis=1)
    small = _sum_leading(_gather_small(small), "sum_small")
    offset = 0
    for k, n in zip(VECTORS, sizes):
        gw[k] = small[0, offset:offset + n]
        offset += n
    gconv_w = small[0, offset:].reshape(3, N_CHIPS, f2q)
    gw["conv_w"] = lax.dynamic_index_in_dim(gconv_w, chip, axis=1, keepdims=False)

    delta, new_m, new_v = {}, {}, {}
    for k in MATRICES:
        delta[k], new_m[k], new_v[k] = _adamw(w[k], gw[k], m[k], v[k])
    tiny = VECTORS + ("conv_w",)
    pack = lambda tree: jnp.concatenate([tree[k].reshape(1, -1) for k in tiny], axis=1)
    d_s, m_s, v_s = _adamw(pack(w), pack(gw), pack(m), pack(v))
    offset = 0
    for k in tiny:
        n = w[k].size
        delta[k], new_m[k], new_v[k] = (a[0, offset:offset + n].reshape(w[k].shape) for a in (d_s, m_s, v_s))
        offset += n

    lead = lambda tree: [tree[k].reshape((1,) + w[k].shape) for k in WEIGHTS]
    return (loss, grad_x[None], *lead(gw), *lead(delta), *lead(new_m), *lead(new_v))
```

```python
import functools

import jax
import jax.numpy as jnp
from jax import lax
from jax.experimental import pallas as pl
from jax.experimental.pallas import tpu as pltpu

F32 = jnp.float32
BF16 = jnp.bfloat16

HEAD_DIM = 128
POOL_WINDOWS = (2, 4, 8, 16)
EPS = 1e-6
GELU_C = 0.7978845608028654
GELU_A = 0.044715

ADAM_LR = 0.001
ADAM_B1 = 0.9
ADAM_B2 = 0.999
ADAM_EPS = 1e-08
ADAM_WD = 0.01
ADAM_STEP = 10

N_CHIPS = 4
N_DEVICES = 8
VMEM_LIMIT_BYTES = 52 * 1024 * 1024
ATTN_BLOCK = 256
ATTN_HEADS = 2
POOL_TILE = 128
RIDE_MID_STEPS = 4
DMA_PIECE_BYTES = 512 * 1024
ROWS_PER_TILE = 16
MESH = pl.DeviceIdType.MESH
ANY = pl.BlockSpec(memory_space=pl.ANY)


def _sds(shape, dtype):
    return jax.ShapeDtypeStruct(tuple(shape), dtype)


def _tile(dim, pref, mult=128):
    t = min(pref, dim) // mult * mult
    while t >= mult:
        if dim % t == 0:
            return t
        t -= mult
    return dim


def _params(*semantics):
    return pltpu.CompilerParams(dimension_semantics=semantics or None, vmem_limit_bytes=VMEM_LIMIT_BYTES)


def _sigmoid(v):
    return 1.0 / (1.0 + jnp.exp(-v))


def _rstd(v):
    return lax.rsqrt(jnp.mean(v * v, axis=-1, keepdims=True) + EPS)


def _rms_bwd(dy, gain, v):
    r = _rstd(v)
    vh = v * r
    gy = dy * gain
    return r * (gy - vh * jnp.mean(gy * vh, axis=-1, keepdims=True)), dy * vh


def _matmul(name, a, b, grid, a_spec, b_spec, o_spec, out, dims, acc_shape, add=None, add_spec=None):
    nk = grid[2]

    def body(*refs):
        if add is None:
            a_ref, b_ref, o_ref, acc = refs
            c_ref = None
        else:
            a_ref, b_ref, c_ref, o_ref, acc = refs
        kk = pl.program_id(2)
        part = lax.dot_general(a_ref[...].astype(BF16), b_ref[...].astype(BF16), (dims, ((), ())),
                               preferred_element_type=F32)

        @pl.when(kk == 0)
        def _():
            acc[...] = part

        @pl.when(kk > 0)
        def _():
            acc[...] += part

        @pl.when(kk == nk - 1)
        def _():
            r = acc[...]
            if c_ref is not None:
                r = r + c_ref[...]
            o_ref[...] = r.astype(o_ref.dtype)

    operands = (a, b) if add is None else (a, b, add)
    in_specs = [a_spec, b_spec] if add is None else [a_spec, b_spec, add_spec]
    return pl.pallas_call(
        body, name=name, grid=grid, in_specs=in_specs, out_specs=o_spec, out_shape=out,
        scratch_shapes=[pltpu.VMEM(acc_shape, F32)],
        compiler_params=_params("parallel", "parallel", "arbitrary"),
    )(*operands)


def _mm_nn(name, a, b3, out_dtype, j0=0, nj=None, c0=0, cn=None, tm=1024, tn=1024, tk=1024):
    m, k = a.shape
    nj_all, kb, width = b3.shape
    assert kb == k
    if cn is None:
        cn = width
        nj = nj_all - j0 if nj is None else nj
    else:
        nj = 1
    n = nj * cn
    tm, tn, tk = _tile(m, tm, 8), _tile(cn, tn), _tile(k, tk)
    assert c0 % tn == 0
    nb, cb = cn // tn, c0 // tn
    return _matmul(
        name, a, b3, (m // tm, n // tn, k // tk),
        pl.BlockSpec((tm, tk), lambda i, j, kk: (i, kk)),
        pl.BlockSpec((None, tk, tn), lambda i, j, kk: (j0 + j // nb, kk, cb + j % nb)),
        pl.BlockSpec((tm, tn), lambda i, j, kk: (i, j)),
        _sds((m, n), out_dtype), ((1,), (0,)), (tm, tn))


def _mm_nt(name, a, b3, out_dtype, add=None, tm=1024, tn=1024, tk=1024):
    m, kc = a.shape
    nj, n, kj = b3.shape
    assert kc == nj * kj
    tm, tn, tk = _tile(m, tm, 8), _tile(n, tn), _tile(kj, tk)
    kb = kj // tk
    o_spec = pl.BlockSpec((tm, tn), lambda i, j, kk: (i, j))
    return _matmul(
        name, a, b3, (m // tm, n // tn, kc // tk),
        pl.BlockSpec((tm, tk), lambda i, j, kk: (i, kk)),
        pl.BlockSpec((None, tn, tk), lambda i, j, kk: (kk // kb, j, kk % kb)),
        o_spec, _sds((m, n), out_dtype), ((1,), (1,)), (tm, tn), add=add, add_spec=o_spec)


def _mm_tn(name, a, b, nj, out_dtype, tm=1024, tn=1024, ts=1024):
    s, m = a.shape
    s2, n = b.shape
    assert s == s2 and n % nj == 0
    width = n // nj
    tm, tn, ts = _tile(m, tm), _tile(width, tn), _tile(s, ts)
    nb = width // tn
    return _matmul(
        name, a, b, (m // tm, n // tn, s // ts),
        pl.BlockSpec((ts, tm), lambda i, j, kk: (kk, i)),
        pl.BlockSpec((ts, tn), lambda i, j, kk: (kk, j)),
        pl.BlockSpec((None, tm, tn), lambda i, j, kk: (j // nb, i, j % nb)),
        _sds((nj, m, width), out_dtype), ((0,), (0,)), (tm, tn))


def _group_nn(name, a, w, out_dtype, transpose_w=False, tm=1024):
    s, gc = a.shape
    g, c, _ = w.shape
    tm = _tile(s, tm, 8)
    return _matmul(
        name, a, w, (s // tm, g, 1),
        pl.BlockSpec((tm, c), lambda i, j, kk: (i, j)),
        pl.BlockSpec((None, c, c), lambda i, j, kk: (j, 0, 0)),
        pl.BlockSpec((tm, c), lambda i, j, kk: (i, j)),
        _sds((s, gc), out_dtype), ((1,), (1,)) if transpose_w else ((1,), (0,)), (tm, c))


def _group_tn(name, a, b, g, out_dtype, ts=1024):
    s, gc = a.shape
    c = gc // g
    ts = _tile(s, ts, 8)
    return _matmul(
        name, a, b, (g, 1, s // ts),
        pl.BlockSpec((ts, c), lambda i, j, kk: (kk, i)),
        pl.BlockSpec((ts, c), lambda i, j, kk: (kk, i)),
        pl.BlockSpec((None, c, c), lambda i, j, kk: (i, 0, 0)),
        _sds((g, c, c), out_dtype), ((0,), (0,)), (c, c))


def _row(tm, d):
    return pl.BlockSpec((tm, d), lambda i: (i, 0))


def _vec(d):
    return pl.BlockSpec((1, d), lambda i: (0, 0))


def _rows_call(name, body, ins, in_specs, outs, out_specs, steps, accumulates):
    return pl.pallas_call(
        body, name=name, grid=(steps,), in_specs=in_specs, out_specs=out_specs, out_shape=outs,
        compiler_params=_params("arbitrary" if accumulates else "parallel"),
    )(*ins)


def _accumulate(ref, value):
    @pl.when(pl.program_id(0) == 0)
    def _():
        ref[...] = value

    @pl.when(pl.program_id(0) > 0)
    def _():
        ref[...] += value


def _colsum(v):
    return jnp.sum(v, axis=0, keepdims=True)


def _rms_fwd(x, gain, tm=256):
    s, d = x.shape

    def body(x_ref, g_ref, h_ref):
        v = x_ref[...]
        h_ref[...] = (v * _rstd(v) * g_ref[...]).astype(BF16)

    return _rows_call("rms_fwd", body, (x, gain), [_row(tm, d), _vec(d)], _sds((s, d), BF16), _row(tm, d),
                      s // tm, False)


def _mix_fwd(ga, gp, ya, yp, tm=256):
    s, d = ga.shape

    def body(ga_ref, gp_ref, ya_ref, yp_ref, o_ref):
        o_ref[...] = (_sigmoid(ga_ref[...]) * ya_ref[...] + _sigmoid(gp_ref[...]) * yp_ref[...]).astype(BF16)

    return _rows_call("mix_fwd", body, (ga, gp, ya, yp), [_row(tm, d)] * 4, _sds((s, d), BF16), _row(tm, d),
                      s // tm, False)


def _scale_cols(v, scale, tm=256):
    s, d = v.shape

    def body(v_ref, s_ref, o_ref):
        o_ref[...] = (v_ref[...] * s_ref[...]).astype(BF16)

    return _rows_call("pool_scale_fwd", body, (v, scale), [_row(tm, d), _vec(d)], _sds((s, d), BF16), _row(tm, d),
                      s // tm, False)


def _post_pre(x, y, gain_post, gain_pre, tm=256):
    s, d = x.shape

    def body(x_ref, y_ref, gp_ref, gn_ref, x2_ref, h_ref):
        y = y_ref[...]
        x2 = x_ref[...] + y * _rstd(y) * gp_ref[...]
        x2_ref[...] = x2
        h_ref[...] = (x2 * _rstd(x2) * gn_ref[...]).astype(BF16)

    return _rows_call("post_pre", body, (x, y, gain_post, gain_pre), [_row(tm, d), _row(tm, d), _vec(d), _vec(d)],
                      (_sds((s, d), F32), _sds((s, d), BF16)), (_row(tm, d), _row(tm, d)), s // tm, False)


def _post(x, y, gain_post, tm=256):
    s, d = x.shape

    def body(x_ref, y_ref, gp_ref, o_ref):
        y = y_ref[...]
        o_ref[...] = x_ref[...] + y * _rstd(y) * gp_ref[...]

    return _rows_call("post", body, (x, y, gain_post), [_row(tm, d), _row(tm, d), _vec(d)], _sds((s, d), F32),
                      _row(tm, d), s // tm, False)


def _final(x3, t, e, gain, target, tm=256):
    s, d = x3.shape

    def body(x_ref, t_ref, e_ref, g_ref, y_ref, dx_ref, loss_ref):
        pe = _sigmoid(t_ref[...]) * e_ref[...]
        diff = x_ref[...] + pe * _rstd(pe) * g_ref[...] - y_ref[...]
        dx_ref[...] = diff * (1.0 / d)
        part = 0.5 * jnp.sum(jnp.mean(diff * diff, axis=-1, keepdims=True), axis=0, keepdims=True)
        _accumulate(loss_ref, jnp.broadcast_to(part, loss_ref.shape))

    return _rows_call("final", body, (x3, t, e, gain, target),
                      [_row(tm, d), _row(tm, d), _row(tm, d), _vec(d), _row(tm, d)],
                      (_sds((s, d), F32), _sds((8, 128), F32)),
                      (_row(tm, d), pl.BlockSpec((8, 128), lambda i: (0, 0))), s // tm, True)


def _bwd_ple(dx4, t, e, gain, tm=256):
    s, d = dx4.shape

    def body(dx_ref, t_ref, e_ref, g_ref, dt_ref, de_ref, dg_ref):
        sg = _sigmoid(t_ref[...])
        ev = e_ref[...]
        dpe, dgain = _rms_bwd(dx_ref[...], g_ref[...], sg * ev)
        de_ref[...] = (dpe * sg).astype(BF16)
        dt_ref[...] = (dpe * ev * sg * (1.0 - sg)).astype(BF16)
        _accumulate(dg_ref, _colsum(dgain))

    return _rows_call("bwd_ple", body, (dx4, t, e, gain), [_row(tm, d), _row(tm, d), _row(tm, d), _vec(d)],
                      (_sds((s, d), BF16), _sds((s, d), BF16), _sds((1, d), F32)),
                      (_row(tm, d), _row(tm, d), _vec(d)), s // tm, True)


def _bwd_post(dx, y, gain, tm=256):
    s, d = dx.shape

    def body(dx_ref, y_ref, g_ref, dy_ref, dg_ref):
        dy, dgain = _rms_bwd(dx_ref[...], g_ref[...], y_ref[...])
        dy_ref[...] = dy.astype(BF16)
        _accumulate(dg_ref, _colsum(dgain))

    return _rows_call("bwd_post", body, (dx, y, gain), [_row(tm, d), _row(tm, d), _vec(d)],
                      (_sds((s, d), BF16), _sds((1, d), F32)), (_row(tm, d), _vec(d)), s // tm, True)


def _bwd_mid(dx3, dh2, x2, gain_pre, mo, gain_post, tm=128):
    s, d = dx3.shape

    def body(dx3_ref, dh_ref, x2_ref, gn_ref, mo_ref, gp_ref, dx2_ref, dmo_ref, dgn_ref, dgp_ref):
        dv, dgn = _rms_bwd(dh_ref[...], gn_ref[...], x2_ref[...])
        dx2 = dx3_ref[...] + dv
        dx2_ref[...] = dx2
        dmo, dgp = _rms_bwd(dx2, gp_ref[...], mo_ref[...])
        dmo_ref[...] = dmo.astype(BF16)
        _accumulate(dgn_ref, _colsum(dgn))
        _accumulate(dgp_ref, _colsum(dgp))

    return _rows_call("bwd_mid", body, (dx3, dh2, x2, gain_pre, mo, gain_post),
                      [_row(tm, d), _row(tm, d), _row(tm, d), _vec(d), _row(tm, d), _vec(d)],
                      (_sds((s, d), F32), _sds((s, d), BF16), _sds((1, d), F32), _sds((1, d), F32)),
                      (_row(tm, d), _row(tm, d), _vec(d), _vec(d)), s // tm, True)


def _bwd_first(dx2, dh1, x, gain, tm=256):
    s, d = dx2.shape

    def body(dx2_ref, dh_ref, x_ref, g_ref, dx_ref, dg_ref):
        dv, dgain = _rms_bwd(dh_ref[...], g_ref[...], x_ref[...])
        dx_ref[...] = dx2_ref[...] + dv
        _accumulate(dg_ref, _colsum(dgain))

    return _rows_call("bwd_first", body, (dx2, dh1, x, gain), [_row(tm, d), _row(tm, d), _row(tm, d), _vec(d)],
                      (_sds((s, d), F32), _sds((1, d), F32)), (_row(tm, d), _vec(d)), s // tm, True)


def _bwd_mix(dmixed, ga, gp, ya, yp, tm=128):
    s, d = dmixed.shape

    def body(dm_ref, ga_ref, gp_ref, ya_ref, yp_ref, dya_ref, dyp_ref, dga_ref, dgp_ref):
        dm = dm_ref[...]
        sa, sp = _sigmoid(ga_ref[...]), _sigmoid(gp_ref[...])
        dya_ref[...] = (dm * sa).astype(BF16)
        dyp_ref[...] = (dm * sp).astype(BF16)
        dga_ref[...] = (dm * ya_ref[...] * sa * (1.0 - sa)).astype(BF16)
        dgp_ref[...] = (dm * yp_ref[...] * sp * (1.0 - sp)).astype(BF16)

    return _rows_call("bwd_mix", body, (dmixed, ga, gp, ya, yp), [_row(tm, d)] * 5,
                      (_sds((s, d), BF16),) * 4, (_row(tm, d),) * 4, s // tm, False)


def _bwd_pool_scale(dps, pg, scale, tm=256):
    s, d = dps.shape

    def body(d_ref, pg_ref, s_ref, dpg_ref, ds_ref):
        dv = d_ref[...]
        dpg_ref[...] = (dv * s_ref[...]).astype(BF16)
        _accumulate(ds_ref, _colsum(dv * pg_ref[...]))

    return _rows_call("bwd_pool_scale", body, (dps, pg, scale), [_row(tm, d), _row(tm, d), _vec(d)],
                      (_sds((s, d), BF16), _sds((1, d), F32)), (_row(tm, d), _vec(d)), s // tm, True)


def _shift_down(v, k, rows):
    return jnp.where(rows >= k, pltpu.roll(v, k, 0), 0.0)


def _shift_up(v, k, rows):
    s = v.shape[0]
    return jnp.where(rows < s - k, pltpu.roll(v, s - k, 0), 0.0)


def _window_pick(group, sums, rows):
    total = sums[-1]
    width = jnp.full((), POOL_WINDOWS[-1], jnp.int32)
    for g in range(len(POOL_WINDOWS) - 2, -1, -1):
        total = jnp.where(group == g, sums[g], total)
        width = jnp.where(group == g, POOL_WINDOWS[g], width)
    return total, jnp.minimum(rows + 1, width).astype(F32)


def _doubling(v, shift, rows):
    sums, k = [], 1
    for _ in POOL_WINDOWS:
        v = v + shift(v, k, rows)
        sums.append(v)
        k *= 2
    return sums


def _pool_fwd(u):
    s, width = u.shape
    per_group = width // len(POOL_WINDOWS) // POOL_TILE

    def body(u_ref, o_ref):
        v = u_ref[...]
        rows = lax.broadcasted_iota(jnp.int32, (s, 1), 0)
        total, count = _window_pick(pl.program_id(0), _doubling(v, _shift_down, rows), rows)
        o_ref[...] = (total / count - v).astype(BF16)

    spec = pl.BlockSpec((s, POOL_TILE), lambda g, j: (0, g * per_group + j))
    return pl.pallas_call(body, name="pool_fwd", grid=(len(POOL_WINDOWS), per_group), in_specs=[spec], out_specs=spec,
                          out_shape=_sds((s, width), BF16), compiler_params=_params("parallel", "parallel"))(u)


def _pool_bwd(dpooled):
    s, width = dpooled.shape
    per_group = width // len(POOL_WINDOWS) // POOL_TILE

    def body(d_ref, o_ref):
        dv = d_ref[...]
        rows = lax.broadcasted_iota(jnp.int32, (s, 1), 0)
        group = pl.program_id(0)
        _, count = _window_pick(group, [dv] * len(POOL_WINDOWS), rows)
        total, _ = _window_pick(group, _doubling(dv / count, _shift_up, rows), rows)
        o_ref[...] = (total - dv).astype(BF16)

    spec = pl.BlockSpec((s, POOL_TILE), lambda g, j: (0, g * per_group + j))
    return pl.pallas_call(body, name="pool_bwd", grid=(len(POOL_WINDOWS), per_group), in_specs=[spec], out_specs=spec,
                          out_shape=_sds((s, width), BF16), compiler_params=_params("parallel", "parallel"))(dpooled)


def _conv(v, w_ref, b_ref, rows):
    out = b_ref[...] + _shift_down(v, 2, rows) * w_ref[0:1, :]
    out = out + _shift_down(v, 1, rows) * w_ref[1:2, :]
    return out + v * w_ref[2:3, :]


def _gelu_parts(v):
    th = jnp.tanh(GELU_C * (v + GELU_A * v * v * v))
    return th, 0.5 * v * (1.0 + th)


def _conv_gelu_fwd(up, conv_w, conv_b, tc=128):
    s, f2 = up.shape
    f = f2 // 2
    nb = f // tc

    def body(g_ref, v_ref, wg_ref, wv_ref, bg_ref, bv_ref, o_ref):
        rows = lax.broadcasted_iota(jnp.int32, (s, 1), 0)
        _, gl = _gelu_parts(_conv(g_ref[...], wg_ref, bg_ref, rows))
        o_ref[...] = (gl * _conv(v_ref[...], wv_ref, bv_ref, rows)).astype(BF16)

    lo = lambda r: pl.BlockSpec((r, tc), lambda j: (0, j))
    hi = lambda r: pl.BlockSpec((r, tc), lambda j: (0, j + nb))
    return pl.pallas_call(
        body, name="conv_gelu_fwd", grid=(nb,), in_specs=[lo(s), hi(s), lo(3), hi(3), lo(1), hi(1)], out_specs=lo(s),
        out_shape=_sds((s, f), BF16), compiler_params=_params("parallel"))(up, up, conv_w, conv_w, conv_b, conv_b)


def _conv_gelu_bwd(up, dact, conv_w, conv_b, tc=128):
    s, f2 = up.shape
    f = f2 // 2
    nb = f // tc

    def body(g_ref, v_ref, d_ref, wg_ref, wv_ref, bg_ref, bv_ref, dg_ref, dv_ref, dwg_ref, dwv_ref, dbg_ref, dbv_ref):
        rows = lax.broadcasted_iota(jnp.int32, (s, 1), 0)
        dact = d_ref[...].astype(F32)

        def back(pre, dc, w_ref, dpre_ref, dw_ref, db_ref):
            db_ref[...] = _colsum(dc)
            dw_ref[0:1, :] = _colsum(dc * _shift_down(pre, 2, rows))
            dw_ref[1:2, :] = _colsum(dc * _shift_down(pre, 1, rows))
            dw_ref[2:3, :] = _colsum(dc * pre)
            dpre = dc * w_ref[2:3, :] + _shift_up(dc, 1, rows) * w_ref[1:2, :] + _shift_up(dc, 2, rows) * w_ref[0:1, :]
            dpre_ref[...] = dpre.astype(BF16)

        gate, val = g_ref[...], v_ref[...]
        cg = _conv(gate, wg_ref, bg_ref, rows)
        cv = _conv(val, wv_ref, bv_ref, rows)
        th, gl = _gelu_parts(cg)
        dgl = 0.5 * (1.0 + th) + 0.5 * cg * (1.0 - th * th) * GELU_C * (1.0 + 3.0 * GELU_A * cg * cg)
        back(val, dact * gl, wv_ref, dv_ref, dwv_ref, dbv_ref)
        back(gate, dact * cv * dgl, wg_ref, dg_ref, dwg_ref, dbg_ref)

    lo = lambda r: pl.BlockSpec((r, tc), lambda j: (0, j))
    hi = lambda r: pl.BlockSpec((r, tc), lambda j: (0, j + nb))
    return pl.pallas_call(
        body, name="conv_gelu_bwd", grid=(nb,),
        in_specs=[lo(s), hi(s), lo(s), lo(3), hi(3), lo(1), hi(1)],
        out_specs=(lo(s), lo(s), lo(3), lo(3), lo(1), lo(1)),
        out_shape=(_sds((s, f), BF16), _sds((s, f), BF16), _sds((3, f), F32), _sds((3, f), F32),
                   _sds((1, f), F32), _sds((1, f), F32)),
        compiler_params=_params("parallel"))(up, up, dact, conv_w, conv_w, conv_b, conv_b)


def _dot(a, b, dims):
    return lax.dot_general(a, b, (dims, ((), ())), preferred_element_type=F32)


def _running_sums(v, tri):
    hi = v.astype(BF16)
    lo = (v - hi.astype(F32)).astype(BF16)
    return _dot(hi, tri, ((1,), (0,))) + _dot(lo, tri, ((1,), (0,)))


def _attn_scores(q, kt, mask):
    z = _dot(q, kt, ((1,), (1,))) * (HEAD_DIM ** -0.5)
    e = jnp.exp(-jnp.abs(z))
    log_1m_beta = -(jnp.maximum(z, 0.0) + jnp.log(1.0 + e))
    return z, e, log_1m_beta if mask is None else jnp.where(mask, log_1m_beta, 0.0)


def _masked(v, mask):
    return v if mask is None else jnp.where(mask, v, 0.0)


def _attn_consts(t):
    rows = lax.broadcasted_iota(jnp.int32, (t, t), 0)
    cols = lax.broadcasted_iota(jnp.int32, (t, t), 1)
    return (rows >= cols).astype(BF16), (rows <= cols).astype(BF16), cols < rows


def _attn_specs(s, t, heads):
    lanes = ATTN_HEADS * HEAD_DIM
    steps = heads // ATTN_HEADS
    q = pl.BlockSpec((t, lanes), lambda h, i: (i, h))
    k = pl.BlockSpec((s, lanes), lambda h, i: (0, steps + h))
    whole = pl.BlockSpec((s, lanes), lambda h, i: (0, h))
    rows = pl.BlockSpec((ATTN_HEADS, t, 1), lambda h, i: (h, i, 0))
    return q, k, whole, rows


def _head(p):
    return pl.ds(p * HEAD_DIM, HEAD_DIM)


def _ride_along(ride, grid, n_in, n_out, own_semantics):
    if ride is None:
        return [], [], [], {}, [], own_semantics, lambda refs: (lambda: None, lambda: None)
    n = len(ride.operands)
    steps = [g - 1 for g in grid]

    def at(step):
        hit = pl.program_id(0) == step[0]
        for axis in range(1, len(grid)):
            hit = jnp.logical_and(hit, pl.program_id(axis) == step[axis])
        return hit

    def hooks(refs):
        ins, outs = refs[n_in:n_in + n], refs[n_in + n + n_out:n_in + 2 * n + n_out]
        sems = refs[n_in + 2 * n + n_out:]

        def before():
            pl.when(at([0] * len(grid)))(lambda: ride.first(ins, outs, sems))
            if ride.mid is not None:
                pl.when(at(steps[:-1] + [max(steps[-1] - RIDE_MID_STEPS, 0)]))(lambda: ride.mid(ins, outs, sems))

        def after():
            pl.when(at(steps))(lambda: ride.last(ins, outs, sems))

        return before, after

    return (ride.operands, [ANY] * n, ride.out_shapes, ride.aliases(n_in, n_out), ride.scratch_shapes,
            ("arbitrary",) * len(grid), hooks)


def _attn_fwd(qk, v, ride=None, t=ATTN_BLOCK):
    s, width = v.shape
    heads = width // HEAD_DIM
    t = min(t, s)
    assert heads % ATTN_HEADS == 0
    grid = (heads // ATTN_HEADS, s // t)
    extra, extra_specs, extra_out, aliases, scratch, semantics, hooks = _ride_along(
        ride, grid, 3, 2, ("parallel", "parallel"))

    def body(*refs):
        q_ref, k_ref, v_ref = refs[:3]
        o_ref, lsum_ref = refs[3 + len(extra):5 + len(extra)]
        before, after = hooks(refs)
        before()
        i = pl.program_id(1)
        tri, _, causal = _attn_consts(t)

        def block(p, start, carry, mask):
            acc, log_rest = carry
            z, _, log_1m_beta = _attn_scores(q_ref[:, _head(p)], k_ref[pl.ds(start, t), _head(p)], mask)
            total = _running_sums(log_1m_beta, tri) + log_rest
            a = _masked(jnp.exp(z + total), mask)
            return acc + _dot(a.astype(BF16), v_ref[pl.ds(start, t), _head(p)], ((1,), (0,))), total[:, 0:1]

        def step(n, carry):
            start = pl.multiple_of((i - 1 - n) * t, t)
            return tuple(block(p, start, carry[p], None) for p in range(ATTN_HEADS))

        zero = (jnp.zeros((t, HEAD_DIM), F32), jnp.zeros((t, 1), F32))
        carry = tuple(block(p, pl.multiple_of(i * t, t), zero, causal) for p in range(ATTN_HEADS))
        carry = lax.fori_loop(0, i, step, carry)
        for p in range(ATTN_HEADS):
            o_ref[:, _head(p)] = carry[p][0].astype(BF16)
            lsum_ref[p] = carry[p][1]
        after()

    q_spec, k_spec, whole, rows = _attn_specs(s, t, heads)
    outs = pl.pallas_call(
        body, name="attn_fwd", grid=grid, in_specs=[q_spec, k_spec, whole] + extra_specs,
        out_specs=(q_spec, rows, *extra_specs),
        out_shape=(_sds((s, width), BF16), _sds((heads, s, 1), F32), *extra_out),
        input_output_aliases=aliases, scratch_shapes=scratch,
        compiler_params=_params(*semantics))(qk, qk, v, *extra)
    return outs[0], outs[1], list(outs[2:])


def _attn_bwd(qk, v, lsum, do, ride=None, t=ATTN_BLOCK):
    s, width = v.shape
    heads = width // HEAD_DIM
    t = min(t, s)
    grid = (heads // ATTN_HEADS, s // t)
    extra, extra_specs, extra_out, aliases, scratch, semantics, hooks = _ride_along(
        ride, grid, 5, 3, ("parallel", "arbitrary"))

    def body(*refs):
        q_ref, k_ref, v_ref, lsum_ref, do_ref = refs[:5]
        dq_ref, dk_ref, dv_ref = refs[5 + len(extra):8 + len(extra)]
        before, after = hooks(refs)
        before()
        i = pl.program_id(1)

        @pl.when(i == 0)
        def _():
            dk_ref[...] = jnp.zeros_like(dk_ref)
            dv_ref[...] = jnp.zeros_like(dv_ref)

        _, triu, causal = _attn_consts(t)

        def block(p, start, carry, mask):
            dq, log_rest, g_before = carry
            q, do_blk = q_ref[:, _head(p)], do_ref[:, _head(p)]
            kt, vt = k_ref[pl.ds(start, t), _head(p)], v_ref[pl.ds(start, t), _head(p)]
            z, e, log_1m_beta = _attn_scores(q, kt, mask)
            upto = _running_sums(log_1m_beta, triu)
            a = _masked(jnp.exp(z + log_rest - upto + log_1m_beta), mask)
            g = a * _dot(do_blk, vt, ((1,), (1,)))
            g_upto = _running_sums(g, triu) + g_before
            sig = jnp.where(z >= 0.0, 1.0, e) / (1.0 + e)
            dz = (_masked(g - sig * g_upto, mask) * (HEAD_DIM ** -0.5)).astype(BF16)
            dk_ref[pl.ds(start, t), _head(p)] += _dot(dz, q, ((0,), (0,)))
            dv_ref[pl.ds(start, t), _head(p)] += _dot(a.astype(BF16), do_blk, ((0,), (0,)))
            return dq + _dot(dz, kt, ((1,), (0,))), log_rest - upto[:, t - 1:t], g_upto[:, t - 1:t]

        def step(n, carry):
            start = pl.multiple_of(n * t, t)
            return tuple(block(p, start, carry[p], None) for p in range(ATTN_HEADS))

        carry = tuple((jnp.zeros((t, HEAD_DIM), F32), lsum_ref[p], jnp.zeros((t, 1), F32)) for p in range(ATTN_HEADS))
        carry = lax.fori_loop(0, i, step, carry)
        for p in range(ATTN_HEADS):
            dq, _, _ = block(p, pl.multiple_of(i * t, t), carry[p], causal)
            dq_ref[:, _head(p)] = dq.astype(BF16)
        after()

    q_spec, k_spec, whole, rows = _attn_specs(s, t, heads)
    outs = pl.pallas_call(
        body, name="attn_bwd", grid=grid, in_specs=[q_spec, k_spec, whole, rows, q_spec] + extra_specs,
        out_specs=(q_spec, whole, whole, *extra_specs),
        out_shape=(_sds((s, width), BF16), _sds((s, width), F32), _sds((s, width), F32), *extra_out),
        input_output_aliases=aliases, scratch_shapes=scratch,
        compiler_params=_params(*semantics))(qk, qk, v, lsum, do, *extra)
    return outs[0], outs[1], outs[2], list(outs[3:])


def _as_rows(a):
    return a.reshape(-1, a.shape[-1])


def _flat_call(name, body, ins, out_dtypes, block_bytes=1 << 20):
    shape = ins[0].shape
    rows, cols = _as_rows(ins[0]).shape
    tr = _tile(rows, max(8, block_bytes // (4 * cols)), 8)
    spec = pl.BlockSpec((tr, cols), lambda i: (i, 0))
    outs = pl.pallas_call(
        body, name=name, grid=(rows // tr,), in_specs=[spec] * len(ins), out_specs=tuple([spec] * len(out_dtypes)),
        out_shape=tuple(_sds((rows, cols), dt) for dt in out_dtypes), compiler_params=_params("parallel"),
    )(*[_as_rows(a) for a in ins])
    return [o.reshape(shape) for o in outs]


def _scalars(*values):
    return jnp.stack([jnp.asarray(v, jnp.int32) for v in values])


def _row_tile(rows, cols):
    return _tile(rows, max(ROWS_PER_TILE, (1 << 20) // (4 * cols)), ROWS_PER_TILE)


def _place_cast(w, chip):
    cols = w.shape[-1]
    rows = w.size // cols
    tr = _row_tile(rows, cols)

    def body(s_ref, w_ref, o_ref):
        o_ref[...] = w_ref[...].astype(BF16)

    out = pl.pallas_call(
        body, name="place_cast",
        grid_spec=pltpu.PrefetchScalarGridSpec(
            num_scalar_prefetch=1, grid=(rows // tr,),
            in_specs=[pl.BlockSpec((tr, cols), lambda i, s: (i, 0))],
            out_specs=pl.BlockSpec((None, tr, cols), lambda i, s: (s[0], i, 0))),
        out_shape=_sds((N_CHIPS, rows, cols), BF16), compiler_params=_params("parallel"),
    )(_scalars(chip), w.reshape(rows, cols))
    return out.reshape((N_CHIPS,) + w.shape)


def _add_halves(g, got, core):
    _, rows, cols = got.shape
    tr = _row_tile(rows, cols)

    def body(s_ref, a_ref, b_ref, o_ref):
        o_ref[...] = (a_ref[...].astype(F32) + b_ref[...].astype(F32)).astype(BF16)

    spec = pl.BlockSpec((None, tr, cols), lambda j, i, s: (j, i, 0))
    return pl.pallas_call(
        body, name="add_halves",
        grid_spec=pltpu.PrefetchScalarGridSpec(
            num_scalar_prefetch=1, grid=(N_CHIPS, rows // tr),
            in_specs=[pl.BlockSpec((None, None, tr, cols), lambda j, i, s: (j, s[0], i, 0)), spec], out_specs=spec),
        out_shape=_sds(got.shape, BF16), compiler_params=_params("parallel", "parallel"),
    )(_scalars(core), g, got)


def _sum_parts(own, others, chip, core):
    _, rows, cols = own.shape
    tr = _row_tile(rows, cols)

    def body(s_ref, a_ref, t_ref, o_ref):
        total = a_ref[...].astype(F32)
        for j in range(N_CHIPS - 1):
            total = total + t_ref[j].astype(F32)
        o_ref[...] = total

    return pl.pallas_call(
        body, name="sum_parts",
        grid_spec=pltpu.PrefetchScalarGridSpec(
            num_scalar_prefetch=1, grid=(rows // tr,),
            in_specs=[pl.BlockSpec((None, tr, cols), lambda i, s: (s[0], i, 0)),
                      pl.BlockSpec((N_CHIPS - 1, tr, cols), lambda i, s: (0, i, 0))],
            out_specs=pl.BlockSpec((None, tr, cols), lambda i, s: (s[1], i, 0))),
        out_shape=_sds((2, rows, cols), F32), compiler_params=_params("parallel"),
    )(_scalars(chip, core), own, others)


def _sum_leading(a, name):
    n = a.shape[0]
    shape = a.shape[1:]
    cols = shape[-1]
    rows = a.size // (n * cols)
    tr = _tile(rows, max(8, (1 << 20) // (4 * cols)), 8)

    def body(a_ref, o_ref):
        total = a_ref[0].astype(F32)
        for j in range(1, n):
            total = total + a_ref[j].astype(F32)
        o_ref[...] = total

    out = pl.pallas_call(
        body, name=name, grid=(rows // tr,), in_specs=[pl.BlockSpec((n, tr, cols), lambda i: (0, i, 0))],
        out_specs=pl.BlockSpec((tr, cols), lambda i: (i, 0)), out_shape=_sds((rows, cols), F32),
        compiler_params=_params("parallel"))(a.reshape(n, rows, cols))
    return out.reshape(shape)


def _adamw(w, g, m, v):
    def body(w_ref, g_ref, m_ref, v_ref, d_ref, nm_ref, nv_ref):
        gv = g_ref[...]
        nm = ADAM_B1 * m_ref[...] + (1.0 - ADAM_B1) * gv
        nv = ADAM_B2 * v_ref[...] + (1.0 - ADAM_B2) * jnp.square(gv)
        nm_ref[...] = nm
        nv_ref[...] = nv
        m_hat = nm / (1.0 - ADAM_B1 ** ADAM_STEP)
        v_hat = nv / (1.0 - ADAM_B2 ** ADAM_STEP)
        d_ref[...] = -ADAM_LR * (m_hat / (jnp.sqrt(v_hat) + ADAM_EPS) + ADAM_WD * w_ref[...])

    return _flat_call("adamw", body, (w, g, m, v), (F32, F32, F32), block_bytes=1 << 19)


def _mesh_position():
    return lax.axis_index("x"), lax.axis_index("y"), lax.axis_index("c")


def _other_chips(x, y):
    return [(1 - x, y), (x, 1 - y), (1 - x, 1 - y)]


def _chip_index(chip):
    return 2 * chip[0] + chip[1]


def _remote(src, dst, send_sem, recv_sem, device):
    return pltpu.make_async_remote_copy(src_ref=src, dst_ref=dst, send_sem=send_sem, recv_sem=recv_sem,
                                        device_id=device, device_id_type=MESH)


def _pieces(view):
    shape = list(view.shape)
    counts = []
    for axis in range(len(shape) - 1):
        want = -(-(view.dtype.itemsize * functools.reduce(lambda a, b: a * b, shape)) // DMA_PIECE_BYTES)
        unit = 1 if axis < len(shape) - 2 else ROWS_PER_TILE
        n, k = shape[axis], 1
        for cand in range(1, n + 1):
            if n % cand == 0 and (n // cand) % unit == 0:
                k = cand
                if cand >= want:
                    break
        counts.append(k)
        shape[axis] = n // k
    grid = [()]
    for axis, k in enumerate(counts):
        grid = [idx + (pl.ds(i * shape[axis], shape[axis]),) for idx in grid for i in range(k)]
    return grid


def _start_remote(src, dst, send_sem, recv_sem, device):
    for idx in _pieces(src):
        _remote(src.at[idx], dst.at[idx], send_sem, recv_sem, device).start()
    return _remote(src, dst, send_sem, recv_sem, device)


class _Ride:
    def __init__(self, operands, out_shapes, aliased, scratch_shapes, first, mid, last):
        self.operands, self.out_shapes, self.aliased = list(operands), list(out_shapes), aliased
        self.scratch_shapes, self.first, self.mid, self.last = list(scratch_shapes), first, mid, last

    def aliases(self, first_in, first_out):
        return {first_in + w: first_out + w for w in range(len(self.operands))} if self.aliased else {}


def _run_ride(name, ride):
    n = len(ride.operands)

    def body(*refs):
        ins, outs, sems = refs[:n], refs[n:2 * n], refs[2 * n:]
        ride.first(ins, outs, sems)
        if ride.mid is not None:
            ride.mid(ins, outs, sems)
        ride.last(ins, outs, sems)

    return pl.pallas_call(
        body, name=name, in_specs=[ANY] * n, out_specs=tuple([ANY] * n), out_shape=tuple(ride.out_shapes),
        input_output_aliases=ride.aliases(0, 0), scratch_shapes=ride.scratch_shapes,
    )(*ride.operands)


def _gather_ride(blocks):
    n = len(blocks)
    halves = [a.shape[1] // 2 for a in blocks]
    assert all(a.shape[1] % 2 == 0 for a in blocks)
    pairs = [(w, j) for w in range(n) for j in range(3)]

    def plan(outs, sems):
        ici_send, ici_recv, d2d_send, d2d_recv = sems
        x, y, c = _mesh_position()
        chips = _other_chips(x, y)

        def half(w, chip, which):
            return outs[w].at[chip, pl.ds(which * halves[w], halves[w])]

        def over_ici(w, j, chip):
            there = half(w, chip, c)
            return there, there, ici_send.at[3 * w + j], ici_recv.at[3 * w + j], (*chips[j], c)

        def over_d2d(w, j, which):
            there = half(w, _chip_index(chips[j]), which)
            return there, there, d2d_send.at[3 * w + j], d2d_recv.at[3 * w + j], (x, y, 1 - c)

        return _chip_index((x, y)), c, chips, over_ici, over_d2d

    def first(ins, outs, sems):
        me, _, _, over_ici, _ = plan(outs, sems)
        for w, j in pairs:
            _start_remote(*over_ici(w, j, me))

    def mid(ins, outs, sems):
        _, c, chips, over_ici, over_d2d = plan(outs, sems)
        for w, j in pairs:
            _remote(*over_ici(w, j, _chip_index(chips[j]))).wait_recv()
            _start_remote(*over_d2d(w, j, c))

    def last(ins, outs, sems):
        me, c, _, over_ici, over_d2d = plan(outs, sems)
        for w, j in pairs:
            _remote(*over_d2d(w, j, 1 - c)).wait_recv()
        for w, j in pairs:
            _remote(*over_ici(w, j, me)).wait_send()
            _remote(*over_d2d(w, j, c)).wait_send()

    return _Ride(blocks, [_sds(a.shape, a.dtype) for a in blocks], True,
                 [pltpu.SemaphoreType.DMA((3 * n,))] * 4, first, mid, last)


def _swap_halves(grads):
    n = len(grads)

    def body(*refs):
        ins, got = refs[:n], refs[n:2 * n]
        send_sem, recv_sem = refs[2 * n:]
        x, y, c = _mesh_position()
        swaps = [_start_remote(ins[w].at[pl.ds(0, N_CHIPS), 1 - c], got[w], send_sem.at[w], recv_sem.at[w],
                               (x, y, 1 - c)) for w in range(n)]
        for cp in swaps:
            cp.wait()

    return pl.pallas_call(
        body, name="swap_halves", in_specs=[ANY] * n, out_specs=tuple([ANY] * n),
        out_shape=tuple(_sds((N_CHIPS,) + a.shape[2:], a.dtype) for a in grads),
        scratch_shapes=[pltpu.SemaphoreType.DMA((n,))] * 2,
    )(*grads)


def _scatter_ride(parts):
    n = len(parts)
    pairs = [(w, j) for w in range(n) for j in range(3)]

    def plan(ins, outs, sems):
        send_sem, recv_sem = sems
        x, y, c = _mesh_position()
        chips = _other_chips(x, y)

        def to_chip(w, j):
            return (ins[w].at[_chip_index(chips[j])], outs[w].at[j], send_sem.at[3 * w + j], recv_sem.at[3 * w + j],
                    (*chips[j], c))

        return to_chip

    def first(ins, outs, sems):
        to_chip = plan(ins, outs, sems)
        for w, j in pairs:
            _start_remote(*to_chip(w, j))

    def last(ins, outs, sems):
        to_chip = plan(ins, outs, sems)
        for w, j in pairs:
            _remote(*to_chip(w, j)).wait_recv()
        for w, j in pairs:
            _remote(*to_chip(w, j)).wait_send()

    return _Ride(parts, [_sds((N_CHIPS - 1,) + a.shape[1:], a.dtype) for a in parts], False,
                 [pltpu.SemaphoreType.DMA((3 * n,))] * 2, first, None, last)


def _join_halves(halves):
    n = len(halves)

    def body(*refs):
        outs = refs[n:2 * n]
        send_sem, recv_sem = refs[2 * n:]
        x, y, c = _mesh_position()
        sibling = (x, y, 1 - c)
        sends = [_start_remote(outs[w].at[c], outs[w].at[c], send_sem.at[w], recv_sem.at[w], sibling)
                 for w in range(n)]
        for w in range(n):
            there = outs[w].at[1 - c]
            _remote(there, there, send_sem.at[w], recv_sem.at[w], sibling).wait_recv()
            sends[w].wait_send()

    return pl.pallas_call(
        body, name="join_halves", in_specs=[ANY] * n, out_specs=tuple([ANY] * n),
        out_shape=tuple(_sds(a.shape, a.dtype) for a in halves),
        input_output_aliases={w: w for w in range(n)},
        scratch_shapes=[pltpu.SemaphoreType.DMA((n,))] * 2,
    )(*halves)


def _gather_small(vec):
    length = vec.shape[1]
    flips = [(fx, fy, fc) for fx in (0, 1) for fy in (0, 1) for fc in (0, 1)][1:]

    def body(v_ref, o_ref, send_sem, recv_sem, local_sem):
        x, y, c = _mesh_position()
        me = 4 * x + 2 * y + c
        local = pltpu.make_async_copy(v_ref, o_ref.at[me], local_sem)
        local.start()
        peers = [(x ^ fx, y ^ fy, c ^ fc) for fx, fy, fc in flips]
        sends = [_remote(v_ref, o_ref.at[me], send_sem.at[k], recv_sem.at[k], peer) for k, peer in enumerate(peers)]
        for cp in sends:
            cp.start()
        for k, (px, py, pc) in enumerate(peers):
            there = o_ref.at[4 * px + 2 * py + pc]
            _remote(there, there, send_sem.at[k], recv_sem.at[k], peers[k]).wait_recv()
        for cp in sends:
            cp.wait_send()
        local.wait()

    return pl.pallas_call(
        body, name="gather_small", in_specs=[ANY], out_specs=ANY, out_shape=_sds((N_DEVICES, 1, length), F32),
        scratch_shapes=[pltpu.SemaphoreType.DMA((N_DEVICES - 1,))] * 2 + [pltpu.SemaphoreType.DMA],
    )(vec)


MATRICES = ("w_in", "w_attn_branch", "w_pool_group", "w_pool_branch", "w_out", "w_up", "w_down", "w_ple", "w_ple_gate")
VECTORS = ("norm_mix_pre", "pool_scale", "norm_mix_post", "norm_ffn_pre", "conv_b", "norm_ffn_post", "norm_ple_post")
WEIGHTS = ("norm_mix_pre", "w_in", "w_attn_branch", "w_pool_group", "pool_scale", "w_pool_branch", "w_out",
           "norm_mix_post", "norm_ffn_pre", "w_up", "conv_w", "conv_b", "w_down", "norm_ffn_post", "w_ple",
           "w_ple_gate", "norm_ple_post")


EARLY = ("w_in",)
LATE = tuple(k for k in MATRICES if k not in EARLY)


def _halves_view(g):
    return g.reshape(N_CHIPS, 2, g.size // (2 * N_CHIPS * g.shape[-1]), g.shape[-1])


def _pair_sums(gmat, names, core):
    flat = [_halves_view(gmat[k]) for k in names]
    return [_add_halves(a, b, core) for a, b in zip(flat, _swap_halves(flat))]


def _local_step(x, p, target, w_in, late_blocks, vec, conv_w, core):
    s, d = x.shape
    groups = len(POOL_WINDOWS)
    aw = w_in.shape[2] // 2
    assert w_in.shape[2] == d

    h1 = _rms_fwd(x, vec["norm_mix_pre"])
    qk = _mm_nn("proj_qk", h1, w_in, BF16, j0=0, nj=1)
    v = _mm_nn("proj_v", h1, w_in, BF16, j0=1, c0=0, cn=aw)
    u = _mm_nn("proj_u", h1, w_in, F32, j0=1, c0=aw, cn=aw)
    ga = _mm_nn("proj_ga", h1, w_in, F32, j0=2, nj=1)
    gp = _mm_nn("proj_gp", h1, w_in, F32, j0=3, nj=1)
    attn, lsum, gathered = _attn_fwd(qk, v, _gather_ride(late_blocks))
    full = dict(zip(LATE, gathered))
    w_ab, w_pb, w_up, w_ple = (full[k] for k in ("w_attn_branch", "w_pool_branch", "w_up", "w_ple"))
    w_pg = full["w_pool_group"].transpose(1, 0, 2, 3)
    w_pg = w_pg.reshape(groups, -1, w_pg.shape[-1])
    w_out, w_down, w_pleg = (full[k].reshape(1, -1, full[k].shape[-1]) for k in ("w_out", "w_down", "w_ple_gate"))
    assert w_pb.shape[1] == aw == w_ab.shape[1]
    ya = _mm_nn("attn_branch", attn, w_ab, F32)
    pooled = _pool_fwd(u)
    pg = _group_nn("pool_group", pooled, w_pg, F32)
    ps = _scale_cols(pg, vec["pool_scale"])
    yp = _mm_nn("pool_branch", ps, w_pb, F32)
    mixed = _mix_fwd(ga, gp, ya, yp)
    mo = _mm_nn("mix_out", mixed, w_out, F32)
    x2, h2 = _post_pre(x, mo, vec["norm_mix_post"], vec["norm_ffn_pre"])
    up = _mm_nn("ffn_up", h2, w_up, F32, tn=1408)
    act = _conv_gelu_fwd(up, conv_w, vec["conv_b"])
    yf = _mm_nn("ffn_down", act, w_down, F32)
    x3 = _post(x2, yf, vec["norm_ffn_post"])
    e = _mm_nn("ple_embed", p, w_ple, F32)
    t = _mm_nn("ple_gate", x3, w_pleg, F32)
    dx4, loss_rows = _final(x3, t, e, vec["norm_ple_post"], target)

    gvec, gmat = {}, {}
    dt, de, gvec["norm_ple_post"] = _bwd_ple(dx4, t, e, vec["norm_ple_post"])
    gmat["w_ple"] = _mm_tn("d_w_ple", p, de, N_CHIPS, BF16)
    gmat["w_ple_gate"] = _mm_tn("d_w_ple_gate", x3, dt, 1, BF16)
    dx3 = _mm_nt("d_x3", dt, w_pleg, F32, add=dx4)
    dyf, gvec["norm_ffn_post"] = _bwd_post(dx3, yf, vec["norm_ffn_post"])
    gmat["w_down"] = _mm_tn("d_w_down", act, dyf, 1, BF16)
    dact = _mm_nt("d_act", dyf, w_down, BF16)
    dup_g, dup_v, dcw_g, dcw_v, dcb_g, dcb_v = _conv_gelu_bwd(up, dact, conv_w, vec["conv_b"])
    dup = jnp.concatenate([dup_g, dup_v], axis=1)
    gconv_w = jnp.concatenate([dcw_g, dcw_v], axis=1)
    gvec["conv_b"] = jnp.concatenate([dcb_g, dcb_v], axis=1)
    gmat["w_up"] = _mm_tn("d_w_up", h2, dup, N_CHIPS, BF16, tn=1408)
    dh2 = _mm_nt("d_h2", dup, w_up, F32, tk=1408)
    dx2, dmo, gvec["norm_ffn_pre"], gvec["norm_mix_post"] = _bwd_mid(
        dx3, dh2, x2, vec["norm_ffn_pre"], mo, vec["norm_mix_post"])
    gmat["w_out"] = _mm_tn("d_w_out", mixed, dmo, 1, BF16)
    dmixed = _mm_nt("d_mixed", dmo, w_out, F32)
    dya, dyp, dga, dgp = _bwd_mix(dmixed, ga, gp, ya, yp)
    gmat["w_attn_branch"] = _mm_tn("d_w_attn_branch", attn, dya, N_CHIPS, BF16)
    dattn = _mm_nt("d_attn", dya, w_ab, BF16)
    gmat["w_pool_branch"] = _mm_tn("d_w_pool_branch", ps, dyp, N_CHIPS, BF16)
    dps = _mm_nt("d_ps", dyp, w_pb, F32)
    dpg, gvec["pool_scale"] = _bwd_pool_scale(dps, pg, vec["pool_scale"])
    g_pg = _group_tn("d_w_pool_group", pooled, dpg, groups, BF16)
    gmat["w_pool_group"] = g_pg.reshape(groups, N_CHIPS, -1, g_pg.shape[-1]).transpose(1, 0, 2, 3)
    dpooled = _group_nn("d_pooled", dpg, w_pg, F32, transpose_w=True)
    du = _pool_bwd(dpooled)
    pair = dict(zip(LATE, _pair_sums(gmat, LATE, core)))
    dq, dk, dv, others = _attn_bwd(qk, v, lsum, dattn, _scatter_ride([pair[k] for k in LATE]))
    others = dict(zip(LATE, others))
    dproj = jnp.concatenate([dq, dk.astype(BF16), dv.astype(BF16), du, dga, dgp], axis=1)
    gmat["w_in"] = _mm_tn("d_w_in", h1, dproj, N_CHIPS, BF16)
    dh1 = _mm_nt("d_h1", dproj, w_in, F32)
    grad_x, gvec["norm_mix_pre"] = _bwd_first(dx2, dh1, x, vec["norm_mix_pre"])
    early = _pair_sums(gmat, EARLY, core)
    pair.update(zip(EARLY, early))
    others.update(zip(EARLY, _run_ride("scatter_to_chips", _scatter_ride(early))))
    return loss_rows[0, 0], grad_x, pair, others, gvec, gconv_w


def _reduce_matrices(pair, others, shapes, chip, core):
    halves = [_sum_parts(pair[k], others[k], chip, core) for k in MATRICES]
    return {k: a.reshape(shapes[k]) for k, a in zip(MATRICES, _join_halves(halves))}


def kernel(x, p, norm_mix_pre, w_in, w_attn_branch, w_pool_group, pool_scale, w_pool_branch, w_out, norm_mix_post, norm_ffn_pre, w_up, conv_w, conv_b, w_down, norm_ffn_post, w_ple, w_ple_gate, norm_ple_post, loss_target, m_norm_mix_pre, m_w_in, m_w_attn_branch, m_w_pool_group, m_pool_scale, m_w_pool_branch, m_w_out, m_norm_mix_post, m_norm_ffn_pre, m_w_up, m_conv_w, m_conv_b, m_w_down, m_norm_ffn_post, m_w_ple, m_w_ple_gate, m_norm_ple_post, v_norm_mix_pre, v_w_in, v_w_attn_branch, v_w_pool_group, v_pool_scale, v_w_pool_branch, v_w_out, v_norm_mix_post, v_norm_ffn_pre, v_w_up, v_conv_w, v_conv_b, v_w_down, v_norm_ffn_post, v_w_ple, v_w_ple_gate, v_norm_ple_post):
    given = dict(locals())
    w = {k: given[k][0] for k in WEIGHTS}
    m = {k: given["m_" + k][0] for k in WEIGHTS}
    v = {k: given["v_" + k][0] for k in WEIGHTS}
    chip = 2 * lax.axis_index("x") + lax.axis_index("y")

    core = lax.axis_index("c")
    w_in_full = _run_ride("gather_weights", _gather_ride([_place_cast(w[k], chip) for k in EARLY]))[0]
    late_blocks = [_place_cast(w[k], chip) for k in LATE]
    vec = {k: w[k].reshape(1, -1) for k in VECTORS}
    taps = _gather_small(w["conv_w"].reshape(1, -1))[:, 0][0::2]
    f2q = w["conv_w"].shape[1]
    conv_w_full = taps.reshape(N_CHIPS, 3, f2q).transpose(1, 0, 2).reshape(3, N_CHIPS * f2q)

    loss_rows, grad_x, pair, others, gvec, gconv_w = _local_step(
        x[0], p[0, 0], loss_target[0], w_in_full, late_blocks, vec, conv_w_full, core)
    loss = lax.psum(loss_rows, ("x", "y", "c"))

    gw = _reduce_matrices(pair, others, {k: w[k].shape for k in MATRICES}, chip, core)
    sizes = [gvec[k].shape[1] for k in VECTORS]
    small = jnp.concatenate([gvec[k] for k in VECTORS] + [gconv_w.reshape(1, -1)], axis=1)
    small = _sum_leading(_gather_small(small), "sum_small")
    offset = 0
    for k, n in zip(VECTORS, sizes):
        gw[k] = small[0, offset:offset + n]
        offset += n
    gconv_w = small[0, offset:].reshape(3, N_CHIPS, f2q)
    gw["conv_w"] = lax.dynamic_index_in_dim(gconv_w, chip, axis=1, keepdims=False)

    delta, new_m, new_v = {}, {}, {}
    for k in MATRICES:
        delta[k], new_m[k], new_v[k] = _adamw(w[k], gw[k], m[k], v[k])
    tiny = VECTORS + ("conv_w",)
    pack = lambda tree: jnp.concatenate([tree[k].reshape(1, -1) for k in tiny], axis=1)
    d_s, m_s, v_s = _adamw(pack(w), pack(gw), pack(m), pack(v))
    offset = 0
    for k in tiny:
        n = w[k].size
        delta[k], new_m[k], new_v[k] = (a[0, offset:offset + n].reshape(w[k].shape) for a in (d_s, m_s, v_s))
        offset += n

    lead = lambda tree: [tree[k].reshape((1,) + w[k].shape) for k in WEIGHTS]
    return (loss, grad_x[None], *lead(gw), *lead(delta), *lead(new_m), *lead(new_v))
```

```python
import functools

import jax
import jax.numpy as jnp
from jax import lax
from jax.experimental import pallas as pl
from jax.experimental.pallas import tpu as pltpu

F32 = jnp.float32
BF16 = jnp.bfloat16

HEAD_DIM = 128
POOL_WINDOWS = (2, 4, 8, 16)
EPS = 1e-6
GELU_C = 0.7978845608028654
GELU_A = 0.044715

ADAM_LR = 0.001
ADAM_B1 = 0.9
ADAM_B2 = 0.999
ADAM_EPS = 1e-08
ADAM_WD = 0.01
ADAM_STEP = 10

N_CHIPS = 4
N_DEVICES = 8
VMEM_LIMIT_BYTES = 52 * 1024 * 1024
ATTN_BLOCK = 256
ATTN_Q_BLOCK = 1024
ATTN_HEADS = 2
POOL_TILE = 128
RIDE_MID_STEPS = 2
DMA_PIECE_BYTES = 512 * 1024
ROWS_PER_TILE = 16
MESH = pl.DeviceIdType.MESH
ANY = pl.BlockSpec(memory_space=pl.ANY)


def _sds(shape, dtype):
    return jax.ShapeDtypeStruct(tuple(shape), dtype)


def _tile(dim, pref, mult=128):
    t = min(pref, dim) // mult * mult
    while t >= mult:
        if dim % t == 0:
            return t
        t -= mult
    return dim


def _params(*semantics):
    return pltpu.CompilerParams(dimension_semantics=semantics or None, vmem_limit_bytes=VMEM_LIMIT_BYTES)


def _sigmoid(v):
    return 1.0 / (1.0 + jnp.exp(-v))


def _rstd(v):
    return lax.rsqrt(jnp.mean(v * v, axis=-1, keepdims=True) + EPS)


def _rms_bwd(dy, gain, v):
    r = _rstd(v)
    vh = v * r
    gy = dy * gain
    return r * (gy - vh * jnp.mean(gy * vh, axis=-1, keepdims=True)), dy * vh


def _ride_along(ride, grid, n_in, n_out, own_semantics):
    if ride is None:
        return [], [], [], {}, [], own_semantics, lambda refs: (lambda: None, lambda: None)
    n = len(ride.operands)
    steps = [g - 1 for g in grid]

    def at(step):
        hit = pl.program_id(0) == step[0]
        for axis in range(1, len(grid)):
            hit = jnp.logical_and(hit, pl.program_id(axis) == step[axis])
        return hit

    def hooks(refs):
        ins, outs = refs[n_in:n_in + n], refs[n_in + n + n_out:n_in + 2 * n + n_out]
        sems = refs[len(refs) - len(ride.scratch_shapes):]

        def before():
            pl.when(at([0] * len(grid)))(lambda: ride.first(ins, outs, sems))
            if ride.mid is not None:
                pl.when(at(steps[:-1] + [max(steps[-1] - RIDE_MID_STEPS, 0)]))(lambda: ride.mid(ins, outs, sems))

        def after():
            pl.when(at(steps))(lambda: ride.last(ins, outs, sems))

        return before, after

    return (ride.operands, [ANY] * n, ride.out_shapes, ride.aliases(n_in, n_out), ride.scratch_shapes,
            ("arbitrary",) * len(grid), hooks)


def _matmul(name, a, b, grid, a_spec, b_spec, o_spec, out, dims, acc_shape, add=None, add_spec=None, ride=None):
    nk = grid[2]
    n_in = 2 if add is None else 3
    extra, extra_specs, extra_out, aliases, ride_scratch, semantics, hooks = _ride_along(
        ride, grid, n_in, 1, ("parallel", "parallel", "arbitrary"))

    def body(*refs):
        before, after = hooks(refs)
        before()
        compute(refs)
        after()

    def compute(refs):
        a_ref, b_ref = refs[:2]
        c_ref = None if add is None else refs[2]
        o_ref = refs[n_in + len(extra)]
        part = lax.dot_general(a_ref[...].astype(BF16), b_ref[...].astype(BF16), (dims, ((), ())),
                               preferred_element_type=F32)

        def finish(r):
            if c_ref is not None:
                r = r + c_ref[...]
            o_ref[...] = r.astype(o_ref.dtype)

        if nk == 1:
            finish(part)
            return
        acc = refs[n_in + 2 * len(extra) + 1]
        kk = pl.program_id(2)

        @pl.when(kk == 0)
        def _():
            acc[...] = part

        @pl.when(jnp.logical_and(kk > 0, kk < nk - 1))
        def _():
            acc[...] += part

        @pl.when(kk == nk - 1)
        def _():
            finish(acc[...] + part)

    operands = (a, b) if add is None else (a, b, add)
    in_specs = [a_spec, b_spec] if add is None else [a_spec, b_spec, add_spec]
    outs = pl.pallas_call(
        body, name=name, grid=grid, in_specs=in_specs + extra_specs, out_specs=(o_spec, *extra_specs),
        out_shape=(out, *extra_out), input_output_aliases=aliases,
        scratch_shapes=([] if nk == 1 else [pltpu.VMEM(acc_shape, F32)]) + ride_scratch,
        compiler_params=_params(*semantics),
    )(*operands, *extra)
    return outs[0] if ride is None else (outs[0], list(outs[1:]))


def _mm_nn(name, a, b3, out_dtype, j0=0, nj=None, c0=0, cn=None, tm=1024, tn=1024, tk=2048, ride=None):
    m, k = a.shape
    nj_all, kb, width = b3.shape
    assert kb == k
    if cn is None:
        cn = width
        nj = nj_all - j0 if nj is None else nj
    else:
        nj = 1
    n = nj * cn
    tm, tn, tk = _tile(m, tm, 8), _tile(cn, tn), _tile(k, tk)
    assert c0 % tn == 0
    nb, cb = cn // tn, c0 // tn
    return _matmul(
        name, a, b3, (m // tm, n // tn, k // tk),
        pl.BlockSpec((tm, tk), lambda i, j, kk: (i, kk)),
        pl.BlockSpec((None, tk, tn), lambda i, j, kk: (j0 + j // nb, kk, cb + j % nb)),
        pl.BlockSpec((tm, tn), lambda i, j, kk: (i, j)),
        _sds((m, n), out_dtype), ((1,), (0,)), (tm, tn), ride=ride)


def _mm_nt(name, a, b3, out_dtype, add=None, tm=1024, tn=1024, tk=2048, ride=None):
    m, kc = a.shape
    nj, n, kj = b3.shape
    assert kc == nj * kj
    tm, tn, tk = _tile(m, tm, 8), _tile(n, tn), _tile(kj, tk)
    kb = kj // tk
    o_spec = pl.BlockSpec((tm, tn), lambda i, j, kk: (i, j))
    return _matmul(
        name, a, b3, (m // tm, n // tn, kc // tk),
        pl.BlockSpec((tm, tk), lambda i, j, kk: (i, kk)),
        pl.BlockSpec((None, tn, tk), lambda i, j, kk: (kk // kb, j, kk % kb)),
        o_spec, _sds((m, n), out_dtype), ((1,), (1,)), (tm, tn), add=add, add_spec=o_spec, ride=ride)


def _mm_tn(name, a, b, nj, out_dtype, tm=512, tn=1024, ts=4096):
    s, m = a.shape
    s2, n = b.shape
    assert s == s2 and n % nj == 0
    width = n // nj
    tm, tn, ts = _tile(m, tm), _tile(width, tn), _tile(s, ts)
    nb = width // tn
    return _matmul(
        name, a, b, (m // tm, n // tn, s // ts),
        pl.BlockSpec((ts, tm), lambda i, j, kk: (kk, i)),
        pl.BlockSpec((ts, tn), lambda i, j, kk: (kk, j)),
        pl.BlockSpec((None, tm, tn), lambda i, j, kk: (j // nb, i, j % nb)),
        _sds((nj, m, width), out_dtype), ((0,), (0,)), (tm, tn))


def _group_nn(name, a, w, out_dtype, transpose_w=False, tm=1024):
    s, gc = a.shape
    g, c, _ = w.shape
    tm = _tile(s, tm, 8)
    return _matmul(
        name, a, w, (s // tm, g, 1),
        pl.BlockSpec((tm, c), lambda i, j, kk: (i, j)),
        pl.BlockSpec((None, c, c), lambda i, j, kk: (j, 0, 0)),
        pl.BlockSpec((tm, c), lambda i, j, kk: (i, j)),
        _sds((s, gc), out_dtype), ((1,), (1,)) if transpose_w else ((1,), (0,)), (tm, c))


def _group_tn(name, a, b, g, out_dtype, ts=4096):
    s, gc = a.shape
    c = gc // g
    ts = _tile(s, ts, 8)
    return _matmul(
        name, a, b, (g, 1, s // ts),
        pl.BlockSpec((ts, c), lambda i, j, kk: (kk, i)),
        pl.BlockSpec((ts, c), lambda i, j, kk: (kk, i)),
        pl.BlockSpec((None, c, c), lambda i, j, kk: (i, 0, 0)),
        _sds((g, c, c), out_dtype), ((0,), (0,)), (c, c))


def _row(tm, d):
    return pl.BlockSpec((tm, d), lambda i: (i, 0))


def _vec(d):
    return pl.BlockSpec((1, d), lambda i: (0, 0))


def _rows_call(name, body, ins, in_specs, outs, out_specs, steps, accumulates):
    return pl.pallas_call(
        body, name=name, grid=(steps,), in_specs=in_specs, out_specs=out_specs, out_shape=outs,
        compiler_params=_params("arbitrary" if accumulates else "parallel"),
    )(*ins)


def _accumulate(ref, value):
    @pl.when(pl.program_id(0) == 0)
    def _():
        ref[...] = value

    @pl.when(pl.program_id(0) > 0)
    def _():
        ref[...] += value


def _colsum(v):
    return jnp.sum(v, axis=0, keepdims=True)


def _rms_fwd(x, gain, tm=256):
    s, d = x.shape

    def body(x_ref, g_ref, h_ref):
        v = x_ref[...]
        h_ref[...] = (v * _rstd(v) * g_ref[...]).astype(BF16)

    return _rows_call("rms_fwd", body, (x, gain), [_row(tm, d), _vec(d)], _sds((s, d), BF16), _row(tm, d),
                      s // tm, False)


def _mix_fwd(ga, gp, ya, yp, tm=256):
    s, d = ga.shape

    def body(ga_ref, gp_ref, ya_ref, yp_ref, o_ref):
        o_ref[...] = (_sigmoid(ga_ref[...]) * ya_ref[...] + _sigmoid(gp_ref[...]) * yp_ref[...]).astype(BF16)

    return _rows_call("mix_fwd", body, (ga, gp, ya, yp), [_row(tm, d)] * 4, _sds((s, d), BF16), _row(tm, d),
                      s // tm, False)


def _scale_cols(v, scale, tm=256):
    s, d = v.shape

    def body(v_ref, s_ref, o_ref):
        o_ref[...] = (v_ref[...] * s_ref[...]).astype(BF16)

    return _rows_call("pool_scale_fwd", body, (v, scale), [_row(tm, d), _vec(d)], _sds((s, d), BF16), _row(tm, d),
                      s // tm, False)


def _post_pre(x, y, gain_post, gain_pre, tm=256):
    s, d = x.shape

    def body(x_ref, y_ref, gp_ref, gn_ref, x2_ref, h_ref):
        y = y_ref[...]
        x2 = x_ref[...] + y * _rstd(y) * gp_ref[...]
        x2_ref[...] = x2
        h_ref[...] = (x2 * _rstd(x2) * gn_ref[...]).astype(BF16)

    return _rows_call("post_pre", body, (x, y, gain_post, gain_pre), [_row(tm, d), _row(tm, d), _vec(d), _vec(d)],
                      (_sds((s, d), F32), _sds((s, d), BF16)), (_row(tm, d), _row(tm, d)), s // tm, False)


def _post(x, y, gain_post, tm=256):
    s, d = x.shape

    def body(x_ref, y_ref, gp_ref, o_ref):
        y = y_ref[...]
        o_ref[...] = x_ref[...] + y * _rstd(y) * gp_ref[...]

    return _rows_call("post", body, (x, y, gain_post), [_row(tm, d), _row(tm, d), _vec(d)], _sds((s, d), F32),
                      _row(tm, d), s // tm, False)


def _final(x3, t, e, gain, target, tm=256):
    s, d = x3.shape

    def body(x_ref, t_ref, e_ref, g_ref, y_ref, dx_ref, loss_ref):
        pe = _sigmoid(t_ref[...]) * e_ref[...]
        diff = x_ref[...] + pe * _rstd(pe) * g_ref[...] - y_ref[...]
        dx_ref[...] = diff * (1.0 / d)
        part = 0.5 * jnp.sum(jnp.mean(diff * diff, axis=-1, keepdims=True), axis=0, keepdims=True)
        _accumulate(loss_ref, jnp.broadcast_to(part, loss_ref.shape))

    return _rows_call("final", body, (x3, t, e, gain, target),
                      [_row(tm, d), _row(tm, d), _row(tm, d), _vec(d), _row(tm, d)],
                      (_sds((s, d), F32), _sds((8, 128), F32)),
                      (_row(tm, d), pl.BlockSpec((8, 128), lambda i: (0, 0))), s // tm, True)


def _bwd_ple(dx4, t, e, gain, tm=256):
    s, d = dx4.shape

    def body(dx_ref, t_ref, e_ref, g_ref, dt_ref, de_ref, dg_ref):
        sg = _sigmoid(t_ref[...])
        ev = e_ref[...]
        dpe, dgain = _rms_bwd(dx_ref[...], g_ref[...], sg * ev)
        de_ref[...] = (dpe * sg).astype(BF16)
        dt_ref[...] = (dpe * ev * sg * (1.0 - sg)).astype(BF16)
        _accumulate(dg_ref, _colsum(dgain))

    return _rows_call("bwd_ple", body, (dx4, t, e, gain), [_row(tm, d), _row(tm, d), _row(tm, d), _vec(d)],
                      (_sds((s, d), BF16), _sds((s, d), BF16), _sds((1, d), F32)),
                      (_row(tm, d), _row(tm, d), _vec(d)), s // tm, True)


def _bwd_post(dx, y, gain, tm=256):
    s, d = dx.shape

    def body(dx_ref, y_ref, g_ref, dy_ref, dg_ref):
        dy, dgain = _rms_bwd(dx_ref[...], g_ref[...], y_ref[...])
        dy_ref[...] = dy.astype(BF16)
        _accumulate(dg_ref, _colsum(dgain))

    return _rows_call("bwd_post", body, (dx, y, gain), [_row(tm, d), _row(tm, d), _vec(d)],
                      (_sds((s, d), BF16), _sds((1, d), F32)), (_row(tm, d), _vec(d)), s // tm, True)


def _bwd_mid(dx3, dh2, x2, gain_pre, mo, gain_post, tm=128):
    s, d = dx3.shape

    def body(dx3_ref, dh_ref, x2_ref, gn_ref, mo_ref, gp_ref, dx2_ref, dmo_ref, dgn_ref, dgp_ref):
        dv, dgn = _rms_bwd(dh_ref[...], gn_ref[...], x2_ref[...])
        dx2 = dx3_ref[...] + dv
        dx2_ref[...] = dx2
        dmo, dgp = _rms_bwd(dx2, gp_ref[...], mo_ref[...])
        dmo_ref[...] = dmo.astype(BF16)
        _accumulate(dgn_ref, _colsum(dgn))
        _accumulate(dgp_ref, _colsum(dgp))

    return _rows_call("bwd_mid", body, (dx3, dh2, x2, gain_pre, mo, gain_post),
                      [_row(tm, d), _row(tm, d), _row(tm, d), _vec(d), _row(tm, d), _vec(d)],
                      (_sds((s, d), F32), _sds((s, d), BF16), _sds((1, d), F32), _sds((1, d), F32)),
                      (_row(tm, d), _row(tm, d), _vec(d), _vec(d)), s // tm, True)


def _bwd_first(dx2, dh1, x, gain, tm=256):
    s, d = dx2.shape

    def body(dx2_ref, dh_ref, x_ref, g_ref, dx_ref, dg_ref):
        dv, dgain = _rms_bwd(dh_ref[...], g_ref[...], x_ref[...])
        dx_ref[...] = dx2_ref[...] + dv
        _accumulate(dg_ref, _colsum(dgain))

    return _rows_call("bwd_first", body, (dx2, dh1, x, gain), [_row(tm, d), _row(tm, d), _row(tm, d), _vec(d)],
                      (_sds((s, d), F32), _sds((1, d), F32)), (_row(tm, d), _vec(d)), s // tm, True)


def _bwd_mix(dmixed, ga, gp, ya, yp, tm=128):
    s, d = dmixed.shape

    def body(dm_ref, ga_ref, gp_ref, ya_ref, yp_ref, dya_ref, dyp_ref, dga_ref, dgp_ref):
        dm = dm_ref[...]
        sa, sp = _sigmoid(ga_ref[...]), _sigmoid(gp_ref[...])
        dya_ref[...] = (dm * sa).astype(BF16)
        dyp_ref[...] = (dm * sp).astype(BF16)
        dga_ref[...] = (dm * ya_ref[...] * sa * (1.0 - sa)).astype(BF16)
        dgp_ref[...] = (dm * yp_ref[...] * sp * (1.0 - sp)).astype(BF16)

    return _rows_call("bwd_mix", body, (dmixed, ga, gp, ya, yp), [_row(tm, d)] * 5,
                      (_sds((s, d), BF16),) * 4, (_row(tm, d),) * 4, s // tm, False)


def _bwd_pool_scale(dps, pg, scale, tm=256):
    s, d = dps.shape

    def body(d_ref, pg_ref, s_ref, dpg_ref, ds_ref):
        dv = d_ref[...]
        dpg_ref[...] = (dv * s_ref[...]).astype(BF16)
        _accumulate(ds_ref, _colsum(dv * pg_ref[...]))

    return _rows_call("bwd_pool_scale", body, (dps, pg, scale), [_row(tm, d), _row(tm, d), _vec(d)],
                      (_sds((s, d), BF16), _sds((1, d), F32)), (_row(tm, d), _vec(d)), s // tm, True)


def _shift_down(v, k, rows):
    return jnp.where(rows >= k, pltpu.roll(v, k, 0), 0.0)


def _shift_up(v, k, rows):
    s = v.shape[0]
    return jnp.where(rows < s - k, pltpu.roll(v, s - k, 0), 0.0)


def _window_pick(group, sums, rows):
    total = sums[-1]
    width = jnp.full((), POOL_WINDOWS[-1], jnp.int32)
    for g in range(len(POOL_WINDOWS) - 2, -1, -1):
        total = jnp.where(group == g, sums[g], total)
        width = jnp.where(group == g, POOL_WINDOWS[g], width)
    return total, jnp.minimum(rows + 1, width).astype(F32)


def _doubling(v, shift, rows):
    sums, k = [], 1
    for _ in POOL_WINDOWS:
        v = v + shift(v, k, rows)
        sums.append(v)
        k *= 2
    return sums


def _pool_fwd(u):
    s, width = u.shape
    per_group = width // len(POOL_WINDOWS) // POOL_TILE

    def body(u_ref, o_ref):
        v = u_ref[...]
        rows = lax.broadcasted_iota(jnp.int32, (s, 1), 0)
        total, count = _window_pick(pl.program_id(0), _doubling(v, _shift_down, rows), rows)
        o_ref[...] = (total / count - v).astype(BF16)

    spec = pl.BlockSpec((s, POOL_TILE), lambda g, j: (0, g * per_group + j))
    return pl.pallas_call(body, name="pool_fwd", grid=(len(POOL_WINDOWS), per_group), in_specs=[spec], out_specs=spec,
                          out_shape=_sds((s, width), BF16), compiler_params=_params("parallel", "parallel"))(u)


def _pool_bwd(dpooled):
    s, width = dpooled.shape
    per_group = width // len(POOL_WINDOWS) // POOL_TILE

    def body(d_ref, o_ref):
        dv = d_ref[...]
        rows = lax.broadcasted_iota(jnp.int32, (s, 1), 0)
        group = pl.program_id(0)
        _, count = _window_pick(group, [dv] * len(POOL_WINDOWS), rows)
        total, _ = _window_pick(group, _doubling(dv / count, _shift_up, rows), rows)
        o_ref[...] = (total - dv).astype(BF16)

    spec = pl.BlockSpec((s, POOL_TILE), lambda g, j: (0, g * per_group + j))
    return pl.pallas_call(body, name="pool_bwd", grid=(len(POOL_WINDOWS), per_group), in_specs=[spec], out_specs=spec,
                          out_shape=_sds((s, width), BF16), compiler_params=_params("parallel", "parallel"))(dpooled)


def _conv(v, w_ref, b_ref, rows):
    out = b_ref[...] + _shift_down(v, 2, rows) * w_ref[0:1, :]
    out = out + _shift_down(v, 1, rows) * w_ref[1:2, :]
    return out + v * w_ref[2:3, :]


def _gelu_parts(v):
    th = jnp.tanh(GELU_C * (v + GELU_A * v * v * v))
    return th, 0.5 * v * (1.0 + th)


def _conv_gelu_fwd(up, conv_w, conv_b, tc=128):
    s, f2 = up.shape
    f = f2 // 2
    nb = f // tc

    def body(g_ref, v_ref, wg_ref, wv_ref, bg_ref, bv_ref, o_ref):
        rows = lax.broadcasted_iota(jnp.int32, (s, 1), 0)
        _, gl = _gelu_parts(_conv(g_ref[...], wg_ref, bg_ref, rows))
        o_ref[...] = (gl * _conv(v_ref[...], wv_ref, bv_ref, rows)).astype(BF16)

    lo = lambda r: pl.BlockSpec((r, tc), lambda j: (0, j))
    hi = lambda r: pl.BlockSpec((r, tc), lambda j: (0, j + nb))
    return pl.pallas_call(
        body, name="conv_gelu_fwd", grid=(nb,), in_specs=[lo(s), hi(s), lo(3), hi(3), lo(1), hi(1)], out_specs=lo(s),
        out_shape=_sds((s, f), BF16), compiler_params=_params("parallel"))(up, up, conv_w, conv_w, conv_b, conv_b)


def _conv_gelu_bwd(up, dact, conv_w, conv_b, tc=128):
    s, f2 = up.shape
    f = f2 // 2
    nb = f // tc

    def body(g_ref, v_ref, d_ref, wg_ref, wv_ref, bg_ref, bv_ref, dg_ref, dv_ref, dwg_ref, dwv_ref, dbg_ref, dbv_ref):
        rows = lax.broadcasted_iota(jnp.int32, (s, 1), 0)
        dact = d_ref[...].astype(F32)

        def back(pre, dc, w_ref, dpre_ref, dw_ref, db_ref):
            db_ref[...] = _colsum(dc)
            dw_ref[0:1, :] = _colsum(dc * _shift_down(pre, 2, rows))
            dw_ref[1:2, :] = _colsum(dc * _shift_down(pre, 1, rows))
            dw_ref[2:3, :] = _colsum(dc * pre)
            dpre = dc * w_ref[2:3, :] + _shift_up(dc, 1, rows) * w_ref[1:2, :] + _shift_up(dc, 2, rows) * w_ref[0:1, :]
            dpre_ref[...] = dpre.astype(BF16)

        gate, val = g_ref[...], v_ref[...]
        cg = _conv(gate, wg_ref, bg_ref, rows)
        cv = _conv(val, wv_ref, bv_ref, rows)
        th, gl = _gelu_parts(cg)
        dgl = 0.5 * (1.0 + th) + 0.5 * cg * (1.0 - th * th) * GELU_C * (1.0 + 3.0 * GELU_A * cg * cg)
        back(val, dact * gl, wv_ref, dv_ref, dwv_ref, dbv_ref)
        back(gate, dact * cv * dgl, wg_ref, dg_ref, dwg_ref, dbg_ref)

    lo = lambda r: pl.BlockSpec((r, tc), lambda j: (0, j))
    hi = lambda r: pl.BlockSpec((r, tc), lambda j: (0, j + nb))
    return pl.pallas_call(
        body, name="conv_gelu_bwd", grid=(nb,),
        in_specs=[lo(s), hi(s), lo(s), lo(3), hi(3), lo(1), hi(1)],
        out_specs=(lo(s), lo(s), lo(3), lo(3), lo(1), lo(1)),
        out_shape=(_sds((s, f), BF16), _sds((s, f), BF16), _sds((3, f), F32), _sds((3, f), F32),
                   _sds((1, f), F32), _sds((1, f), F32)),
        compiler_params=_params("parallel"))(up, up, dact, conv_w, conv_w, conv_b, conv_b)


def _dot(a, b, dims):
    return lax.dot_general(a, b, (dims, ((), ())), preferred_element_type=F32)


def _running_sums(v, tri):
    hi = v.astype(BF16)
    lo = (v - hi.astype(F32)).astype(BF16)
    return _dot(hi, tri, ((1,), (0,))) + _dot(lo, tri, ((1,), (0,)))


def _attn_scores(q, kt, mask):
    z = _dot(q, kt, ((1,), (1,))) * (HEAD_DIM ** -0.5)
    e = jnp.exp(-jnp.abs(z))
    log_1m_beta = -(jnp.maximum(z, 0.0) + jnp.log(1.0 + e))
    return z, e, log_1m_beta if mask is None else jnp.where(mask, log_1m_beta, 0.0)


def _masked(v, mask):
    return v if mask is None else jnp.where(mask, v, 0.0)


def _attn_consts(t):
    rows = lax.broadcasted_iota(jnp.int32, (t, t), 0)
    cols = lax.broadcasted_iota(jnp.int32, (t, t), 1)
    return (rows >= cols).astype(BF16), (rows <= cols).astype(BF16), cols < rows


def _attn_specs(s, tq, heads):
    lanes = ATTN_HEADS * HEAD_DIM
    steps = heads // ATTN_HEADS
    q = pl.BlockSpec((tq, lanes), lambda h, i: (i, h))
    k = pl.BlockSpec((s, lanes), lambda h, i: (0, steps + h))
    whole = pl.BlockSpec((s, lanes), lambda h, i: (0, h))
    rows = pl.BlockSpec((ATTN_HEADS, tq, 1), lambda h, i: (h, i, 0))
    return q, k, whole, rows


def _attn_blocks(s):
    t = min(ATTN_BLOCK, s)
    tq = min(ATTN_Q_BLOCK, s)
    assert tq % t == 0 and s % tq == 0
    return t, tq, tq // t


def _head(p):
    return pl.ds(p * HEAD_DIM, HEAD_DIM)


def _attn_fwd(qk, v, ride=None):
    s, width = v.shape
    heads = width // HEAD_DIM
    t, tq, ratio = _attn_blocks(s)
    assert heads % ATTN_HEADS == 0
    grid = (heads // ATTN_HEADS, s // tq)
    extra, extra_specs, extra_out, aliases, scratch, semantics, hooks = _ride_along(
        ride, grid, 3, 2, ("parallel", "parallel"))

    def body(*refs):
        q_ref, k_ref, v_ref = refs[:3]
        o_ref, lsum_ref = refs[3 + len(extra):5 + len(extra)]
        before, after = hooks(refs)
        before()
        i = pl.program_id(1)
        tri, _, causal = _attn_consts(t)

        def block(p, rows, start, carry, mask):
            acc, log_rest = carry
            z, _, log_1m_beta = _attn_scores(q_ref[rows, _head(p)], k_ref[pl.ds(start, t), _head(p)], mask)
            total = _running_sums(log_1m_beta, tri) + log_rest
            a = _masked(jnp.exp(z + total), mask)
            return acc + _dot(a.astype(BF16), v_ref[pl.ds(start, t), _head(p)], ((1,), (0,))), total[:, 0:1]

        def own_rows(p):
            chunks = []
            for rc in range(ratio):
                carry = (jnp.zeros((t, HEAD_DIM), F32), jnp.zeros((t, 1), F32))
                for r in range(rc, -1, -1):
                    carry = block(p, pl.ds(rc * t, t), pl.multiple_of(i * tq + r * t, t), carry,
                                  causal if r == rc else None)
                chunks.append(carry)
            return tuple(jnp.concatenate([ch[j] for ch in chunks], axis=0) for j in range(2))

        def step(n, carry):
            start = pl.multiple_of((i * ratio - 1 - n) * t, t)
            return tuple(block(p, pl.ds(0, tq), start, carry[p], None) for p in range(ATTN_HEADS))

        carry = lax.fori_loop(0, i * ratio, step, tuple(own_rows(p) for p in range(ATTN_HEADS)))
        for p in range(ATTN_HEADS):
            o_ref[:, _head(p)] = carry[p][0].astype(BF16)
            lsum_ref[p] = carry[p][1]
        after()

    q_spec, k_spec, whole, rows = _attn_specs(s, tq, heads)
    outs = pl.pallas_call(
        body, name="attn_fwd", grid=grid, in_specs=[q_spec, k_spec, whole] + extra_specs,
        out_specs=(q_spec, rows, *extra_specs),
        out_shape=(_sds((s, width), BF16), _sds((heads, s, 1), F32), *extra_out),
        input_output_aliases=aliases, scratch_shapes=scratch,
        compiler_params=_params(*semantics))(qk, qk, v, *extra)
    return outs[0], outs[1], list(outs[2:])


def _attn_bwd(qk, v, lsum, do, ride=None):
    s, width = v.shape
    heads = width // HEAD_DIM
    t, tq, ratio = _attn_blocks(s)
    grid = (heads // ATTN_HEADS, s // tq)
    extra, extra_specs, extra_out, aliases, scratch, semantics, hooks = _ride_along(
        ride, grid, 5, 3, ("parallel", "arbitrary"))

    def body(*refs):
        q_ref, k_ref, v_ref, lsum_ref, do_ref = refs[:5]
        dq_ref, dk_ref, dv_ref = refs[5 + len(extra):8 + len(extra)]
        before, after = hooks(refs)
        before()
        i = pl.program_id(1)

        @pl.when(i == 0)
        def _():
            dk_ref[...] = jnp.zeros_like(dk_ref)
            dv_ref[...] = jnp.zeros_like(dv_ref)

        _, triu, causal = _attn_consts(t)

        def block(p, rows, start, carry, mask):
            dq, log_rest, g_before = carry
            q, do_blk = q_ref[rows, _head(p)], do_ref[rows, _head(p)]
            kt, vt = k_ref[pl.ds(start, t), _head(p)], v_ref[pl.ds(start, t), _head(p)]
            z, e, log_1m_beta = _attn_scores(q, kt, mask)
            upto = _running_sums(log_1m_beta, triu)
            a = _masked(jnp.exp(z + log_rest - upto + log_1m_beta), mask)
            g = a * _dot(do_blk, vt, ((1,), (1,)))
            g_upto = _running_sums(g, triu) + g_before
            sig = jnp.where(z >= 0.0, 1.0, e) / (1.0 + e)
            dz = (_masked(g - sig * g_upto, mask) * (HEAD_DIM ** -0.5)).astype(BF16)
            dk_ref[pl.ds(start, t), _head(p)] += _dot(dz, q, ((0,), (0,)))
            dv_ref[pl.ds(start, t), _head(p)] += _dot(a.astype(BF16), do_blk, ((0,), (0,)))
            return dq + _dot(dz, kt, ((1,), (0,))), log_rest - upto[:, t - 1:t], g_upto[:, t - 1:t]

        def step(n, carry):
            start = pl.multiple_of(n * t, t)
            return tuple(block(p, pl.ds(0, tq), start, carry[p], None) for p in range(ATTN_HEADS))

        carry = tuple((jnp.zeros((tq, HEAD_DIM), F32), lsum_ref[p], jnp.zeros((tq, 1), F32))
                      for p in range(ATTN_HEADS))
        carry = lax.fori_loop(0, i * ratio, step, carry)
        for p in range(ATTN_HEADS):
            for rc in range(ratio):
                part = tuple(val[rc * t:(rc + 1) * t] for val in carry[p])
                for r in range(rc + 1):
                    part = block(p, pl.ds(rc * t, t), pl.multiple_of(i * tq + r * t, t), part,
                                 causal if r == rc else None)
                dq_ref[pl.ds(rc * t, t), _head(p)] = part[0].astype(BF16)
        after()

    q_spec, k_spec, whole, rows = _attn_specs(s, tq, heads)
    outs = pl.pallas_call(
        body, name="attn_bwd", grid=grid, in_specs=[q_spec, k_spec, whole, rows, q_spec] + extra_specs,
        out_specs=(q_spec, whole, whole, *extra_specs),
        out_shape=(_sds((s, width), BF16), _sds((s, width), F32), _sds((s, width), F32), *extra_out),
        input_output_aliases=aliases, scratch_shapes=scratch,
        compiler_params=_params(*semantics))(qk, qk, v, lsum, do, *extra)
    return outs[0], outs[1], outs[2], list(outs[3:])


def _as_rows(a):
    return a.reshape(-1, a.shape[-1])


def _flat_call(name, body, ins, out_dtypes, block_bytes=1 << 20):
    shape = ins[0].shape
    rows, cols = _as_rows(ins[0]).shape
    tr = _tile(rows, max(8, block_bytes // (4 * cols)), 8)
    spec = pl.BlockSpec((tr, cols), lambda i: (i, 0))
    outs = pl.pallas_call(
        body, name=name, grid=(rows // tr,), in_specs=[spec] * len(ins), out_specs=tuple([spec] * len(out_dtypes)),
        out_shape=tuple(_sds((rows, cols), dt) for dt in out_dtypes), compiler_params=_params("parallel"),
    )(*[_as_rows(a) for a in ins])
    return [o.reshape(shape) for o in outs]


def _scalars(*values):
    return jnp.stack([jnp.asarray(v, jnp.int32) for v in values])


def _row_tile(rows, cols):
    return _tile(rows, max(ROWS_PER_TILE, (1 << 20) // (4 * cols)), ROWS_PER_TILE)


def _place_cast(w, chip):
    cols = w.shape[-1]
    rows = w.size // cols
    tr = _row_tile(rows, cols)

    def body(s_ref, w_ref, o_ref):
        o_ref[...] = w_ref[...].astype(BF16)

    out = pl.pallas_call(
        body, name="place_cast",
        grid_spec=pltpu.PrefetchScalarGridSpec(
            num_scalar_prefetch=1, grid=(rows // tr,),
            in_specs=[pl.BlockSpec((tr, cols), lambda i, s: (i, 0))],
            out_specs=pl.BlockSpec((None, tr, cols), lambda i, s: (s[0], i, 0))),
        out_shape=_sds((N_CHIPS, rows, cols), BF16), compiler_params=_params("parallel"),
    )(_scalars(chip), w.reshape(rows, cols))
    return out.reshape((N_CHIPS,) + w.shape)


def _add_halves(g, got, core):
    _, rows, cols = got.shape
    tr = _row_tile(rows, cols)

    def body(s_ref, a_ref, b_ref, o_ref):
        o_ref[...] = (a_ref[...].astype(F32) + b_ref[...].astype(F32)).astype(BF16)

    spec = pl.BlockSpec((None, tr, cols), lambda j, i, s: (j, i, 0))
    return pl.pallas_call(
        body, name="add_halves",
        grid_spec=pltpu.PrefetchScalarGridSpec(
            num_scalar_prefetch=1, grid=(N_CHIPS, rows // tr),
            in_specs=[pl.BlockSpec((None, None, tr, cols), lambda j, i, s: (j, s[0], i, 0)), spec], out_specs=spec),
        out_shape=_sds(got.shape, BF16), compiler_params=_params("parallel", "parallel"),
    )(_scalars(core), g, got)


def _sum_parts(own, others, chip, core):
    _, rows, cols = own.shape
    tr = _row_tile(rows, cols)

    def body(s_ref, a_ref, t_ref, o_ref):
        total = a_ref[...].astype(F32)
        for j in range(N_CHIPS - 1):
            total = total + t_ref[j].astype(F32)
        o_ref[...] = total

    return pl.pallas_call(
        body, name="sum_parts",
        grid_spec=pltpu.PrefetchScalarGridSpec(
            num_scalar_prefetch=1, grid=(rows // tr,),
            in_specs=[pl.BlockSpec((None, tr, cols), lambda i, s: (s[0], i, 0)),
                      pl.BlockSpec((N_CHIPS - 1, tr, cols), lambda i, s: (0, i, 0))],
            out_specs=pl.BlockSpec((None, tr, cols), lambda i, s: (s[1], i, 0))),
        out_shape=_sds((2, rows, cols), F32), compiler_params=_params("parallel"),
    )(_scalars(chip, core), own, others)


def _sum_leading(a, name):
    n = a.shape[0]
    shape = a.shape[1:]
    cols = shape[-1]
    rows = a.size // (n * cols)
    tr = _tile(rows, max(8, (1 << 20) // (4 * cols)), 8)

    def body(a_ref, o_ref):
        total = a_ref[0].astype(F32)
        for j in range(1, n):
            total = total + a_ref[j].astype(F32)
        o_ref[...] = total

    out = pl.pallas_call(
        body, name=name, grid=(rows // tr,), in_specs=[pl.BlockSpec((n, tr, cols), lambda i: (0, i, 0))],
        out_specs=pl.BlockSpec((tr, cols), lambda i: (i, 0)), out_shape=_sds((rows, cols), F32),
        compiler_params=_params("parallel"))(a.reshape(n, rows, cols))
    return out.reshape(shape)


def _adamw(w, g, m, v):
    def body(w_ref, g_ref, m_ref, v_ref, d_ref, nm_ref, nv_ref):
        gv = g_ref[...]
        nm = ADAM_B1 * m_ref[...] + (1.0 - ADAM_B1) * gv
        nv = ADAM_B2 * v_ref[...] + (1.0 - ADAM_B2) * jnp.square(gv)
        nm_ref[...] = nm
        nv_ref[...] = nv
        m_hat = nm / (1.0 - ADAM_B1 ** ADAM_STEP)
        v_hat = nv / (1.0 - ADAM_B2 ** ADAM_STEP)
        d_ref[...] = -ADAM_LR * (m_hat / (jnp.sqrt(v_hat) + ADAM_EPS) + ADAM_WD * w_ref[...])

    return _flat_call("adamw", body, (w, g, m, v), (F32, F32, F32), block_bytes=1 << 19)


def _mesh_position():
    return lax.axis_index("x"), lax.axis_index("y"), lax.axis_index("c")


def _other_chips(x, y):
    return [(1 - x, y), (x, 1 - y), (1 - x, 1 - y)]


def _chip_index(chip):
    return 2 * chip[0] + chip[1]


def _remote(src, dst, send_sem, recv_sem, device):
    return pltpu.make_async_remote_copy(src_ref=src, dst_ref=dst, send_sem=send_sem, recv_sem=recv_sem,
                                        device_id=device, device_id_type=MESH)


def _pieces(view):
    shape = list(view.shape)
    counts = []
    for axis in range(len(shape) - 1):
        want = -(-(view.dtype.itemsize * functools.reduce(lambda a, b: a * b, shape)) // DMA_PIECE_BYTES)
        unit = 1 if axis < len(shape) - 2 else ROWS_PER_TILE
        n, k = shape[axis], 1
        for cand in range(1, n + 1):
            if n % cand == 0 and (n // cand) % unit == 0:
                k = cand
                if cand >= want:
                    break
        counts.append(k)
        shape[axis] = n // k
    grid = [()]
    for axis, k in enumerate(counts):
        grid = [idx + (pl.ds(i * shape[axis], shape[axis]),) for idx in grid for i in range(k)]
    return grid


def _start_remote(src, dst, send_sem, recv_sem, device):
    for idx in _pieces(src):
        _remote(src.at[idx], dst.at[idx], send_sem, recv_sem, device).start()
    return _remote(src, dst, send_sem, recv_sem, device)


class _Ride:
    def __init__(self, operands, out_shapes, aliased, scratch_shapes, first, mid, last):
        self.operands, self.out_shapes, self.aliased = list(operands), list(out_shapes), aliased
        self.scratch_shapes, self.first, self.mid, self.last = list(scratch_shapes), first, mid, last

    def aliases(self, first_in, first_out):
        return {first_in + w: first_out + w for w in range(len(self.operands))} if self.aliased else {}


def _run_ride(name, ride):
    n = len(ride.operands)

    def body(*refs):
        ins, outs, sems = refs[:n], refs[n:2 * n], refs[2 * n:]
        ride.first(ins, outs, sems)
        if ride.mid is not None:
            ride.mid(ins, outs, sems)
        ride.last(ins, outs, sems)

    return pl.pallas_call(
        body, name=name, in_specs=[ANY] * n, out_specs=tuple([ANY] * n), out_shape=tuple(ride.out_shapes),
        input_output_aliases=ride.aliases(0, 0), scratch_shapes=ride.scratch_shapes,
    )(*ride.operands)


def _gather_ride(blocks):
    n = len(blocks)
    halves = [a.shape[1] // 2 for a in blocks]
    assert all(a.shape[1] % 2 == 0 for a in blocks)
    pairs = [(w, j) for w in range(n) for j in range(3)]

    def plan(outs, sems):
        ici_send, ici_recv, d2d_send, d2d_recv = sems
        x, y, c = _mesh_position()
        chips = _other_chips(x, y)

        def half(w, chip, which):
            return outs[w].at[chip, pl.ds(which * halves[w], halves[w])]

        def over_ici(w, j, chip):
            there = half(w, chip, c)
            return there, there, ici_send.at[3 * w + j], ici_recv.at[3 * w + j], (*chips[j], c)

        def over_d2d(w, j, which):
            there = half(w, _chip_index(chips[j]), which)
            return there, there, d2d_send.at[3 * w + j], d2d_recv.at[3 * w + j], (x, y, 1 - c)

        return _chip_index((x, y)), c, chips, over_ici, over_d2d

    def first(ins, outs, sems):
        me, _, _, over_ici, _ = plan(outs, sems)
        for w, j in pairs:
            _start_remote(*over_ici(w, j, me))

    def mid(ins, outs, sems):
        _, c, chips, over_ici, over_d2d = plan(outs, sems)
        for w, j in pairs:
            _remote(*over_ici(w, j, _chip_index(chips[j]))).wait_recv()
            _start_remote(*over_d2d(w, j, c))

    def last(ins, outs, sems):
        me, c, _, over_ici, over_d2d = plan(outs, sems)
        for w, j in pairs:
            _remote(*over_d2d(w, j, 1 - c)).wait_recv()
        for w, j in pairs:
            _remote(*over_ici(w, j, me)).wait_send()
            _remote(*over_d2d(w, j, c)).wait_send()

    return _Ride(blocks, [_sds(a.shape, a.dtype) for a in blocks], True,
                 [pltpu.SemaphoreType.DMA((3 * n,))] * 4, first, mid, last)


def _swap_halves(grads):
    n = len(grads)

    def body(*refs):
        ins, got = refs[:n], refs[n:2 * n]
        send_sem, recv_sem = refs[2 * n:]
        x, y, c = _mesh_position()
        swaps = [_start_remote(ins[w].at[pl.ds(0, N_CHIPS), 1 - c], got[w], send_sem.at[w], recv_sem.at[w],
                               (x, y, 1 - c)) for w in range(n)]
        for cp in swaps:
            cp.wait()

    return pl.pallas_call(
        body, name="swap_halves", in_specs=[ANY] * n, out_specs=tuple([ANY] * n),
        out_shape=tuple(_sds((N_CHIPS,) + a.shape[2:], a.dtype) for a in grads),
        scratch_shapes=[pltpu.SemaphoreType.DMA((n,))] * 2,
    )(*grads)


def _scatter_ride(parts):
    n = len(parts)
    pairs = [(w, j) for w in range(n) for j in range(3)]

    def plan(ins, outs, sems):
        send_sem, recv_sem = sems
        x, y, c = _mesh_position()
        chips = _other_chips(x, y)

        def to_chip(w, j):
            return (ins[w].at[_chip_index(chips[j])], outs[w].at[j], send_sem.at[3 * w + j], recv_sem.at[3 * w + j],
                    (*chips[j], c))

        return to_chip

    def first(ins, outs, sems):
        to_chip = plan(ins, outs, sems)
        for w, j in pairs:
            _start_remote(*to_chip(w, j))

    def last(ins, outs, sems):
        to_chip = plan(ins, outs, sems)
        for w, j in pairs:
            _remote(*to_chip(w, j)).wait_recv()
        for w, j in pairs:
            _remote(*to_chip(w, j)).wait_send()

    return _Ride(parts, [_sds((N_CHIPS - 1,) + a.shape[1:], a.dtype) for a in parts], False,
                 [pltpu.SemaphoreType.DMA((3 * n,))] * 2, first, None, last)


def _join_halves(halves):
    n = len(halves)

    def body(*refs):
        outs = refs[n:2 * n]
        send_sem, recv_sem = refs[2 * n:]
        x, y, c = _mesh_position()
        sibling = (x, y, 1 - c)
        sends = [_start_remote(outs[w].at[c], outs[w].at[c], send_sem.at[w], recv_sem.at[w], sibling)
                 for w in range(n)]
        for w in range(n):
            there = outs[w].at[1 - c]
            _remote(there, there, send_sem.at[w], recv_sem.at[w], sibling).wait_recv()
            sends[w].wait_send()

    return pl.pallas_call(
        body, name="join_halves", in_specs=[ANY] * n, out_specs=tuple([ANY] * n),
        out_shape=tuple(_sds(a.shape, a.dtype) for a in halves),
        input_output_aliases={w: w for w in range(n)},
        scratch_shapes=[pltpu.SemaphoreType.DMA((n,))] * 2,
    )(*halves)


def _gather_small(vec):
    length = vec.shape[1]
    flips = [(fx, fy, fc) for fx in (0, 1) for fy in (0, 1) for fc in (0, 1)][1:]

    def body(v_ref, o_ref, send_sem, recv_sem, local_sem):
        x, y, c = _mesh_position()
        me = 4 * x + 2 * y + c
        local = pltpu.make_async_copy(v_ref, o_ref.at[me], local_sem)
        local.start()
        peers = [(x ^ fx, y ^ fy, c ^ fc) for fx, fy, fc in flips]
        sends = [_remote(v_ref, o_ref.at[me], send_sem.at[k], recv_sem.at[k], peer) for k, peer in enumerate(peers)]
        for cp in sends:
            cp.start()
        for k, (px, py, pc) in enumerate(peers):
            there = o_ref.at[4 * px + 2 * py + pc]
            _remote(there, there, send_sem.at[k], recv_sem.at[k], peers[k]).wait_recv()
        for cp in sends:
            cp.wait_send()
        local.wait()

    return pl.pallas_call(
        body, name="gather_small", in_specs=[ANY], out_specs=ANY, out_shape=_sds((N_DEVICES, 1, length), F32),
        scratch_shapes=[pltpu.SemaphoreType.DMA((N_DEVICES - 1,))] * 2 + [pltpu.SemaphoreType.DMA],
    )(vec)


MATRICES = ("w_in", "w_attn_branch", "w_pool_group", "w_pool_branch", "w_out", "w_up", "w_down", "w_ple", "w_ple_gate")
VECTORS = ("norm_mix_pre", "pool_scale", "norm_mix_post", "norm_ffn_pre", "conv_b", "norm_ffn_post", "norm_ple_post")
WEIGHTS = ("norm_mix_pre", "w_in", "w_attn_branch", "w_pool_group", "pool_scale", "w_pool_branch", "w_out",
           "norm_mix_post", "norm_ffn_pre", "w_up", "conv_w", "conv_b", "w_down", "norm_ffn_post", "w_ple",
           "w_ple_gate", "norm_ple_post")


EARLY = ("w_in",)
LATE = tuple(k for k in MATRICES if k not in EARLY)
RIDES = (("w_ple_gate",), ("w_ple",), ("w_attn_branch", "w_pool_branch"), ("w_out", "w_pool_group"),
         ("w_up", "w_down"))
assert sorted(k for names in RIDES for k in names) == sorted(LATE)


def _halves_view(g):
    return g.reshape(N_CHIPS, 2, g.size // (2 * N_CHIPS * g.shape[-1]), g.shape[-1])


def _pair_sums(gmat, names, core):
    flat = [_halves_view(gmat[k]) for k in names]
    return [_add_halves(a, b, core) for a, b in zip(flat, _swap_halves(flat))]


def _local_step(x, p, target, w_in, late_blocks, vec, conv_w, core):
    s, d = x.shape
    groups = len(POOL_WINDOWS)
    aw = w_in.shape[2] // 2
    assert w_in.shape[2] == d

    h1 = _rms_fwd(x, vec["norm_mix_pre"])
    full = {}

    def carrying(names):
        return _gather_ride([late_blocks[k] for k in names])

    def carried(names, outs):
        full.update(zip(names, outs))

    qk, got = _mm_nn("proj_qk", h1, w_in, BF16, j0=0, nj=1, ride=carrying(RIDES[0]))
    carried(RIDES[0], got)
    v = _mm_nn("proj_v", h1, w_in, BF16, j0=1, c0=0, cn=aw)
    u, got = _mm_nn("proj_u", h1, w_in, F32, j0=1, c0=aw, cn=aw, ride=carrying(RIDES[1]))
    carried(RIDES[1], got)
    ga, got = _mm_nn("proj_ga", h1, w_in, F32, j0=2, nj=1, ride=carrying(RIDES[2]))
    carried(RIDES[2], got)
    gp, got = _mm_nn("proj_gp", h1, w_in, F32, j0=3, nj=1, ride=carrying(RIDES[3]))
    carried(RIDES[3], got)
    attn, lsum, got = _attn_fwd(qk, v, carrying(RIDES[4]))
    carried(RIDES[4], got)
    w_ab, w_pb, w_up, w_ple = (full[k] for k in ("w_attn_branch", "w_pool_branch", "w_up", "w_ple"))
    w_pg = full["w_pool_group"].transpose(1, 0, 2, 3)
    w_pg = w_pg.reshape(groups, -1, w_pg.shape[-1])
    w_out, w_down, w_pleg = (full[k].reshape(1, -1, full[k].shape[-1]) for k in ("w_out", "w_down", "w_ple_gate"))
    assert w_pb.shape[1] == aw == w_ab.shape[1]
    ya = _mm_nn("attn_branch", attn, w_ab, F32)
    pooled = _pool_fwd(u)
    pg = _group_nn("pool_group", pooled, w_pg, F32)
    ps = _scale_cols(pg, vec["pool_scale"])
    yp = _mm_nn("pool_branch", ps, w_pb, F32)
    mixed = _mix_fwd(ga, gp, ya, yp)
    mo = _mm_nn("mix_out", mixed, w_out, F32)
    x2, h2 = _post_pre(x, mo, vec["norm_mix_post"], vec["norm_ffn_pre"])
    up = _mm_nn("ffn_up", h2, w_up, F32, tn=1408)
    act = _conv_gelu_fwd(up, conv_w, vec["conv_b"])
    yf = _mm_nn("ffn_down", act, w_down, F32, tk=2816)
    x3 = _post(x2, yf, vec["norm_ffn_post"])
    e = _mm_nn("ple_embed", p, w_ple, F32)
    t = _mm_nn("ple_gate", x3, w_pleg, F32)
    dx4, loss_rows = _final(x3, t, e, vec["norm_ple_post"], target)

    gvec, gmat = {}, {}
    dt, de, gvec["norm_ple_post"] = _bwd_ple(dx4, t, e, vec["norm_ple_post"])
    gmat["w_ple"] = _mm_tn("d_w_ple", p, de, N_CHIPS, BF16)
    gmat["w_ple_gate"] = _mm_tn("d_w_ple_gate", x3, dt, 1, BF16)
    dx3 = _mm_nt("d_x3", dt, w_pleg, F32, add=dx4)
    dyf, gvec["norm_ffn_post"] = _bwd_post(dx3, yf, vec["norm_ffn_post"])
    gmat["w_down"] = _mm_tn("d_w_down", act, dyf, 1, BF16)
    dact = _mm_nt("d_act", dyf, w_down, BF16, tn=1408)
    dup_g, dup_v, dcw_g, dcw_v, dcb_g, dcb_v = _conv_gelu_bwd(up, dact, conv_w, vec["conv_b"])
    dup = jnp.concatenate([dup_g, dup_v], axis=1)
    gconv_w = jnp.concatenate([dcw_g, dcw_v], axis=1)
    gvec["conv_b"] = jnp.concatenate([dcb_g, dcb_v], axis=1)
    gmat["w_up"] = _mm_tn("d_w_up", h2, dup, N_CHIPS, BF16, tn=1408)
    dh2 = _mm_nt("d_h2", dup, w_up, F32, tk=2816)
    dx2, dmo, gvec["norm_ffn_pre"], gvec["norm_mix_post"] = _bwd_mid(
        dx3, dh2, x2, vec["norm_ffn_pre"], mo, vec["norm_mix_post"])
    gmat["w_out"] = _mm_tn("d_w_out", mixed, dmo, 1, BF16)
    dmixed = _mm_nt("d_mixed", dmo, w_out, F32)
    dya, dyp, dga, dgp = _bwd_mix(dmixed, ga, gp, ya, yp)
    gmat["w_attn_branch"] = _mm_tn("d_w_attn_branch", attn, dya, N_CHIPS, BF16)
    dattn = _mm_nt("d_attn", dya, w_ab, BF16)
    gmat["w_pool_branch"] = _mm_tn("d_w_pool_branch", ps, dyp, N_CHIPS, BF16)
    dps = _mm_nt("d_ps", dyp, w_pb, F32)
    dpg, gvec["pool_scale"] = _bwd_pool_scale(dps, pg, vec["pool_scale"])
    g_pg = _group_tn("d_w_pool_group", pooled, dpg, groups, BF16)
    gmat["w_pool_group"] = g_pg.reshape(groups, N_CHIPS, -1, g_pg.shape[-1]).transpose(1, 0, 2, 3)
    dpooled = _group_nn("d_pooled", dpg, w_pg, F32, transpose_w=True)
    du = _pool_bwd(dpooled)
    pair = dict(zip(LATE, _pair_sums(gmat, LATE, core)))
    dq, dk, dv, others = _attn_bwd(qk, v, lsum, dattn, _scatter_ride([pair[k] for k in LATE]))
    others = dict(zip(LATE, others))
    dproj = jnp.concatenate([dq, dk.astype(BF16), dv.astype(BF16), du, dga, dgp], axis=1)
    gmat["w_in"] = _mm_tn("d_w_in", h1, dproj, N_CHIPS, BF16)
    early = _pair_sums(gmat, EARLY, core)
    pair.update(zip(EARLY, early))
    dh1, got = _mm_nt("d_h1", dproj, w_in, F32, ride=_scatter_ride(early))
    others.update(zip(EARLY, got))
    grad_x, gvec["norm_mix_pre"] = _bwd_first(dx2, dh1, x, vec["norm_mix_pre"])
    return loss_rows[0, 0], grad_x, pair, others, gvec, gconv_w


def _reduce_matrices(pair, others, shapes, chip, core):
    halves = [_sum_parts(pair[k], others[k], chip, core) for k in MATRICES]
    return {k: a.reshape(shapes[k]) for k, a in zip(MATRICES, _join_halves(halves))}


def kernel(x, p, norm_mix_pre, w_in, w_attn_branch, w_pool_group, pool_scale, w_pool_branch, w_out, norm_mix_post, norm_ffn_pre, w_up, conv_w, conv_b, w_down, norm_ffn_post, w_ple, w_ple_gate, norm_ple_post, loss_target, m_norm_mix_pre, m_w_in, m_w_attn_branch, m_w_pool_group, m_pool_scale, m_w_pool_branch, m_w_out, m_norm_mix_post, m_norm_ffn_pre, m_w_up, m_conv_w, m_conv_b, m_w_down, m_norm_ffn_post, m_w_ple, m_w_ple_gate, m_norm_ple_post, v_norm_mix_pre, v_w_in, v_w_attn_branch, v_w_pool_group, v_pool_scale, v_w_pool_branch, v_w_out, v_norm_mix_post, v_norm_ffn_pre, v_w_up, v_conv_w, v_conv_b, v_w_down, v_norm_ffn_post, v_w_ple, v_w_ple_gate, v_norm_ple_post):
    given = dict(locals())
    w = {k: given[k][0] for k in WEIGHTS}
    m = {k: given["m_" + k][0] for k in WEIGHTS}
    v = {k: given["v_" + k][0] for k in WEIGHTS}
    chip = 2 * lax.axis_index("x") + lax.axis_index("y")

    core = lax.axis_index("c")
    w_in_full = _run_ride("gather_weights", _gather_ride([_place_cast(w[k], chip) for k in EARLY]))[0]
    late_blocks = {k: _place_cast(w[k], chip) for k in LATE}
    vec = {k: w[k].reshape(1, -1) for k in VECTORS}
    taps = _gather_small(w["conv_w"].reshape(1, -1))[:, 0][0::2]
    f2q = w["conv_w"].shape[1]
    conv_w_full = taps.reshape(N_CHIPS, 3, f2q).transpose(1, 0, 2).reshape(3, N_CHIPS * f2q)

    loss_rows, grad_x, pair, others, gvec, gconv_w = _local_step(
        x[0], p[0, 0], loss_target[0], w_in_full, late_blocks, vec, conv_w_full, core)
    loss = lax.psum(loss_rows, ("x", "y", "c"))

    gw = _reduce_matrices(pair, others, {k: w[k].shape for k in MATRICES}, chip, core)
    sizes = [gvec[k].shape[1] for k in VECTORS]
    small = jnp.concatenate([gvec[k] for k in VECTORS] + [gconv_w.reshape(1, -1)], axis=1)
    small = _sum_leading(_gather_small(small), "sum_small")
    offset = 0
    for k, n in zip(VECTORS, sizes):
        gw[k] = small[0, offset:offset + n]
        offset += n
    gconv_w = small[0, offset:].reshape(3, N_CHIPS, f2q)
    gw["conv_w"] = lax.dynamic_index_in_dim(gconv_w, chip, axis=1, keepdims=False)

    delta, new_m, new_v = {}, {}, {}
    for k in MATRICES:
        delta[k], new_m[k], new_v[k] = _adamw(w[k], gw[k], m[k], v[k])
    tiny = VECTORS + ("conv_w",)
    pack = lambda tree: jnp.concatenate([tree[k].reshape(1, -1) for k in tiny], axis=1)
    d_s, m_s, v_s = _adamw(pack(w), pack(gw), pack(m), pack(v))
    offset = 0
    for k in tiny:
        n = w[k].size
        delta[k], new_m[k], new_v[k] = (a[0, offset:offset + n].reshape(w[k].shape) for a in (d_s, m_s, v_s))
        offset += n

    lead = lambda tree: [tree[k].reshape((1,) + w[k].shape) for k in WEIGHTS]
    return (loss, grad_x[None], *lead(gw), *lead(delta), *lead(new_m), *lead(new_v))
```

```python
import functools

import jax
import jax.numpy as jnp
from jax import lax
from jax.experimental import pallas as pl
from jax.experimental.pallas import tpu as pltpu

F32 = jnp.float32
BF16 = jnp.bfloat16

HEAD_DIM = 128
POOL_WINDOWS = (2, 4, 8, 16)
EPS = 1e-6
GELU_C = 0.7978845608028654
GELU_A = 0.044715

ADAM_LR = 0.001
ADAM_B1 = 0.9
ADAM_B2 = 0.999
ADAM_EPS = 1e-08
ADAM_WD = 0.01
ADAM_STEP = 10

N_CHIPS = 4
N_DEVICES = 8
VMEM_LIMIT_BYTES = 52 * 1024 * 1024
ATTN_BLOCK = 256
ATTN_Q_BLOCK = 1024
ATTN_HEADS = 2
POOL_TILE = 128
RIDE_MID_STEPS = 2
DMA_PIECE_BYTES = 512 * 1024
ROWS_PER_TILE = 16
MESH = pl.DeviceIdType.MESH
ANY = pl.BlockSpec(memory_space=pl.ANY)


def _sds(shape, dtype):
    return jax.ShapeDtypeStruct(tuple(shape), dtype)


def _tile(dim, pref, mult=128):
    t = min(pref, dim) // mult * mult
    while t >= mult:
        if dim % t == 0:
            return t
        t -= mult
    return dim


def _params(*semantics):
    return pltpu.CompilerParams(dimension_semantics=semantics or None, vmem_limit_bytes=VMEM_LIMIT_BYTES)


def _sigmoid(v):
    return 1.0 / (1.0 + jnp.exp(-v))


def _rstd(v):
    return lax.rsqrt(jnp.mean(v * v, axis=-1, keepdims=True) + EPS)


def _rms_bwd(dy, gain, v):
    r = _rstd(v)
    vh = v * r
    gy = dy * gain
    return r * (gy - vh * jnp.mean(gy * vh, axis=-1, keepdims=True)), dy * vh


def _ride_along(ride, grid, n_in, n_out, own_semantics):
    if ride is None:
        return [], [], [], {}, [], own_semantics, lambda refs: (lambda: None, lambda: None)
    n = len(ride.operands)
    steps = [g - 1 for g in grid]

    def at(step):
        hit = pl.program_id(0) == step[0]
        for axis in range(1, len(grid)):
            hit = jnp.logical_and(hit, pl.program_id(axis) == step[axis])
        return hit

    def hooks(refs):
        ins, outs = refs[n_in:n_in + n], refs[n_in + n + n_out:n_in + 2 * n + n_out]
        sems = refs[len(refs) - len(ride.scratch_shapes):]

        def before():
            pl.when(at([0] * len(grid)))(lambda: ride.first(ins, outs, sems))
            if ride.mid is not None:
                pl.when(at(steps[:-1] + [max(steps[-1] - RIDE_MID_STEPS, 0)]))(lambda: ride.mid(ins, outs, sems))

        def after():
            pl.when(at(steps))(lambda: ride.last(ins, outs, sems))

        return before, after

    return (ride.operands, [ANY] * n, ride.out_shapes, ride.aliases(n_in, n_out), ride.scratch_shapes,
            ("arbitrary",) * len(grid), hooks)


def _matmul(name, a, b, grid, a_spec, b_spec, o_spec, out, dims, acc_shape, add=None, add_spec=None, ride=None):
    nk = grid[2]
    n_in = 2 if add is None else 3
    extra, extra_specs, extra_out, aliases, ride_scratch, semantics, hooks = _ride_along(
        ride, grid, n_in, 1, ("parallel", "parallel", "arbitrary"))

    def body(*refs):
        before, after = hooks(refs)
        before()
        compute(refs)
        after()

    def compute(refs):
        a_ref, b_ref = refs[:2]
        c_ref = None if add is None else refs[2]
        o_ref = refs[n_in + len(extra)]
        part = lax.dot_general(a_ref[...].astype(BF16), b_ref[...].astype(BF16), (dims, ((), ())),
                               preferred_element_type=F32)

        def finish(r):
            if c_ref is not None:
                r = r + c_ref[...]
            o_ref[...] = r.astype(o_ref.dtype)

        if nk == 1:
            finish(part)
            return
        acc = refs[n_in + 2 * len(extra) + 1]
        kk = pl.program_id(2)

        @pl.when(kk == 0)
        def _():
            acc[...] = part

        @pl.when(jnp.logical_and(kk > 0, kk < nk - 1))
        def _():
            acc[...] += part

        @pl.when(kk == nk - 1)
        def _():
            finish(acc[...] + part)

    operands = (a, b) if add is None else (a, b, add)
    in_specs = [a_spec, b_spec] if add is None else [a_spec, b_spec, add_spec]
    outs = pl.pallas_call(
        body, name=name, grid=grid, in_specs=in_specs + extra_specs, out_specs=(o_spec, *extra_specs),
        out_shape=(out, *extra_out), input_output_aliases=aliases,
        scratch_shapes=([] if nk == 1 else [pltpu.VMEM(acc_shape, F32)]) + ride_scratch,
        compiler_params=_params(*semantics),
    )(*operands, *extra)
    return outs[0] if ride is None else (outs[0], list(outs[1:]))


def _mm_nn(name, a, b3, out_dtype, j0=0, nj=None, c0=0, cn=None, tm=1024, tn=1024, tk=2048, ride=None):
    m, k = a.shape
    nj_all, kb, width = b3.shape
    assert kb == k
    if cn is None:
        cn = width
        nj = nj_all - j0 if nj is None else nj
    else:
        nj = 1
    n = nj * cn
    tm, tn, tk = _tile(m, tm, 8), _tile(cn, tn), _tile(k, tk)
    assert c0 % tn == 0
    nb, cb = cn // tn, c0 // tn
    return _matmul(
        name, a, b3, (m // tm, n // tn, k // tk),
        pl.BlockSpec((tm, tk), lambda i, j, kk: (i, kk)),
        pl.BlockSpec((None, tk, tn), lambda i, j, kk: (j0 + j // nb, kk, cb + j % nb)),
        pl.BlockSpec((tm, tn), lambda i, j, kk: (i, j)),
        _sds((m, n), out_dtype), ((1,), (0,)), (tm, tn), ride=ride)


def _mm_nt(name, a, b3, out_dtype, add=None, tm=1024, tn=1024, tk=2048, ride=None):
    m, kc = a.shape
    nj, n, kj = b3.shape
    assert kc == nj * kj
    tm, tn, tk = _tile(m, tm, 8), _tile(n, tn), _tile(kj, tk)
    kb = kj // tk
    o_spec = pl.BlockSpec((tm, tn), lambda i, j, kk: (i, j))
    return _matmul(
        name, a, b3, (m // tm, n // tn, kc // tk),
        pl.BlockSpec((tm, tk), lambda i, j, kk: (i, kk)),
        pl.BlockSpec((None, tn, tk), lambda i, j, kk: (kk // kb, j, kk % kb)),
        o_spec, _sds((m, n), out_dtype), ((1,), (1,)), (tm, tn), add=add, add_spec=o_spec, ride=ride)


def _mm_tn(name, a, b, nj, out_dtype, tm=512, tn=1024, ts=4096):
    s, m = a.shape
    s2, n = b.shape
    assert s == s2 and n % nj == 0
    width = n // nj
    tm, tn, ts = _tile(m, tm), _tile(width, tn), _tile(s, ts)
    nb = width // tn
    return _matmul(
        name, a, b, (m // tm, n // tn, s // ts),
        pl.BlockSpec((ts, tm), lambda i, j, kk: (kk, i)),
        pl.BlockSpec((ts, tn), lambda i, j, kk: (kk, j)),
        pl.BlockSpec((None, tm, tn), lambda i, j, kk: (j // nb, i, j % nb)),
        _sds((nj, m, width), out_dtype), ((0,), (0,)), (tm, tn))


def _group_nn(name, a, w, out_dtype, transpose_w=False, tm=1024):
    s, gc = a.shape
    g, c, _ = w.shape
    tm = _tile(s, tm, 8)
    return _matmul(
        name, a, w, (s // tm, g, 1),
        pl.BlockSpec((tm, c), lambda i, j, kk: (i, j)),
        pl.BlockSpec((None, c, c), lambda i, j, kk: (j, 0, 0)),
        pl.BlockSpec((tm, c), lambda i, j, kk: (i, j)),
        _sds((s, gc), out_dtype), ((1,), (1,)) if transpose_w else ((1,), (0,)), (tm, c))


def _group_tn(name, a, b, g, out_dtype, ts=4096):
    s, gc = a.shape
    c = gc // g
    ts = _tile(s, ts, 8)
    return _matmul(
        name, a, b, (g, 1, s // ts),
        pl.BlockSpec((ts, c), lambda i, j, kk: (kk, i)),
        pl.BlockSpec((ts, c), lambda i, j, kk: (kk, i)),
        pl.BlockSpec((None, c, c), lambda i, j, kk: (i, 0, 0)),
        _sds((g, c, c), out_dtype), ((0,), (0,)), (c, c))


def _row(tm, d):
    return pl.BlockSpec((tm, d), lambda i: (i, 0))


def _vec(d):
    return pl.BlockSpec((1, d), lambda i: (0, 0))


def _rows_call(name, body, ins, in_specs, outs, out_specs, steps, accumulates):
    return pl.pallas_call(
        body, name=name, grid=(steps,), in_specs=in_specs, out_specs=out_specs, out_shape=outs,
        compiler_params=_params("arbitrary" if accumulates else "parallel"),
    )(*ins)


def _accumulate(ref, value):
    @pl.when(pl.program_id(0) == 0)
    def _():
        ref[...] = value

    @pl.when(pl.program_id(0) > 0)
    def _():
        ref[...] += value


def _colsum(v):
    return jnp.sum(v, axis=0, keepdims=True)


def _rms_fwd(x, gain, tm=256):
    s, d = x.shape

    def body(x_ref, g_ref, h_ref):
        v = x_ref[...]
        h_ref[...] = (v * _rstd(v) * g_ref[...]).astype(BF16)

    return _rows_call("rms_fwd", body, (x, gain), [_row(tm, d), _vec(d)], _sds((s, d), BF16), _row(tm, d),
                      s // tm, False)


def _mix_fwd(ga, gp, ya, yp, tm=256):
    s, d = ga.shape

    def body(ga_ref, gp_ref, ya_ref, yp_ref, o_ref):
        ga, gp, ya, yp = (ref[...].astype(F32) for ref in (ga_ref, gp_ref, ya_ref, yp_ref))
        o_ref[...] = (_sigmoid(ga) * ya + _sigmoid(gp) * yp).astype(BF16)

    return _rows_call("mix_fwd", body, (ga, gp, ya, yp), [_row(tm, d)] * 4, _sds((s, d), BF16), _row(tm, d),
                      s // tm, False)


def _scale_cols(v, scale, tm=256):
    s, d = v.shape

    def body(v_ref, s_ref, o_ref):
        o_ref[...] = (v_ref[...] * s_ref[...]).astype(BF16)

    return _rows_call("pool_scale_fwd", body, (v, scale), [_row(tm, d), _vec(d)], _sds((s, d), BF16), _row(tm, d),
                      s // tm, False)


def _post_pre(x, y, gain_post, gain_pre, tm=256):
    s, d = x.shape

    def body(x_ref, y_ref, gp_ref, gn_ref, x2_ref, h_ref):
        y = y_ref[...]
        x2 = x_ref[...] + y * _rstd(y) * gp_ref[...]
        x2_ref[...] = x2
        h_ref[...] = (x2 * _rstd(x2) * gn_ref[...]).astype(BF16)

    return _rows_call("post_pre", body, (x, y, gain_post, gain_pre), [_row(tm, d), _row(tm, d), _vec(d), _vec(d)],
                      (_sds((s, d), F32), _sds((s, d), BF16)), (_row(tm, d), _row(tm, d)), s // tm, False)


def _post(x, y, gain_post, tm=256):
    s, d = x.shape

    def body(x_ref, y_ref, gp_ref, o_ref):
        y = y_ref[...]
        o_ref[...] = x_ref[...] + y * _rstd(y) * gp_ref[...]

    return _rows_call("post", body, (x, y, gain_post), [_row(tm, d), _row(tm, d), _vec(d)], _sds((s, d), F32),
                      _row(tm, d), s // tm, False)


def _final(x3, t, e, gain, target, tm=256):
    s, d = x3.shape

    def body(x_ref, t_ref, e_ref, g_ref, y_ref, dx_ref, dt_ref, de_ref, dg_ref, loss_ref):
        sg = _sigmoid(t_ref[...])
        ev = e_ref[...]
        pe = sg * ev
        diff = x_ref[...] + pe * _rstd(pe) * g_ref[...] - y_ref[...]
        dx = diff * (1.0 / d)
        dx_ref[...] = dx
        part = 0.5 * jnp.sum(jnp.mean(diff * diff, axis=-1, keepdims=True), axis=0, keepdims=True)
        _accumulate(loss_ref, jnp.broadcast_to(part, loss_ref.shape))
        dpe, dgain = _rms_bwd(dx, g_ref[...], pe)
        de_ref[...] = (dpe * sg).astype(BF16)
        dt_ref[...] = (dpe * ev * sg * (1.0 - sg)).astype(BF16)
        _accumulate(dg_ref, _colsum(dgain))

    return _rows_call("final", body, (x3, t, e, gain, target),
                      [_row(tm, d), _row(tm, d), _row(tm, d), _vec(d), _row(tm, d)],
                      (_sds((s, d), F32), _sds((s, d), BF16), _sds((s, d), BF16), _sds((1, d), F32),
                       _sds((8, 128), F32)),
                      (_row(tm, d), _row(tm, d), _row(tm, d), _vec(d), pl.BlockSpec((8, 128), lambda i: (0, 0))),
                      s // tm, True)


def _concat_cols(name, parts, tm=256):
    s = parts[0].shape[0]
    widths = [a.shape[1] for a in parts]

    def body(*refs):
        o_ref, offset = refs[-1], 0
        for ref, width in zip(refs[:-1], widths):
            o_ref[:, offset:offset + width] = ref[...].astype(BF16)
            offset += width

    return _rows_call(name, body, parts, [_row(tm, width) for width in widths], _sds((s, sum(widths)), BF16),
                      _row(tm, sum(widths)), s // tm, False)


def _bwd_post(dx, y, gain, tm=256):
    s, d = dx.shape

    def body(dx_ref, y_ref, g_ref, dy_ref, dg_ref):
        dy, dgain = _rms_bwd(dx_ref[...], g_ref[...], y_ref[...])
        dy_ref[...] = dy.astype(BF16)
        _accumulate(dg_ref, _colsum(dgain))

    return _rows_call("bwd_post", body, (dx, y, gain), [_row(tm, d), _row(tm, d), _vec(d)],
                      (_sds((s, d), BF16), _sds((1, d), F32)), (_row(tm, d), _vec(d)), s // tm, True)


def _bwd_mid(dx3, dh2, x2, gain_pre, mo, gain_post, tm=128):
    s, d = dx3.shape

    def body(dx3_ref, dh_ref, x2_ref, gn_ref, mo_ref, gp_ref, dx2_ref, dmo_ref, dgn_ref, dgp_ref):
        dv, dgn = _rms_bwd(dh_ref[...], gn_ref[...], x2_ref[...])
        dx2 = dx3_ref[...] + dv
        dx2_ref[...] = dx2
        dmo, dgp = _rms_bwd(dx2, gp_ref[...], mo_ref[...])
        dmo_ref[...] = dmo.astype(BF16)
        _accumulate(dgn_ref, _colsum(dgn))
        _accumulate(dgp_ref, _colsum(dgp))

    return _rows_call("bwd_mid", body, (dx3, dh2, x2, gain_pre, mo, gain_post),
                      [_row(tm, d), _row(tm, d), _row(tm, d), _vec(d), _row(tm, d), _vec(d)],
                      (_sds((s, d), F32), _sds((s, d), BF16), _sds((1, d), F32), _sds((1, d), F32)),
                      (_row(tm, d), _row(tm, d), _vec(d), _vec(d)), s // tm, True)


def _bwd_first(dx2, dh1, x, gain, tm=256):
    s, d = dx2.shape

    def body(dx2_ref, dh_ref, x_ref, g_ref, dx_ref, dg_ref):
        dv, dgain = _rms_bwd(dh_ref[...], g_ref[...], x_ref[...])
        dx_ref[...] = dx2_ref[...] + dv
        _accumulate(dg_ref, _colsum(dgain))

    return _rows_call("bwd_first", body, (dx2, dh1, x, gain), [_row(tm, d), _row(tm, d), _row(tm, d), _vec(d)],
                      (_sds((s, d), F32), _sds((1, d), F32)), (_row(tm, d), _vec(d)), s // tm, True)


def _bwd_mix(dmixed, ga, gp, ya, yp, tm=256):
    s, d = dmixed.shape

    def body(dm_ref, ga_ref, gp_ref, ya_ref, yp_ref, dya_ref, dyp_ref, dga_ref, dgp_ref):
        dm = dm_ref[...]
        ga, gp, ya, yp = (ref[...].astype(F32) for ref in (ga_ref, gp_ref, ya_ref, yp_ref))
        sa, sp = _sigmoid(ga), _sigmoid(gp)
        dya_ref[...] = (dm * sa).astype(BF16)
        dyp_ref[...] = (dm * sp).astype(BF16)
        dga_ref[...] = (dm * ya * sa * (1.0 - sa)).astype(BF16)
        dgp_ref[...] = (dm * yp * sp * (1.0 - sp)).astype(BF16)

    return _rows_call("bwd_mix", body, (dmixed, ga, gp, ya, yp), [_row(tm, d)] * 5,
                      (_sds((s, d), BF16),) * 4, (_row(tm, d),) * 4, s // tm, False)


def _bwd_pool_scale(dps, pg, scale, tm=256):
    s, d = dps.shape

    def body(d_ref, pg_ref, s_ref, dpg_ref, ds_ref):
        dv = d_ref[...]
        dpg_ref[...] = (dv * s_ref[...]).astype(BF16)
        _accumulate(ds_ref, _colsum(dv * pg_ref[...]))

    return _rows_call("bwd_pool_scale", body, (dps, pg, scale), [_row(tm, d), _row(tm, d), _vec(d)],
                      (_sds((s, d), BF16), _sds((1, d), F32)), (_row(tm, d), _vec(d)), s // tm, True)


def _shift_down(v, k, rows):
    return jnp.where(rows >= k, pltpu.roll(v, k, 0), 0.0)


def _shift_up(v, k, rows):
    s = v.shape[0]
    return jnp.where(rows < s - k, pltpu.roll(v, s - k, 0), 0.0)


def _window_pick(group, sums, rows):
    total = sums[-1]
    width = jnp.full((), POOL_WINDOWS[-1], jnp.int32)
    for g in range(len(POOL_WINDOWS) - 2, -1, -1):
        total = jnp.where(group == g, sums[g], total)
        width = jnp.where(group == g, POOL_WINDOWS[g], width)
    return total, jnp.minimum(rows + 1, width).astype(F32)


def _doubling(v, shift, rows):
    sums, k = [], 1
    for _ in POOL_WINDOWS:
        v = v + shift(v, k, rows)
        sums.append(v)
        k *= 2
    return sums


def _pool_fwd(u):
    s, width = u.shape
    per_group = width // len(POOL_WINDOWS) // POOL_TILE

    def body(u_ref, o_ref):
        v = u_ref[...]
        rows = lax.broadcasted_iota(jnp.int32, (s, 1), 0)
        total, count = _window_pick(pl.program_id(0), _doubling(v, _shift_down, rows), rows)
        o_ref[...] = (total / count - v).astype(BF16)

    spec = pl.BlockSpec((s, POOL_TILE), lambda g, j: (0, g * per_group + j))
    return pl.pallas_call(body, name="pool_fwd", grid=(len(POOL_WINDOWS), per_group), in_specs=[spec], out_specs=spec,
                          out_shape=_sds((s, width), BF16), compiler_params=_params("parallel", "parallel"))(u)


def _pool_bwd(dpooled):
    s, width = dpooled.shape
    per_group = width // len(POOL_WINDOWS) // POOL_TILE

    def body(d_ref, o_ref):
        dv = d_ref[...]
        rows = lax.broadcasted_iota(jnp.int32, (s, 1), 0)
        group = pl.program_id(0)
        _, count = _window_pick(group, [dv] * len(POOL_WINDOWS), rows)
        total, _ = _window_pick(group, _doubling(dv / count, _shift_up, rows), rows)
        o_ref[...] = (total - dv).astype(BF16)

    spec = pl.BlockSpec((s, POOL_TILE), lambda g, j: (0, g * per_group + j))
    return pl.pallas_call(body, name="pool_bwd", grid=(len(POOL_WINDOWS), per_group), in_specs=[spec], out_specs=spec,
                          out_shape=_sds((s, width), BF16), compiler_params=_params("parallel", "parallel"))(dpooled)


def _conv(v, w_ref, b_ref, rows):
    out = b_ref[...] + _shift_down(v, 2, rows) * w_ref[0:1, :]
    out = out + _shift_down(v, 1, rows) * w_ref[1:2, :]
    return out + v * w_ref[2:3, :]


def _gelu_parts(v):
    th = jnp.tanh(GELU_C * (v + GELU_A * v * v * v))
    return th, 0.5 * v * (1.0 + th)


def _conv_gelu_fwd(up, conv_w, conv_b, tc=128):
    s, f2 = up.shape
    f = f2 // 2
    nb = f // tc

    def body(g_ref, v_ref, wg_ref, wv_ref, bg_ref, bv_ref, o_ref):
        rows = lax.broadcasted_iota(jnp.int32, (s, 1), 0)
        _, gl = _gelu_parts(_conv(g_ref[...], wg_ref, bg_ref, rows))
        o_ref[...] = (gl * _conv(v_ref[...], wv_ref, bv_ref, rows)).astype(BF16)

    lo = lambda r: pl.BlockSpec((r, tc), lambda j: (0, j))
    hi = lambda r: pl.BlockSpec((r, tc), lambda j: (0, j + nb))
    return pl.pallas_call(
        body, name="conv_gelu_fwd", grid=(nb,), in_specs=[lo(s), hi(s), lo(3), hi(3), lo(1), hi(1)], out_specs=lo(s),
        out_shape=_sds((s, f), BF16), compiler_params=_params("parallel"))(up, up, conv_w, conv_w, conv_b, conv_b)


def _conv_gelu_bwd(up, dact, conv_w, conv_b, tc=128):
    s, f2 = up.shape
    f = f2 // 2
    nb = f // tc

    def body(g_ref, v_ref, d_ref, wg_ref, wv_ref, bg_ref, bv_ref, dg_ref, dv_ref, dwg_ref, dwv_ref, dbg_ref, dbv_ref):
        rows = lax.broadcasted_iota(jnp.int32, (s, 1), 0)
        dact = d_ref[...].astype(F32)

        def back(pre, dc, w_ref, dpre_ref, dw_ref, db_ref):
            dc1, dc2 = _shift_up(dc, 1, rows), _shift_up(dc, 2, rows)
            db_ref[...] = _colsum(dc)
            dw_ref[0:1, :] = _colsum(dc2 * pre)
            dw_ref[1:2, :] = _colsum(dc1 * pre)
            dw_ref[2:3, :] = _colsum(dc * pre)
            dpre_ref[...] = (dc * w_ref[2:3, :] + dc1 * w_ref[1:2, :] + dc2 * w_ref[0:1, :]).astype(BF16)

        gate, val = g_ref[...], v_ref[...]
        cg = _conv(gate, wg_ref, bg_ref, rows)
        cv = _conv(val, wv_ref, bv_ref, rows)
        th, gl = _gelu_parts(cg)
        dgl = 0.5 * (1.0 + th) + 0.5 * cg * (1.0 - th * th) * GELU_C * (1.0 + 3.0 * GELU_A * cg * cg)
        back(val, dact * gl, wv_ref, dv_ref, dwv_ref, dbv_ref)
        back(gate, dact * cv * dgl, wg_ref, dg_ref, dwg_ref, dbg_ref)

    lo = lambda r: pl.BlockSpec((r, tc), lambda j: (0, j))
    hi = lambda r: pl.BlockSpec((r, tc), lambda j: (0, j + nb))
    return pl.pallas_call(
        body, name="conv_gelu_bwd", grid=(nb,),
        in_specs=[lo(s), hi(s), lo(s), lo(3), hi(3), lo(1), hi(1)],
        out_specs=(lo(s), lo(s), lo(3), lo(3), lo(1), lo(1)),
        out_shape=(_sds((s, f), BF16), _sds((s, f), BF16), _sds((3, f), F32), _sds((3, f), F32),
                   _sds((1, f), F32), _sds((1, f), F32)),
        compiler_params=_params("parallel"))(up, up, dact, conv_w, conv_w, conv_b, conv_b)


def _dot(a, b, dims):
    return lax.dot_general(a, b, (dims, ((), ())), preferred_element_type=F32)


def _running_sums(v, tri):
    hi = v.astype(BF16)
    lo = (v - hi.astype(F32)).astype(BF16)
    return _dot(hi, tri, ((1,), (0,))) + _dot(lo, tri, ((1,), (0,)))


def _attn_scores(q, kt, mask):
    z = _dot(q, kt, ((1,), (1,))) * (HEAD_DIM ** -0.5)
    e = jnp.exp(-jnp.abs(z))
    log_1m_beta = -(jnp.maximum(z, 0.0) + jnp.log(1.0 + e))
    return z, e, log_1m_beta if mask is None else jnp.where(mask, log_1m_beta, 0.0)


def _masked(v, mask):
    return v if mask is None else jnp.where(mask, v, 0.0)


def _attn_consts(t):
    rows = lax.broadcasted_iota(jnp.int32, (t, t), 0)
    cols = lax.broadcasted_iota(jnp.int32, (t, t), 1)
    return (rows >= cols).astype(BF16), (rows <= cols).astype(BF16)


def _causal(rows, t):
    return lax.broadcasted_iota(jnp.int32, (rows, t), 1) < lax.broadcasted_iota(jnp.int32, (rows, t), 0)


def _attn_specs(s, tq, heads):
    lanes = ATTN_HEADS * HEAD_DIM
    steps = heads // ATTN_HEADS
    q = pl.BlockSpec((tq, lanes), lambda h, i: (i, h))
    k = pl.BlockSpec((s, lanes), lambda h, i: (0, steps + h))
    whole = pl.BlockSpec((s, lanes), lambda h, i: (0, h))
    rows = pl.BlockSpec((ATTN_HEADS, tq, 1), lambda h, i: (h, i, 0))
    return q, k, whole, rows


def _attn_blocks(s):
    t = min(ATTN_BLOCK, s)
    tq = min(ATTN_Q_BLOCK, s)
    assert tq % t == 0 and s % tq == 0
    return t, tq, tq // t


def _head(p):
    return pl.ds(p * HEAD_DIM, HEAD_DIM)


def _attn_fwd(qk, v, ride=None):
    s, width = v.shape
    heads = width // HEAD_DIM
    t, tq, ratio = _attn_blocks(s)
    assert heads % ATTN_HEADS == 0
    grid = (heads // ATTN_HEADS, s // tq)
    extra, extra_specs, extra_out, aliases, scratch, semantics, hooks = _ride_along(
        ride, grid, 3, 2, ("parallel", "parallel"))

    def body(*refs):
        q_ref, k_ref, v_ref = refs[:3]
        o_ref, lsum_ref = refs[3 + len(extra):5 + len(extra)]
        before, after = hooks(refs)
        before()
        i = pl.program_id(1)
        tri, _ = _attn_consts(t)

        def block(p, rows, start, carry, mask):
            acc, log_rest = carry
            z, _, log_1m_beta = _attn_scores(q_ref[rows, _head(p)], k_ref[pl.ds(start, t), _head(p)], mask)
            total = _running_sums(log_1m_beta, tri) + log_rest
            a = _masked(jnp.exp(z + total), mask)
            return acc + _dot(a.astype(BF16), v_ref[pl.ds(start, t), _head(p)], ((1,), (0,))), total[:, 0:1]

        def own_rows(p):
            carry = (jnp.zeros((tq, HEAD_DIM), F32), jnp.zeros((tq, 1), F32))
            for r in range(ratio - 1, -1, -1):
                below = tuple(val[r * t:] for val in carry)
                below = block(p, pl.ds(r * t, tq - r * t), pl.multiple_of(i * tq + r * t, t), below,
                              _causal(tq - r * t, t))
                carry = tuple(jnp.concatenate([val[:r * t], new], axis=0) if r else new
                              for val, new in zip(carry, below))
            return carry

        def step(n, carry):
            start = pl.multiple_of((i * ratio - 1 - n) * t, t)
            return tuple(block(p, pl.ds(0, tq), start, carry[p], None) for p in range(ATTN_HEADS))

        carry = lax.fori_loop(0, i * ratio, step, tuple(own_rows(p) for p in range(ATTN_HEADS)))
        for p in range(ATTN_HEADS):
            o_ref[:, _head(p)] = carry[p][0].astype(BF16)
            lsum_ref[p] = carry[p][1]
        after()

    q_spec, k_spec, whole, rows = _attn_specs(s, tq, heads)
    outs = pl.pallas_call(
        body, name="attn_fwd", grid=grid, in_specs=[q_spec, k_spec, whole] + extra_specs,
        out_specs=(q_spec, rows, *extra_specs),
        out_shape=(_sds((s, width), BF16), _sds((heads, s, 1), F32), *extra_out),
        input_output_aliases=aliases, scratch_shapes=scratch,
        compiler_params=_params(*semantics))(qk, qk, v, *extra)
    return outs[0], outs[1], list(outs[2:])


def _attn_bwd(qk, v, lsum, do, ride=None):
    s, width = v.shape
    heads = width // HEAD_DIM
    t, tq, ratio = _attn_blocks(s)
    grid = (heads // ATTN_HEADS, s // tq)
    extra, extra_specs, extra_out, aliases, scratch, semantics, hooks = _ride_along(
        ride, grid, 5, 3, ("parallel", "arbitrary"))

    def body(*refs):
        q_ref, k_ref, v_ref, lsum_ref, do_ref = refs[:5]
        dq_ref, dk_ref, dv_ref = refs[5 + len(extra):8 + len(extra)]
        before, after = hooks(refs)
        before()
        i = pl.program_id(1)

        @pl.when(i == 0)
        def _():
            dk_ref[...] = jnp.zeros_like(dk_ref)
            dv_ref[...] = jnp.zeros_like(dv_ref)

        _, triu = _attn_consts(t)

        def block(p, rows, start, carry, mask):
            dq, log_rest, g_before = carry
            q, do_blk = q_ref[rows, _head(p)], do_ref[rows, _head(p)]
            kt, vt = k_ref[pl.ds(start, t), _head(p)], v_ref[pl.ds(start, t), _head(p)]
            z, e, log_1m_beta = _attn_scores(q, kt, mask)
            upto = _running_sums(log_1m_beta, triu)
            a = _masked(jnp.exp(z + log_rest - upto + log_1m_beta), mask)
            g = a * _dot(do_blk, vt, ((1,), (1,)))
            g_upto = _running_sums(g, triu) + g_before
            sig = jnp.where(z >= 0.0, 1.0, e) / (1.0 + e)
            dz = (_masked(g - sig * g_upto, mask) * (HEAD_DIM ** -0.5)).astype(BF16)
            dk_ref[pl.ds(start, t), _head(p)] += _dot(dz, q, ((0,), (0,)))
            dv_ref[pl.ds(start, t), _head(p)] += _dot(a.astype(BF16), do_blk, ((0,), (0,)))
            return dq + _dot(dz, kt, ((1,), (0,))), log_rest - upto[:, t - 1:t], g_upto[:, t - 1:t]

        def step(n, carry):
            start = pl.multiple_of(n * t, t)
            return tuple(block(p, pl.ds(0, tq), start, carry[p], None) for p in range(ATTN_HEADS))

        carry = tuple((jnp.zeros((tq, HEAD_DIM), F32), lsum_ref[p], jnp.zeros((tq, 1), F32))
                      for p in range(ATTN_HEADS))
        carry = lax.fori_loop(0, i * ratio, step, carry)
        for p in range(ATTN_HEADS):
            state = carry[p]
            for r in range(ratio):
                below = tuple(val[r * t:] for val in state)
                below = block(p, pl.ds(r * t, tq - r * t), pl.multiple_of(i * tq + r * t, t), below,
                              _causal(tq - r * t, t))
                state = tuple(jnp.concatenate([val[:r * t], new], axis=0) if r else new
                              for val, new in zip(state, below))
            dq_ref[:, _head(p)] = state[0].astype(BF16)
        after()

    q_spec, k_spec, whole, rows = _attn_specs(s, tq, heads)
    outs = pl.pallas_call(
        body, name="attn_bwd", grid=grid, in_specs=[q_spec, k_spec, whole, rows, q_spec] + extra_specs,
        out_specs=(q_spec, whole, whole, *extra_specs),
        out_shape=(_sds((s, width), BF16), _sds((s, width), F32), _sds((s, width), F32), *extra_out),
        input_output_aliases=aliases, scratch_shapes=scratch,
        compiler_params=_params(*semantics))(qk, qk, v, lsum, do, *extra)
    return outs[0], outs[1], outs[2], list(outs[3:])


def _as_rows(a):
    return a.reshape(-1, a.shape[-1])


def _flat_call(name, body, ins, out_dtypes, block_bytes=1 << 20):
    shape = ins[0].shape
    rows, cols = _as_rows(ins[0]).shape
    tr = _tile(rows, max(8, block_bytes // (4 * cols)), 8)
    spec = pl.BlockSpec((tr, cols), lambda i: (i, 0))
    outs = pl.pallas_call(
        body, name=name, grid=(rows // tr,), in_specs=[spec] * len(ins), out_specs=tuple([spec] * len(out_dtypes)),
        out_shape=tuple(_sds((rows, cols), dt) for dt in out_dtypes), compiler_params=_params("parallel"),
    )(*[_as_rows(a) for a in ins])
    return [o.reshape(shape) for o in outs]


def _scalars(*values):
    return jnp.stack([jnp.asarray(v, jnp.int32) for v in values])


def _row_tile(rows, cols):
    return _tile(rows, max(ROWS_PER_TILE, (1 << 22) // (4 * cols)), ROWS_PER_TILE)


def _place_cast(w, chip):
    cols = w.shape[-1]
    rows = w.size // cols
    tr = _row_tile(rows, cols)

    def body(s_ref, w_ref, o_ref):
        o_ref[...] = w_ref[...].astype(BF16)

    out = pl.pallas_call(
        body, name="place_cast",
        grid_spec=pltpu.PrefetchScalarGridSpec(
            num_scalar_prefetch=1, grid=(rows // tr,),
            in_specs=[pl.BlockSpec((tr, cols), lambda i, s: (i, 0))],
            out_specs=pl.BlockSpec((None, tr, cols), lambda i, s: (s[0], i, 0))),
        out_shape=_sds((N_CHIPS, rows, cols), BF16), compiler_params=_params("parallel"),
    )(_scalars(chip), w.reshape(rows, cols))
    return out.reshape((N_CHIPS,) + w.shape)


def _add_halves(g, got, core):
    _, rows, cols = got.shape
    tr = _row_tile(rows, cols)

    def body(s_ref, a_ref, b_ref, o_ref):
        o_ref[...] = (a_ref[...].astype(F32) + b_ref[...].astype(F32)).astype(BF16)

    spec = pl.BlockSpec((None, tr, cols), lambda j, i, s: (j, i, 0))
    return pl.pallas_call(
        body, name="add_halves",
        grid_spec=pltpu.PrefetchScalarGridSpec(
            num_scalar_prefetch=1, grid=(N_CHIPS, rows // tr),
            in_specs=[pl.BlockSpec((None, None, tr, cols), lambda j, i, s: (j, s[0], i, 0)), spec], out_specs=spec),
        out_shape=_sds(got.shape, BF16), compiler_params=_params("parallel", "parallel"),
    )(_scalars(core), g, got)


def _sum_parts(own, others, chip, core):
    _, rows, cols = own.shape
    tr = _row_tile(rows, cols)

    def body(s_ref, a_ref, t_ref, o_ref):
        total = a_ref[...].astype(F32)
        for j in range(N_CHIPS - 1):
            total = total + t_ref[j].astype(F32)
        o_ref[...] = total

    return pl.pallas_call(
        body, name="sum_parts",
        grid_spec=pltpu.PrefetchScalarGridSpec(
            num_scalar_prefetch=1, grid=(rows // tr,),
            in_specs=[pl.BlockSpec((None, tr, cols), lambda i, s: (s[0], i, 0)),
                      pl.BlockSpec((N_CHIPS - 1, tr, cols), lambda i, s: (0, i, 0))],
            out_specs=pl.BlockSpec((None, tr, cols), lambda i, s: (s[1], i, 0))),
        out_shape=_sds((2, rows, cols), F32), compiler_params=_params("parallel"),
    )(_scalars(chip, core), own, others)


def _sum_leading(a, name):
    n = a.shape[0]
    shape = a.shape[1:]
    cols = shape[-1]
    rows = a.size // (n * cols)
    tr = _tile(rows, max(8, (1 << 20) // (4 * cols)), 8)

    def body(a_ref, o_ref):
        total = a_ref[0].astype(F32)
        for j in range(1, n):
            total = total + a_ref[j].astype(F32)
        o_ref[...] = total

    out = pl.pallas_call(
        body, name=name, grid=(rows // tr,), in_specs=[pl.BlockSpec((n, tr, cols), lambda i: (0, i, 0))],
        out_specs=pl.BlockSpec((tr, cols), lambda i: (i, 0)), out_shape=_sds((rows, cols), F32),
        compiler_params=_params("parallel"))(a.reshape(n, rows, cols))
    return out.reshape(shape)


def _adamw(w, g, m, v):
    def body(w_ref, g_ref, m_ref, v_ref, go_ref, d_ref, nm_ref, nv_ref):
        gv = g_ref[...]
        go_ref[...] = gv
        nm = ADAM_B1 * m_ref[...] + (1.0 - ADAM_B1) * gv
        nv = ADAM_B2 * v_ref[...] + (1.0 - ADAM_B2) * jnp.square(gv)
        nm_ref[...] = nm
        nv_ref[...] = nv
        m_hat = nm / (1.0 - ADAM_B1 ** ADAM_STEP)
        v_hat = nv / (1.0 - ADAM_B2 ** ADAM_STEP)
        d_ref[...] = -ADAM_LR * (m_hat / (jnp.sqrt(v_hat) + ADAM_EPS) + ADAM_WD * w_ref[...])

    return _flat_call("adamw", body, (w, g, m, v), (F32, F32, F32, F32), block_bytes=1 << 21)


def _mesh_position():
    return lax.axis_index("x"), lax.axis_index("y"), lax.axis_index("c")


def _other_chips(x, y):
    return [(1 - x, y), (x, 1 - y), (1 - x, 1 - y)]


def _chip_index(chip):
    return 2 * chip[0] + chip[1]


def _remote(src, dst, send_sem, recv_sem, device):
    return pltpu.make_async_remote_copy(src_ref=src, dst_ref=dst, send_sem=send_sem, recv_sem=recv_sem,
                                        device_id=device, device_id_type=MESH)


def _pieces(view):
    shape = list(view.shape)
    counts = []
    for axis in range(len(shape) - 1):
        want = -(-(view.dtype.itemsize * functools.reduce(lambda a, b: a * b, shape)) // DMA_PIECE_BYTES)
        unit = 1 if axis < len(shape) - 2 else ROWS_PER_TILE
        n, k = shape[axis], 1
        for cand in range(1, n + 1):
            if n % cand == 0 and (n // cand) % unit == 0:
                k = cand
                if cand >= want:
                    break
        counts.append(k)
        shape[axis] = n // k
    grid = [()]
    for axis, k in enumerate(counts):
        grid = [idx + (pl.ds(i * shape[axis], shape[axis]),) for idx in grid for i in range(k)]
    return grid


def _start_remote(src, dst, send_sem, recv_sem, device):
    for idx in _pieces(src):
        _remote(src.at[idx], dst.at[idx], send_sem, recv_sem, device).start()
    return _remote(src, dst, send_sem, recv_sem, device)


class _Ride:
    def __init__(self, operands, out_shapes, aliased, scratch_shapes, first, mid, last):
        self.operands, self.out_shapes, self.aliased = list(operands), list(out_shapes), aliased
        self.scratch_shapes, self.first, self.mid, self.last = list(scratch_shapes), first, mid, last

    def aliases(self, first_in, first_out):
        return {first_in + w: first_out + w for w in range(len(self.operands))} if self.aliased else {}


def _run_ride(name, ride):
    n = len(ride.operands)

    def body(*refs):
        ins, outs, sems = refs[:n], refs[n:2 * n], refs[2 * n:]
        ride.first(ins, outs, sems)
        if ride.mid is not None:
            ride.mid(ins, outs, sems)
        ride.last(ins, outs, sems)

    return pl.pallas_call(
        body, name=name, in_specs=[ANY] * n, out_specs=tuple([ANY] * n), out_shape=tuple(ride.out_shapes),
        input_output_aliases=ride.aliases(0, 0), scratch_shapes=ride.scratch_shapes,
    )(*ride.operands)


def _gather_ride(blocks):
    n = len(blocks)
    halves = [a.shape[1] // 2 for a in blocks]
    assert all(a.shape[1] % 2 == 0 for a in blocks)
    pairs = [(w, j) for w in range(n) for j in range(3)]

    def plan(outs, sems):
        ici_send, ici_recv, d2d_send, d2d_recv = sems
        x, y, c = _mesh_position()
        chips = _other_chips(x, y)

        def half(w, chip, which):
            return outs[w].at[chip, pl.ds(which * halves[w], halves[w])]

        def over_ici(w, j, chip):
            there = half(w, chip, c)
            return there, there, ici_send.at[3 * w + j], ici_recv.at[3 * w + j], (*chips[j], c)

        def over_d2d(w, j, which):
            there = half(w, _chip_index(chips[j]), which)
            return there, there, d2d_send.at[3 * w + j], d2d_recv.at[3 * w + j], (x, y, 1 - c)

        return _chip_index((x, y)), c, chips, over_ici, over_d2d

    def first(ins, outs, sems):
        me, _, _, over_ici, _ = plan(outs, sems)
        for w, j in pairs:
            _start_remote(*over_ici(w, j, me))

    def mid(ins, outs, sems):
        _, c, chips, over_ici, over_d2d = plan(outs, sems)
        for w, j in pairs:
            _remote(*over_ici(w, j, _chip_index(chips[j]))).wait_recv()
            _start_remote(*over_d2d(w, j, c))

    def last(ins, outs, sems):
        me, c, _, over_ici, over_d2d = plan(outs, sems)
        for w, j in pairs:
            _remote(*over_d2d(w, j, 1 - c)).wait_recv()
        for w, j in pairs:
            _remote(*over_ici(w, j, me)).wait_send()
            _remote(*over_d2d(w, j, c)).wait_send()

    return _Ride(blocks, [_sds(a.shape, a.dtype) for a in blocks], True,
                 [pltpu.SemaphoreType.DMA((3 * n,))] * 4, first, mid, last)


def _swap_halves(grads):
    n = len(grads)

    def body(*refs):
        ins, got = refs[:n], refs[n:2 * n]
        send_sem, recv_sem = refs[2 * n:]
        x, y, c = _mesh_position()
        swaps = [_start_remote(ins[w].at[pl.ds(0, N_CHIPS), 1 - c], got[w], send_sem.at[w], recv_sem.at[w],
                               (x, y, 1 - c)) for w in range(n)]
        for cp in swaps:
            cp.wait()

    return pl.pallas_call(
        body, name="swap_halves", in_specs=[ANY] * n, out_specs=tuple([ANY] * n),
        out_shape=tuple(_sds((N_CHIPS,) + a.shape[2:], a.dtype) for a in grads),
        scratch_shapes=[pltpu.SemaphoreType.DMA((n,))] * 2,
    )(*grads)


def _scatter_ride(parts):
    n = len(parts)
    pairs = [(w, j) for w in range(n) for j in range(3)]

    def plan(ins, outs, sems):
        send_sem, recv_sem = sems
        x, y, c = _mesh_position()
        chips = _other_chips(x, y)

        def to_chip(w, j):
            return (ins[w].at[_chip_index(chips[j])], outs[w].at[j], send_sem.at[3 * w + j], recv_sem.at[3 * w + j],
                    (*chips[j], c))

        return to_chip

    def first(ins, outs, sems):
        to_chip = plan(ins, outs, sems)
        for w, j in pairs:
            _start_remote(*to_chip(w, j))

    def last(ins, outs, sems):
        to_chip = plan(ins, outs, sems)
        for w, j in pairs:
            _remote(*to_chip(w, j)).wait_recv()
        for w, j in pairs:
            _remote(*to_chip(w, j)).wait_send()

    return _Ride(parts, [_sds((N_CHIPS - 1,) + a.shape[1:], a.dtype) for a in parts], False,
                 [pltpu.SemaphoreType.DMA((3 * n,))] * 2, first, None, last)


def _join_halves(halves):
    n = len(halves)

    def body(*refs):
        outs = refs[n:2 * n]
        send_sem, recv_sem = refs[2 * n:]
        x, y, c = _mesh_position()
        sibling = (x, y, 1 - c)
        sends = [_start_remote(outs[w].at[c], outs[w].at[c], send_sem.at[w], recv_sem.at[w], sibling)
                 for w in range(n)]
        for w in range(n):
            there = outs[w].at[1 - c]
            _remote(there, there, send_sem.at[w], recv_sem.at[w], sibling).wait_recv()
            sends[w].wait_send()

    return pl.pallas_call(
        body, name="join_halves", in_specs=[ANY] * n, out_specs=tuple([ANY] * n),
        out_shape=tuple(_sds(a.shape, a.dtype) for a in halves),
        input_output_aliases={w: w for w in range(n)},
        scratch_shapes=[pltpu.SemaphoreType.DMA((n,))] * 2,
    )(*halves)


def _gather_small(vec):
    length = vec.shape[1]
    flips = [(fx, fy, fc) for fx in (0, 1) for fy in (0, 1) for fc in (0, 1)][1:]

    def body(v_ref, o_ref, send_sem, recv_sem, local_sem):
        x, y, c = _mesh_position()
        me = 4 * x + 2 * y + c
        local = pltpu.make_async_copy(v_ref, o_ref.at[me], local_sem)
        local.start()
        peers = [(x ^ fx, y ^ fy, c ^ fc) for fx, fy, fc in flips]
        sends = [_remote(v_ref, o_ref.at[me], send_sem.at[k], recv_sem.at[k], peer) for k, peer in enumerate(peers)]
        for cp in sends:
            cp.start()
        for k, (px, py, pc) in enumerate(peers):
            there = o_ref.at[4 * px + 2 * py + pc]
            _remote(there, there, send_sem.at[k], recv_sem.at[k], peers[k]).wait_recv()
        for cp in sends:
            cp.wait_send()
        local.wait()

    return pl.pallas_call(
        body, name="gather_small", in_specs=[ANY], out_specs=ANY, out_shape=_sds((N_DEVICES, 1, length), F32),
        scratch_shapes=[pltpu.SemaphoreType.DMA((N_DEVICES - 1,))] * 2 + [pltpu.SemaphoreType.DMA],
    )(vec)


MATRICES = ("w_in", "w_attn_branch", "w_pool_group", "w_pool_branch", "w_out", "w_up", "w_down", "w_ple", "w_ple_gate")
VECTORS = ("norm_mix_pre", "pool_scale", "norm_mix_post", "norm_ffn_pre", "conv_b", "norm_ffn_post", "norm_ple_post")
WEIGHTS = ("norm_mix_pre", "w_in", "w_attn_branch", "w_pool_group", "pool_scale", "w_pool_branch", "w_out",
           "norm_mix_post", "norm_ffn_pre", "w_up", "conv_w", "conv_b", "w_down", "norm_ffn_post", "w_ple",
           "w_ple_gate", "norm_ple_post")


EARLY = ("w_in",)
LATE = tuple(k for k in MATRICES if k not in EARLY)
RIDES = (("w_ple_gate",), ("w_ple",), ("w_attn_branch", "w_pool_branch"), ("w_out", "w_pool_group"),
         ("w_up",), ("w_down",))
assert sorted(k for names in RIDES for k in names) == sorted(LATE)


def _halves_view(g):
    return g.reshape(N_CHIPS, 2, g.size // (2 * N_CHIPS * g.shape[-1]), g.shape[-1])


def _pair_sums(gmat, names, core):
    flat = [_halves_view(gmat[k]) for k in names]
    return [_add_halves(a, b, core) for a, b in zip(flat, _swap_halves(flat))]


def _local_step(x, p, target, w_in, late_blocks, vec, conv_w, core):
    s, d = x.shape
    groups = len(POOL_WINDOWS)
    aw = w_in.shape[2] // 2
    assert w_in.shape[2] == d

    h1 = _rms_fwd(x, vec["norm_mix_pre"])
    full = {}

    def carrying(names):
        return _gather_ride([late_blocks[k] for k in names])

    def carried(names, outs):
        full.update(zip(names, outs))

    qk, got = _mm_nn("proj_qk", h1, w_in, BF16, j0=0, nj=1, ride=carrying(RIDES[0]))
    carried(RIDES[0], got)
    v = _mm_nn("proj_v", h1, w_in, BF16, j0=1, c0=0, cn=aw)
    u, got = _mm_nn("proj_u", h1, w_in, F32, j0=1, c0=aw, cn=aw, ride=carrying(RIDES[1]))
    carried(RIDES[1], got)
    ga, got = _mm_nn("proj_ga", h1, w_in, BF16, j0=2, nj=1, ride=carrying(RIDES[2]))
    carried(RIDES[2], got)
    gp, got = _mm_nn("proj_gp", h1, w_in, BF16, j0=3, nj=1, ride=carrying(RIDES[3]))
    carried(RIDES[3], got)
    attn, lsum, got = _attn_fwd(qk, v, carrying(RIDES[4]))
    carried(RIDES[4], got)
    w_ab, w_pb, w_up, w_ple = (full[k] for k in ("w_attn_branch", "w_pool_branch", "w_up", "w_ple"))
    w_pg = full["w_pool_group"].transpose(1, 0, 2, 3)
    w_pg = w_pg.reshape(groups, -1, w_pg.shape[-1])
    as_rows = lambda k: full[k].reshape(1, -1, full[k].shape[-1])
    w_out, w_pleg = as_rows("w_out"), as_rows("w_ple_gate")
    assert w_pb.shape[1] == aw == w_ab.shape[1]
    ya = _mm_nn("attn_branch", attn, w_ab, BF16)
    pooled = _pool_fwd(u)
    pg = _group_nn("pool_group", pooled, w_pg, F32)
    ps = _scale_cols(pg, vec["pool_scale"])
    yp = _mm_nn("pool_branch", ps, w_pb, BF16)
    mixed = _mix_fwd(ga, gp, ya, yp)
    mo = _mm_nn("mix_out", mixed, w_out, F32)
    x2, h2 = _post_pre(x, mo, vec["norm_mix_post"], vec["norm_ffn_pre"])
    up, got = _mm_nn("ffn_up", h2, w_up, F32, tn=1408, ride=carrying(RIDES[5]))
    carried(RIDES[5], got)
    w_down = as_rows("w_down")
    act = _conv_gelu_fwd(up, conv_w, vec["conv_b"])
    yf = _mm_nn("ffn_down", act, w_down, F32, tk=2816)
    x3 = _post(x2, yf, vec["norm_ffn_post"])
    e = _mm_nn("ple_embed", p, w_ple, F32)
    t = _mm_nn("ple_gate", x3, w_pleg, F32)
    dx4, dt, de, g_ple_post, loss_rows = _final(x3, t, e, vec["norm_ple_post"], target)

    gvec, gmat = {}, {}
    gvec["norm_ple_post"] = g_ple_post
    gmat["w_ple"] = _mm_tn("d_w_ple", p, de, N_CHIPS, BF16)
    gmat["w_ple_gate"] = _mm_tn("d_w_ple_gate", x3, dt, 1, BF16)
    dx3 = _mm_nt("d_x3", dt, w_pleg, F32, add=dx4)
    dyf, gvec["norm_ffn_post"] = _bwd_post(dx3, yf, vec["norm_ffn_post"])
    gmat["w_down"] = _mm_tn("d_w_down", act, dyf, 1, BF16)
    dact = _mm_nt("d_act", dyf, w_down, BF16, tn=1408)
    dup_g, dup_v, dcw_g, dcw_v, dcb_g, dcb_v = _conv_gelu_bwd(up, dact, conv_w, vec["conv_b"])
    dup = _concat_cols("concat_dup", (dup_g, dup_v))
    gconv_w = jnp.concatenate([dcw_g, dcw_v], axis=1)
    gvec["conv_b"] = jnp.concatenate([dcb_g, dcb_v], axis=1)
    gmat["w_up"] = _mm_tn("d_w_up", h2, dup, N_CHIPS, BF16, tn=1408)
    dh2 = _mm_nt("d_h2", dup, w_up, F32, tk=2816)
    dx2, dmo, gvec["norm_ffn_pre"], gvec["norm_mix_post"] = _bwd_mid(
        dx3, dh2, x2, vec["norm_ffn_pre"], mo, vec["norm_mix_post"])
    gmat["w_out"] = _mm_tn("d_w_out", mixed, dmo, 1, BF16)
    dmixed = _mm_nt("d_mixed", dmo, w_out, F32)
    dya, dyp, dga, dgp = _bwd_mix(dmixed, ga, gp, ya, yp)
    gmat["w_attn_branch"] = _mm_tn("d_w_attn_branch", attn, dya, N_CHIPS, BF16)
    dattn = _mm_nt("d_attn", dya, w_ab, BF16)
    gmat["w_pool_branch"] = _mm_tn("d_w_pool_branch", ps, dyp, N_CHIPS, BF16)
    dps = _mm_nt("d_ps", dyp, w_pb, F32)
    dpg, gvec["pool_scale"] = _bwd_pool_scale(dps, pg, vec["pool_scale"])
    g_pg = _group_tn("d_w_pool_group", pooled, dpg, groups, BF16)
    gmat["w_pool_group"] = g_pg.reshape(groups, N_CHIPS, -1, g_pg.shape[-1]).transpose(1, 0, 2, 3)
    dpooled = _group_nn("d_pooled", dpg, w_pg, F32, transpose_w=True)
    du = _pool_bwd(dpooled)
    pair = dict(zip(LATE, _pair_sums(gmat, LATE, core)))
    dq, dk, dv, others = _attn_bwd(qk, v, lsum, dattn, _scatter_ride([pair[k] for k in LATE]))
    others = dict(zip(LATE, others))
    dproj = _concat_cols("concat_dproj", (dq, dk, dv, du, dga, dgp))
    gmat["w_in"] = _mm_tn("d_w_in", h1, dproj, N_CHIPS, BF16)
    early = _pair_sums(gmat, EARLY, core)
    pair.update(zip(EARLY, early))
    dh1, got = _mm_nt("d_h1", dproj, w_in, F32, ride=_scatter_ride(early))
    others.update(zip(EARLY, got))
    grad_x, gvec["norm_mix_pre"] = _bwd_first(dx2, dh1, x, vec["norm_mix_pre"])
    return loss_rows[0, 0], grad_x, pair, others, gvec, gconv_w


def _reduce_matrices(pair, others, shapes, chip, core):
    halves = [_sum_parts(pair[k], others[k], chip, core) for k in MATRICES]
    return {k: a.reshape(shapes[k]) for k, a in zip(MATRICES, _join_halves(halves))}


def kernel(x, p, norm_mix_pre, w_in, w_attn_branch, w_pool_group, pool_scale, w_pool_branch, w_out, norm_mix_post, norm_ffn_pre, w_up, conv_w, conv_b, w_down, norm_ffn_post, w_ple, w_ple_gate, norm_ple_post, loss_target, m_norm_mix_pre, m_w_in, m_w_attn_branch, m_w_pool_group, m_pool_scale, m_w_pool_branch, m_w_out, m_norm_mix_post, m_norm_ffn_pre, m_w_up, m_conv_w, m_conv_b, m_w_down, m_norm_ffn_post, m_w_ple, m_w_ple_gate, m_norm_ple_post, v_norm_mix_pre, v_w_in, v_w_attn_branch, v_w_pool_group, v_pool_scale, v_w_pool_branch, v_w_out, v_norm_mix_post, v_norm_ffn_pre, v_w_up, v_conv_w, v_conv_b, v_w_down, v_norm_ffn_post, v_w_ple, v_w_ple_gate, v_norm_ple_post):
    given = dict(locals())
    w = {k: given[k][0] for k in WEIGHTS}
    m = {k: given["m_" + k][0] for k in WEIGHTS}
    v = {k: given["v_" + k][0] for k in WEIGHTS}
    chip = 2 * lax.axis_index("x") + lax.axis_index("y")

    core = lax.axis_index("c")
    w_in_full = _run_ride("gather_weights", _gather_ride([_place_cast(w[k], chip) for k in EARLY]))[0]
    late_blocks = {k: _place_cast(w[k], chip) for k in LATE}
    vec = {k: w[k].reshape(1, -1) for k in VECTORS}
    taps = _gather_small(w["conv_w"].reshape(1, -1))[:, 0][0::2]
    f2q = w["conv_w"].shape[1]
    conv_w_full = taps.reshape(N_CHIPS, 3, f2q).transpose(1, 0, 2).reshape(3, N_CHIPS * f2q)

    loss_rows, grad_x, pair, others, gvec, gconv_w = _local_step(
        x[0], p[0, 0], loss_target[0], w_in_full, late_blocks, vec, conv_w_full, core)
    loss = lax.psum(loss_rows, ("x", "y", "c"))

    gw = _reduce_matrices(pair, others, {k: w[k].shape for k in MATRICES}, chip, core)
    sizes = [gvec[k].shape[1] for k in VECTORS]
    small = jnp.concatenate([gvec[k] for k in VECTORS] + [gconv_w.reshape(1, -1)], axis=1)
    small = _sum_leading(_gather_small(small), "sum_small")
    offset = 0
    for k, n in zip(VECTORS, sizes):
        gw[k] = small[0, offset:offset + n]
        offset += n
    gconv_w = small[0, offset:].reshape(3, N_CHIPS, f2q)
    gw["conv_w"] = lax.dynamic_index_in_dim(gconv_w, chip, axis=1, keepdims=False)

    delta, new_m, new_v = {}, {}, {}
    for k in MATRICES:
        gw[k], delta[k], new_m[k], new_v[k] = _adamw(w[k], gw[k], m[k], v[k])
    tiny = VECTORS + ("conv_w",)
    pack = lambda tree: jnp.concatenate([tree[k].reshape(1, -1) for k in tiny], axis=1)
    _, d_s, m_s, v_s = _adamw(pack(w), pack(gw), pack(m), pack(v))
    offset = 0
    for k in tiny:
        n = w[k].size
        delta[k], new_m[k], new_v[k] = (a[0, offset:offset + n].reshape(w[k].shape) for a in (d_s, m_s, v_s))
        offset += n

    lead = lambda tree: [tree[k].reshape((1,) + w[k].shape) for k in WEIGHTS]
    return (loss, grad_x[None], *lead(gw), *lead(delta), *lead(new_m), *lead(new_v))
```

```python
import functools

import jax
import jax.numpy as jnp
from jax import lax
from jax.experimental import pallas as pl
from jax.experimental.pallas import tpu as pltpu

F32 = jnp.float32
BF16 = jnp.bfloat16

HEAD_DIM = 128
POOL_WINDOWS = (2, 4, 8, 16)
EPS = 1e-6
GELU_C = 0.7978845608028654
GELU_A = 0.044715

ADAM_LR = 0.001
ADAM_B1 = 0.9
ADAM_B2 = 0.999
ADAM_EPS = 1e-08
ADAM_WD = 0.01
ADAM_STEP = 10

N_CHIPS = 4
N_DEVICES = 8
VMEM_LIMIT_BYTES = 52 * 1024 * 1024
ATTN_BLOCK = 256
ATTN_Q_BLOCK = 2048
ATTN_HEADS = 1
POOL_TILE = 128
RIDE_MID_STEPS = 2
DMA_PIECE_BYTES = 512 * 1024
ROWS_PER_TILE = 16
MESH = pl.DeviceIdType.MESH
ANY = pl.BlockSpec(memory_space=pl.ANY)


def _sds(shape, dtype):
    return jax.ShapeDtypeStruct(tuple(shape), dtype)


def _tile(dim, pref, mult=128):
    t = min(pref, dim) // mult * mult
    while t >= mult:
        if dim % t == 0:
            return t
        t -= mult
    return dim


def _params(*semantics):
    return pltpu.CompilerParams(dimension_semantics=semantics or None, vmem_limit_bytes=VMEM_LIMIT_BYTES)


def _sigmoid(v):
    return 1.0 / (1.0 + jnp.exp(-v))


def _rstd(v):
    return lax.rsqrt(jnp.mean(v * v, axis=-1, keepdims=True) + EPS)


def _rms_bwd(dy, gain, v):
    r = _rstd(v)
    vh = v * r
    gy = dy * gain
    return r * (gy - vh * jnp.mean(gy * vh, axis=-1, keepdims=True)), dy * vh


def _ride_along(ride, grid, n_in, n_out, own_semantics):
    if ride is None:
        return [], [], [], {}, [], own_semantics, lambda refs: (lambda: None, lambda: None)
    n = len(ride.operands)
    steps = [g - 1 for g in grid]

    def at(step):
        hit = pl.program_id(0) == step[0]
        for axis in range(1, len(grid)):
            hit = jnp.logical_and(hit, pl.program_id(axis) == step[axis])
        return hit

    def hooks(refs):
        ins, outs = refs[n_in:n_in + n], refs[n_in + n + n_out:n_in + 2 * n + n_out]
        sems = refs[len(refs) - len(ride.scratch_shapes):]

        def before():
            pl.when(at([0] * len(grid)))(lambda: ride.first(ins, outs, sems))
            if ride.mid is not None:
                pl.when(at(steps[:-1] + [max(steps[-1] - RIDE_MID_STEPS, 0)]))(lambda: ride.mid(ins, outs, sems))

        def after():
            pl.when(at(steps))(lambda: ride.last(ins, outs, sems))

        return before, after

    return (ride.operands, [ANY] * n, ride.out_shapes, ride.aliases(n_in, n_out), ride.scratch_shapes,
            ("arbitrary",) * len(grid), hooks)


def _matmul(name, a, b, grid, a_spec, b_spec, o_spec, out, dims, acc_shape, add=None, add_spec=None, ride=None):
    nk = grid[2]
    n_in = 2 if add is None else 3
    extra, extra_specs, extra_out, aliases, ride_scratch, semantics, hooks = _ride_along(
        ride, grid, n_in, 1, ("parallel", "parallel", "arbitrary"))

    def body(*refs):
        before, after = hooks(refs)
        before()
        compute(refs)
        after()

    def compute(refs):
        a_ref, b_ref = refs[:2]
        c_ref = None if add is None else refs[2]
        o_ref = refs[n_in + len(extra)]
        part = lax.dot_general(a_ref[...].astype(BF16), b_ref[...].astype(BF16), (dims, ((), ())),
                               preferred_element_type=F32)

        def finish(r):
            if c_ref is not None:
                r = r + c_ref[...]
            o_ref[...] = r.astype(o_ref.dtype)

        if nk == 1:
            finish(part)
            return
        acc = refs[n_in + 2 * len(extra) + 1]
        kk = pl.program_id(2)

        @pl.when(kk == 0)
        def _():
            acc[...] = part

        @pl.when(jnp.logical_and(kk > 0, kk < nk - 1))
        def _():
            acc[...] += part

        @pl.when(kk == nk - 1)
        def _():
            finish(acc[...] + part)

    operands = (a, b) if add is None else (a, b, add)
    in_specs = [a_spec, b_spec] if add is None else [a_spec, b_spec, add_spec]
    outs = pl.pallas_call(
        body, name=name, grid=grid, in_specs=in_specs + extra_specs, out_specs=(o_spec, *extra_specs),
        out_shape=(out, *extra_out), input_output_aliases=aliases,
        scratch_shapes=([] if nk == 1 else [pltpu.VMEM(acc_shape, F32)]) + ride_scratch,
        compiler_params=_params(*semantics),
    )(*operands, *extra)
    return outs[0] if ride is None else (outs[0], list(outs[1:]))


def _mm_nn(name, a, b3, out_dtype, j0=0, nj=None, c0=0, cn=None, tm=1024, tn=1024, tk=2048, ride=None):
    m, k = a.shape
    nj_all, kb, width = b3.shape
    assert kb == k
    if cn is None:
        cn = width
        nj = nj_all - j0 if nj is None else nj
    else:
        nj = 1
    n = nj * cn
    tm, tn, tk = _tile(m, tm, 8), _tile(cn, tn), _tile(k, tk)
    assert c0 % tn == 0
    nb, cb = cn // tn, c0 // tn
    return _matmul(
        name, a, b3, (m // tm, n // tn, k // tk),
        pl.BlockSpec((tm, tk), lambda i, j, kk: (i, kk)),
        pl.BlockSpec((None, tk, tn), lambda i, j, kk: (j0 + j // nb, kk, cb + j % nb)),
        pl.BlockSpec((tm, tn), lambda i, j, kk: (i, j)),
        _sds((m, n), out_dtype), ((1,), (0,)), (tm, tn), ride=ride)


def _mm_nt(name, a, b3, out_dtype, add=None, tm=1024, tn=1024, tk=2048, ride=None):
    m, kc = a.shape
    nj, n, kj = b3.shape
    assert kc == nj * kj
    tm, tn, tk = _tile(m, tm, 8), _tile(n, tn), _tile(kj, tk)
    kb = kj // tk
    o_spec = pl.BlockSpec((tm, tn), lambda i, j, kk: (i, j))
    return _matmul(
        name, a, b3, (m // tm, n // tn, kc // tk),
        pl.BlockSpec((tm, tk), lambda i, j, kk: (i, kk)),
        pl.BlockSpec((None, tn, tk), lambda i, j, kk: (kk // kb, j, kk % kb)),
        o_spec, _sds((m, n), out_dtype), ((1,), (1,)), (tm, tn), add=add, add_spec=o_spec, ride=ride)


def _mm_tn(name, a, b, nj, out_dtype, tm=512, tn=1024, ts=4096, ride=None):
    s, m = a.shape
    s2, n = b.shape
    assert s == s2 and n % nj == 0
    width = n // nj
    tm, tn, ts = _tile(m, tm), _tile(width, tn), _tile(s, ts)
    nb = width // tn
    return _matmul(
        name, a, b, (m // tm, n // tn, s // ts),
        pl.BlockSpec((ts, tm), lambda i, j, kk: (kk, i)),
        pl.BlockSpec((ts, tn), lambda i, j, kk: (kk, j)),
        pl.BlockSpec((None, tm, tn), lambda i, j, kk: (j // nb, i, j % nb)),
        _sds((nj, m, width), out_dtype), ((0,), (0,)), (tm, tn), ride=ride)


def _group_nn(name, a, w, out_dtype, transpose_w=False, tm=1024):
    s, gc = a.shape
    g, c, _ = w.shape
    tm = _tile(s, tm, 8)
    return _matmul(
        name, a, w, (s // tm, g, 1),
        pl.BlockSpec((tm, c), lambda i, j, kk: (i, j)),
        pl.BlockSpec((None, c, c), lambda i, j, kk: (j, 0, 0)),
        pl.BlockSpec((tm, c), lambda i, j, kk: (i, j)),
        _sds((s, gc), out_dtype), ((1,), (1,)) if transpose_w else ((1,), (0,)), (tm, c))


def _group_tn(name, a, b, g, out_dtype, ts=4096):
    s, gc = a.shape
    c = gc // g
    ts = _tile(s, ts, 8)
    return _matmul(
        name, a, b, (g, 1, s // ts),
        pl.BlockSpec((ts, c), lambda i, j, kk: (kk, i)),
        pl.BlockSpec((ts, c), lambda i, j, kk: (kk, i)),
        pl.BlockSpec((None, c, c), lambda i, j, kk: (i, 0, 0)),
        _sds((g, c, c), out_dtype), ((0,), (0,)), (c, c))


def _row(tm, d):
    return pl.BlockSpec((tm, d), lambda i: (i, 0))


def _vec(d):
    return pl.BlockSpec((1, d), lambda i: (0, 0))


def _rows_call(name, body, ins, in_specs, outs, out_specs, steps, accumulates):
    return pl.pallas_call(
        body, name=name, grid=(steps,), in_specs=in_specs, out_specs=out_specs, out_shape=outs,
        compiler_params=_params("arbitrary" if accumulates else "parallel"),
    )(*ins)


def _accumulate(ref, value):
    @pl.when(pl.program_id(0) == 0)
    def _():
        ref[...] = value

    @pl.when(pl.program_id(0) > 0)
    def _():
        ref[...] += value


def _colsum(v):
    return jnp.sum(v, axis=0, keepdims=True)


def _rms_fwd(x, gain, tm=256):
    s, d = x.shape

    def body(x_ref, g_ref, h_ref):
        v = x_ref[...]
        h_ref[...] = (v * _rstd(v) * g_ref[...]).astype(BF16)

    return _rows_call("rms_fwd", body, (x, gain), [_row(tm, d), _vec(d)], _sds((s, d), BF16), _row(tm, d),
                      s // tm, False)


def _mix_fwd(ga, gp, ya, yp, tm=256):
    s, d = ga.shape

    def body(ga_ref, gp_ref, ya_ref, yp_ref, o_ref):
        ga, gp, ya, yp = (ref[...].astype(F32) for ref in (ga_ref, gp_ref, ya_ref, yp_ref))
        o_ref[...] = (_sigmoid(ga) * ya + _sigmoid(gp) * yp).astype(BF16)

    return _rows_call("mix_fwd", body, (ga, gp, ya, yp), [_row(tm, d)] * 4, _sds((s, d), BF16), _row(tm, d),
                      s // tm, False)


def _scale_cols(v, scale, tm=256):
    s, d = v.shape

    def body(v_ref, s_ref, o_ref):
        o_ref[...] = (v_ref[...] * s_ref[...]).astype(BF16)

    return _rows_call("pool_scale_fwd", body, (v, scale), [_row(tm, d), _vec(d)], _sds((s, d), BF16), _row(tm, d),
                      s // tm, False)


def _post_pre(x, y, gain_post, gain_pre, tm=256):
    s, d = x.shape

    def body(x_ref, y_ref, gp_ref, gn_ref, x2_ref, h_ref):
        y = y_ref[...]
        x2 = x_ref[...] + y * _rstd(y) * gp_ref[...]
        x2_ref[...] = x2
        h_ref[...] = (x2 * _rstd(x2) * gn_ref[...]).astype(BF16)

    return _rows_call("post_pre", body, (x, y, gain_post, gain_pre), [_row(tm, d), _row(tm, d), _vec(d), _vec(d)],
                      (_sds((s, d), F32), _sds((s, d), BF16)), (_row(tm, d), _row(tm, d)), s // tm, False)


def _post(x, y, gain_post, tm=256):
    s, d = x.shape

    def body(x_ref, y_ref, gp_ref, o_ref):
        y = y_ref[...]
        o_ref[...] = x_ref[...] + y * _rstd(y) * gp_ref[...]

    return _rows_call("post", body, (x, y, gain_post), [_row(tm, d), _row(tm, d), _vec(d)], _sds((s, d), F32),
                      _row(tm, d), s // tm, False)


def _final(x3, t, e, gain, target, tm=256):
    s, d = x3.shape

    def body(x_ref, t_ref, e_ref, g_ref, y_ref, dx_ref, dt_ref, de_ref, dg_ref, loss_ref):
        sg = _sigmoid(t_ref[...])
        ev = e_ref[...]
        pe = sg * ev
        diff = x_ref[...] + pe * _rstd(pe) * g_ref[...] - y_ref[...]
        dx = diff * (1.0 / d)
        dx_ref[...] = dx
        part = 0.5 * jnp.sum(jnp.mean(diff * diff, axis=-1, keepdims=True), axis=0, keepdims=True)
        _accumulate(loss_ref, jnp.broadcast_to(part, loss_ref.shape))
        dpe, dgain = _rms_bwd(dx, g_ref[...], pe)
        de_ref[...] = (dpe * sg).astype(BF16)
        dt_ref[...] = (dpe * ev * sg * (1.0 - sg)).astype(BF16)
        _accumulate(dg_ref, _colsum(dgain))

    return _rows_call("final", body, (x3, t, e, gain, target),
                      [_row(tm, d), _row(tm, d), _row(tm, d), _vec(d), _row(tm, d)],
                      (_sds((s, d), F32), _sds((s, d), BF16), _sds((s, d), BF16), _sds((1, d), F32),
                       _sds((8, 128), F32)),
                      (_row(tm, d), _row(tm, d), _row(tm, d), _vec(d), pl.BlockSpec((8, 128), lambda i: (0, 0))),
                      s // tm, True)


def _concat_cols(name, parts, tm=256):
    s = parts[0].shape[0]
    widths = [a.shape[1] for a in parts]

    def body(*refs):
        o_ref, offset = refs[-1], 0
        for ref, width in zip(refs[:-1], widths):
            o_ref[:, offset:offset + width] = ref[...].astype(BF16)
            offset += width

    return _rows_call(name, body, parts, [_row(tm, width) for width in widths], _sds((s, sum(widths)), BF16),
                      _row(tm, sum(widths)), s // tm, False)


def _bwd_post(dx, y, gain, tm=256):
    s, d = dx.shape

    def body(dx_ref, y_ref, g_ref, dy_ref, dg_ref):
        dy, dgain = _rms_bwd(dx_ref[...], g_ref[...], y_ref[...])
        dy_ref[...] = dy.astype(BF16)
        _accumulate(dg_ref, _colsum(dgain))

    return _rows_call("bwd_post", body, (dx, y, gain), [_row(tm, d), _row(tm, d), _vec(d)],
                      (_sds((s, d), BF16), _sds((1, d), F32)), (_row(tm, d), _vec(d)), s // tm, True)


def _bwd_mid(dx3, dh2, x2, gain_pre, mo, gain_post, tm=128):
    s, d = dx3.shape

    def body(dx3_ref, dh_ref, x2_ref, gn_ref, mo_ref, gp_ref, dx2_ref, dmo_ref, dgn_ref, dgp_ref):
        dv, dgn = _rms_bwd(dh_ref[...], gn_ref[...], x2_ref[...])
        dx2 = dx3_ref[...] + dv
        dx2_ref[...] = dx2
        dmo, dgp = _rms_bwd(dx2, gp_ref[...], mo_ref[...])
        dmo_ref[...] = dmo.astype(BF16)
        _accumulate(dgn_ref, _colsum(dgn))
        _accumulate(dgp_ref, _colsum(dgp))

    return _rows_call("bwd_mid", body, (dx3, dh2, x2, gain_pre, mo, gain_post),
                      [_row(tm, d), _row(tm, d), _row(tm, d), _vec(d), _row(tm, d), _vec(d)],
                      (_sds((s, d), F32), _sds((s, d), BF16), _sds((1, d), F32), _sds((1, d), F32)),
                      (_row(tm, d), _row(tm, d), _vec(d), _vec(d)), s // tm, True)


def _bwd_first(dx2, dh1, x, gain, tm=256):
    s, d = dx2.shape

    def body(dx2_ref, dh_ref, x_ref, g_ref, dx_ref, dg_ref):
        dv, dgain = _rms_bwd(dh_ref[...], g_ref[...], x_ref[...])
        dx_ref[...] = dx2_ref[...] + dv
        _accumulate(dg_ref, _colsum(dgain))

    return _rows_call("bwd_first", body, (dx2, dh1, x, gain), [_row(tm, d), _row(tm, d), _row(tm, d), _vec(d)],
                      (_sds((s, d), F32), _sds((1, d), F32)), (_row(tm, d), _vec(d)), s // tm, True)


def _bwd_mix(dmixed, ga, gp, ya, yp, tm=256):
    s, d = dmixed.shape

    def body(dm_ref, ga_ref, gp_ref, ya_ref, yp_ref, dya_ref, dyp_ref, dga_ref, dgp_ref):
        dm = dm_ref[...]
        ga, gp, ya, yp = (ref[...].astype(F32) for ref in (ga_ref, gp_ref, ya_ref, yp_ref))
        sa, sp = _sigmoid(ga), _sigmoid(gp)
        dya_ref[...] = (dm * sa).astype(BF16)
        dyp_ref[...] = (dm * sp).astype(BF16)
        dga_ref[...] = (dm * ya * sa * (1.0 - sa)).astype(BF16)
        dgp_ref[...] = (dm * yp * sp * (1.0 - sp)).astype(BF16)

    return _rows_call("bwd_mix", body, (dmixed, ga, gp, ya, yp), [_row(tm, d)] * 5,
                      (_sds((s, d), BF16),) * 4, (_row(tm, d),) * 4, s // tm, False)


def _bwd_pool_scale(dps, pg, scale, tm=256):
    s, d = dps.shape

    def body(d_ref, pg_ref, s_ref, dpg_ref, ds_ref):
        dv = d_ref[...]
        dpg_ref[...] = (dv * s_ref[...]).astype(BF16)
        _accumulate(ds_ref, _colsum(dv * pg_ref[...]))

    return _rows_call("bwd_pool_scale", body, (dps, pg, scale), [_row(tm, d), _row(tm, d), _vec(d)],
                      (_sds((s, d), BF16), _sds((1, d), F32)), (_row(tm, d), _vec(d)), s // tm, True)


def _shift_down(v, k, rows):
    return jnp.where(rows >= k, pltpu.roll(v, k, 0), 0.0)


def _shift_up(v, k, rows):
    s = v.shape[0]
    return jnp.where(rows < s - k, pltpu.roll(v, s - k, 0), 0.0)


def _window_pick(group, sums, rows):
    total = sums[-1]
    width = jnp.full((), POOL_WINDOWS[-1], jnp.int32)
    for g in range(len(POOL_WINDOWS) - 2, -1, -1):
        total = jnp.where(group == g, sums[g], total)
        width = jnp.where(group == g, POOL_WINDOWS[g], width)
    return total, jnp.minimum(rows + 1, width).astype(F32)


def _doubling(v, shift, rows):
    sums, k = [], 1
    for _ in POOL_WINDOWS:
        v = v + shift(v, k, rows)
        sums.append(v)
        k *= 2
    return sums


def _pool_fwd(u):
    s, width = u.shape
    per_group = width // len(POOL_WINDOWS) // POOL_TILE

    def body(u_ref, o_ref):
        v = u_ref[...]
        rows = lax.broadcasted_iota(jnp.int32, (s, 1), 0)
        total, count = _window_pick(pl.program_id(0), _doubling(v, _shift_down, rows), rows)
        o_ref[...] = (total / count - v).astype(BF16)

    spec = pl.BlockSpec((s, POOL_TILE), lambda g, j: (0, g * per_group + j))
    return pl.pallas_call(body, name="pool_fwd", grid=(len(POOL_WINDOWS), per_group), in_specs=[spec], out_specs=spec,
                          out_shape=_sds((s, width), BF16), compiler_params=_params("parallel", "parallel"))(u)


def _pool_bwd(dpooled):
    s, width = dpooled.shape
    per_group = width // len(POOL_WINDOWS) // POOL_TILE

    def body(d_ref, o_ref):
        dv = d_ref[...]
        rows = lax.broadcasted_iota(jnp.int32, (s, 1), 0)
        group = pl.program_id(0)
        _, count = _window_pick(group, [dv] * len(POOL_WINDOWS), rows)
        total, _ = _window_pick(group, _doubling(dv / count, _shift_up, rows), rows)
        o_ref[...] = (total - dv).astype(BF16)

    spec = pl.BlockSpec((s, POOL_TILE), lambda g, j: (0, g * per_group + j))
    return pl.pallas_call(body, name="pool_bwd", grid=(len(POOL_WINDOWS), per_group), in_specs=[spec], out_specs=spec,
                          out_shape=_sds((s, width), BF16), compiler_params=_params("parallel", "parallel"))(dpooled)


def _conv(v, w_ref, b_ref, rows):
    out = b_ref[...] + _shift_down(v, 2, rows) * w_ref[0:1, :]
    out = out + _shift_down(v, 1, rows) * w_ref[1:2, :]
    return out + v * w_ref[2:3, :]


def _gelu_parts(v):
    th = jnp.tanh(GELU_C * (v + GELU_A * v * v * v))
    return th, 0.5 * v * (1.0 + th)


def _conv_gelu_fwd(up, conv_w, conv_b, tc=128):
    s, f2 = up.shape
    f = f2 // 2
    nb = f // tc

    def body(g_ref, v_ref, wg_ref, wv_ref, bg_ref, bv_ref, o_ref):
        rows = lax.broadcasted_iota(jnp.int32, (s, 1), 0)
        _, gl = _gelu_parts(_conv(g_ref[...], wg_ref, bg_ref, rows))
        o_ref[...] = (gl * _conv(v_ref[...], wv_ref, bv_ref, rows)).astype(BF16)

    lo = lambda r: pl.BlockSpec((r, tc), lambda j: (0, j))
    hi = lambda r: pl.BlockSpec((r, tc), lambda j: (0, j + nb))
    return pl.pallas_call(
        body, name="conv_gelu_fwd", grid=(nb,), in_specs=[lo(s), hi(s), lo(3), hi(3), lo(1), hi(1)], out_specs=lo(s),
        out_shape=_sds((s, f), BF16), compiler_params=_params("parallel"))(up, up, conv_w, conv_w, conv_b, conv_b)


def _conv_gelu_bwd(up, dact, conv_w, conv_b, tc=128):
    s, f2 = up.shape
    f = f2 // 2
    nb = f // tc

    def body(g_ref, v_ref, d_ref, wg_ref, wv_ref, bg_ref, bv_ref, dg_ref, dv_ref, dwg_ref, dwv_ref, dbg_ref, dbv_ref):
        rows = lax.broadcasted_iota(jnp.int32, (s, 1), 0)
        dact = d_ref[...].astype(F32)

        def back(pre, dc, w_ref, dpre_ref, dw_ref, db_ref):
            dc1, dc2 = _shift_up(dc, 1, rows), _shift_up(dc, 2, rows)
            db_ref[...] = _colsum(dc)
            dw_ref[0:1, :] = _colsum(dc2 * pre)
            dw_ref[1:2, :] = _colsum(dc1 * pre)
            dw_ref[2:3, :] = _colsum(dc * pre)
            dpre_ref[...] = (dc * w_ref[2:3, :] + dc1 * w_ref[1:2, :] + dc2 * w_ref[0:1, :]).astype(BF16)

        gate, val = g_ref[...], v_ref[...]
        cg = _conv(gate, wg_ref, bg_ref, rows)
        cv = _conv(val, wv_ref, bv_ref, rows)
        th, gl = _gelu_parts(cg)
        dgl = 0.5 * (1.0 + th) + 0.5 * cg * (1.0 - th * th) * GELU_C * (1.0 + 3.0 * GELU_A * cg * cg)
        back(val, dact * gl, wv_ref, dv_ref, dwv_ref, dbv_ref)
        back(gate, dact * cv * dgl, wg_ref, dg_ref, dwg_ref, dbg_ref)

    lo = lambda r: pl.BlockSpec((r, tc), lambda j: (0, j))
    hi = lambda r: pl.BlockSpec((r, tc), lambda j: (0, j + nb))
    return pl.pallas_call(
        body, name="conv_gelu_bwd", grid=(nb,),
        in_specs=[lo(s), hi(s), lo(s), lo(3), hi(3), lo(1), hi(1)],
        out_specs=(lo(s), lo(s), lo(3), lo(3), lo(1), lo(1)),
        out_shape=(_sds((s, f), BF16), _sds((s, f), BF16), _sds((3, f), F32), _sds((3, f), F32),
                   _sds((1, f), F32), _sds((1, f), F32)),
        compiler_params=_params("parallel"))(up, up, dact, conv_w, conv_w, conv_b, conv_b)


def _dot(a, b, dims):
    return lax.dot_general(a, b, (dims, ((), ())), preferred_element_type=F32)


def _running_sums(v, tri):
    hi = v.astype(BF16)
    lo = (v - hi.astype(F32)).astype(BF16)
    return _dot(hi, tri, ((1,), (0,))) + _dot(lo, tri, ((1,), (0,)))


def _attn_scores(q, kt, mask):
    z = _dot(q, kt, ((1,), (1,))) * (HEAD_DIM ** -0.5)
    e = jnp.exp(-jnp.abs(z))
    log_1m_beta = -(jnp.maximum(z, 0.0) + jnp.log(1.0 + e))
    return z, e, log_1m_beta if mask is None else jnp.where(mask, log_1m_beta, 0.0)


def _masked(v, mask):
    return v if mask is None else jnp.where(mask, v, 0.0)


def _attn_consts(t):
    rows = lax.broadcasted_iota(jnp.int32, (t, t), 0)
    cols = lax.broadcasted_iota(jnp.int32, (t, t), 1)
    return (rows >= cols).astype(BF16), (rows <= cols).astype(BF16)


def _causal(rows, t):
    return lax.broadcasted_iota(jnp.int32, (rows, t), 1) < lax.broadcasted_iota(jnp.int32, (rows, t), 0)


def _attn_specs(s, tq, heads):
    lanes = ATTN_HEADS * HEAD_DIM
    steps = heads // ATTN_HEADS
    q = pl.BlockSpec((tq, lanes), lambda h, i: (i, h))
    k = pl.BlockSpec((s, lanes), lambda h, i: (0, steps + h))
    whole = pl.BlockSpec((s, lanes), lambda h, i: (0, h))
    rows = pl.BlockSpec((ATTN_HEADS, tq, 1), lambda h, i: (h, i, 0))
    return q, k, whole, rows


def _attn_blocks(s):
    t = min(ATTN_BLOCK, s)
    tq = min(ATTN_Q_BLOCK, s)
    assert tq % t == 0 and s % tq == 0
    return t, tq, tq // t


def _head(p):
    return pl.ds(p * HEAD_DIM, HEAD_DIM)


def _attn_fwd(qk, v, ride=None):
    s, width = v.shape
    heads = width // HEAD_DIM
    t, tq, ratio = _attn_blocks(s)
    assert heads % ATTN_HEADS == 0
    grid = (heads // ATTN_HEADS, s // tq)
    extra, extra_specs, extra_out, aliases, scratch, semantics, hooks = _ride_along(
        ride, grid, 3, 2, ("parallel", "parallel"))

    def body(*refs):
        q_ref, k_ref, v_ref = refs[:3]
        o_ref, lsum_ref = refs[3 + len(extra):5 + len(extra)]
        before, after = hooks(refs)
        before()
        i = pl.program_id(1)
        tri, _ = _attn_consts(t)

        def block(p, rows, start, carry, mask):
            acc, log_rest = carry
            z, _, log_1m_beta = _attn_scores(q_ref[rows, _head(p)], k_ref[pl.ds(start, t), _head(p)], mask)
            total = _running_sums(log_1m_beta, tri) + log_rest
            a = _masked(jnp.exp(z + total), mask)
            return acc + _dot(a.astype(BF16), v_ref[pl.ds(start, t), _head(p)], ((1,), (0,))), total[:, 0:1]

        def own_rows(p):
            carry = (jnp.zeros((tq, HEAD_DIM), F32), jnp.zeros((tq, 1), F32))
            for r in range(ratio - 1, -1, -1):
                below = tuple(val[r * t:] for val in carry)
                below = block(p, pl.ds(r * t, tq - r * t), pl.multiple_of(i * tq + r * t, t), below,
                              _causal(tq - r * t, t))
                carry = tuple(jnp.concatenate([val[:r * t], new], axis=0) if r else new
                              for val, new in zip(carry, below))
            return carry

        def step(n, carry):
            start = pl.multiple_of((i * ratio - 1 - n) * t, t)
            return tuple(block(p, pl.ds(0, tq), start, carry[p], None) for p in range(ATTN_HEADS))

        carry = lax.fori_loop(0, i * ratio, step, tuple(own_rows(p) for p in range(ATTN_HEADS)))
        for p in range(ATTN_HEADS):
            o_ref[:, _head(p)] = carry[p][0].astype(BF16)
            lsum_ref[p] = carry[p][1]
        after()

    q_spec, k_spec, whole, rows = _attn_specs(s, tq, heads)
    outs = pl.pallas_call(
        body, name="attn_fwd", grid=grid, in_specs=[q_spec, k_spec, whole] + extra_specs,
        out_specs=(q_spec, rows, *extra_specs),
        out_shape=(_sds((s, width), BF16), _sds((heads, s, 1), F32), *extra_out),
        input_output_aliases=aliases, scratch_shapes=scratch,
        compiler_params=_params(*semantics))(qk, qk, v, *extra)
    return outs[0], outs[1], list(outs[2:])


def _attn_bwd(qk, v, lsum, do, ride=None):
    s, width = v.shape
    heads = width // HEAD_DIM
    t, tq, ratio = _attn_blocks(s)
    grid = (heads // ATTN_HEADS, s // tq)
    extra, extra_specs, extra_out, aliases, scratch, semantics, hooks = _ride_along(
        ride, grid, 5, 3, ("parallel", "arbitrary"))

    def body(*refs):
        q_ref, k_ref, v_ref, lsum_ref, do_ref = refs[:5]
        dq_ref, dk_ref, dv_ref = refs[5 + len(extra):8 + len(extra)]
        before, after = hooks(refs)
        before()
        i = pl.program_id(1)

        @pl.when(i == 0)
        def _():
            dk_ref[...] = jnp.zeros_like(dk_ref)
            dv_ref[...] = jnp.zeros_like(dv_ref)

        _, triu = _attn_consts(t)

        def block(p, rows, start, carry, mask):
            dq, log_rest, g_before = carry
            q, do_blk = q_ref[rows, _head(p)], do_ref[rows, _head(p)]
            kt, vt = k_ref[pl.ds(start, t), _head(p)], v_ref[pl.ds(start, t), _head(p)]
            z, e, log_1m_beta = _attn_scores(q, kt, mask)
            upto = _running_sums(log_1m_beta, triu)
            a = _masked(jnp.exp(z + log_rest - upto + log_1m_beta), mask)
            g = a * _dot(do_blk, vt, ((1,), (1,)))
            g_upto = _running_sums(g, triu) + g_before
            sig = jnp.where(z >= 0.0, 1.0, e) / (1.0 + e)
            dz = (_masked(g - sig * g_upto, mask) * (HEAD_DIM ** -0.5)).astype(BF16)
            dk_ref[pl.ds(start, t), _head(p)] += _dot(dz, q, ((0,), (0,)))
            dv_ref[pl.ds(start, t), _head(p)] += _dot(a.astype(BF16), do_blk, ((0,), (0,)))
            return dq + _dot(dz, kt, ((1,), (0,))), log_rest - upto[:, t - 1:t], g_upto[:, t - 1:t]

        def step(n, carry):
            start = pl.multiple_of(n * t, t)
            return tuple(block(p, pl.ds(0, tq), start, carry[p], None) for p in range(ATTN_HEADS))

        carry = tuple((jnp.zeros((tq, HEAD_DIM), F32), lsum_ref[p], jnp.zeros((tq, 1), F32))
                      for p in range(ATTN_HEADS))
        carry = lax.fori_loop(0, i * ratio, step, carry)
        for p in range(ATTN_HEADS):
            state = carry[p]
            for r in range(ratio):
                below = tuple(val[r * t:] for val in state)
                below = block(p, pl.ds(r * t, tq - r * t), pl.multiple_of(i * tq + r * t, t), below,
                              _causal(tq - r * t, t))
                state = tuple(jnp.concatenate([val[:r * t], new], axis=0) if r else new
                              for val, new in zip(state, below))
            dq_ref[:, _head(p)] = state[0].astype(BF16)
        after()

    q_spec, k_spec, whole, rows = _attn_specs(s, tq, heads)
    outs = pl.pallas_call(
        body, name="attn_bwd", grid=grid, in_specs=[q_spec, k_spec, whole, rows, q_spec] + extra_specs,
        out_specs=(q_spec, whole, whole, *extra_specs),
        out_shape=(_sds((s, width), BF16), _sds((s, width), F32), _sds((s, width), F32), *extra_out),
        input_output_aliases=aliases, scratch_shapes=scratch,
        compiler_params=_params(*semantics))(qk, qk, v, lsum, do, *extra)
    return outs[0], outs[1], outs[2], list(outs[3:])


def _as_rows(a):
    return a.reshape(-1, a.shape[-1])


def _flat_call(name, body, ins, out_dtypes, block_bytes=1 << 20):
    shape = ins[0].shape
    rows, cols = _as_rows(ins[0]).shape
    tr = _tile(rows, max(8, block_bytes // (4 * cols)), 8)
    spec = pl.BlockSpec((tr, cols), lambda i: (i, 0))
    outs = pl.pallas_call(
        body, name=name, grid=(rows // tr,), in_specs=[spec] * len(ins), out_specs=tuple([spec] * len(out_dtypes)),
        out_shape=tuple(_sds((rows, cols), dt) for dt in out_dtypes), compiler_params=_params("parallel"),
    )(*[_as_rows(a) for a in ins])
    return [o.reshape(shape) for o in outs]


def _scalars(*values):
    return jnp.stack([jnp.asarray(v, jnp.int32) for v in values])


def _row_tile(rows, cols):
    return _tile(rows, max(ROWS_PER_TILE, (1 << 22) // (4 * cols)), ROWS_PER_TILE)


def _place_cast(w, chip):
    cols = w.shape[-1]
    rows = w.size // cols
    tr = _row_tile(rows, cols)

    def body(s_ref, w_ref, o_ref):
        o_ref[...] = w_ref[...].astype(BF16)

    out = pl.pallas_call(
        body, name="place_cast",
        grid_spec=pltpu.PrefetchScalarGridSpec(
            num_scalar_prefetch=1, grid=(rows // tr,),
            in_specs=[pl.BlockSpec((tr, cols), lambda i, s: (i, 0))],
            out_specs=pl.BlockSpec((None, tr, cols), lambda i, s: (s[0], i, 0))),
        out_shape=_sds((N_CHIPS, rows, cols), BF16), compiler_params=_params("parallel"),
    )(_scalars(chip), w.reshape(rows, cols))
    return out.reshape((N_CHIPS,) + w.shape)


def _add_halves(g, got, core):
    _, rows, cols = got.shape
    tr = _row_tile(rows, cols)

    def body(s_ref, a_ref, b_ref, o_ref):
        o_ref[...] = (a_ref[...].astype(F32) + b_ref[...].astype(F32)).astype(BF16)

    spec = pl.BlockSpec((None, tr, cols), lambda j, i, s: (j, i, 0))
    return pl.pallas_call(
        body, name="add_halves",
        grid_spec=pltpu.PrefetchScalarGridSpec(
            num_scalar_prefetch=1, grid=(N_CHIPS, rows // tr),
            in_specs=[pl.BlockSpec((None, None, tr, cols), lambda j, i, s: (j, s[0], i, 0)), spec], out_specs=spec),
        out_shape=_sds(got.shape, BF16), compiler_params=_params("parallel", "parallel"),
    )(_scalars(core), g, got)


def _sum_parts(own, others, chip, core):
    _, rows, cols = own.shape
    tr = _row_tile(rows, cols)

    def body(s_ref, a_ref, t_ref, o_ref):
        total = a_ref[...].astype(F32)
        for j in range(N_CHIPS - 1):
            total = total + t_ref[j].astype(F32)
        o_ref[...] = total

    return pl.pallas_call(
        body, name="sum_parts",
        grid_spec=pltpu.PrefetchScalarGridSpec(
            num_scalar_prefetch=1, grid=(rows // tr,),
            in_specs=[pl.BlockSpec((None, tr, cols), lambda i, s: (s[0], i, 0)),
                      pl.BlockSpec((N_CHIPS - 1, tr, cols), lambda i, s: (0, i, 0))],
            out_specs=pl.BlockSpec((None, tr, cols), lambda i, s: (s[1], i, 0))),
        out_shape=_sds((2, rows, cols), F32), compiler_params=_params("parallel"),
    )(_scalars(chip, core), own, others)


def _sum_leading(a, name):
    n = a.shape[0]
    shape = a.shape[1:]
    cols = shape[-1]
    rows = a.size // (n * cols)
    tr = _tile(rows, max(8, (1 << 20) // (4 * cols)), 8)

    def body(a_ref, o_ref):
        total = a_ref[0].astype(F32)
        for j in range(1, n):
            total = total + a_ref[j].astype(F32)
        o_ref[...] = total

    out = pl.pallas_call(
        body, name=name, grid=(rows // tr,), in_specs=[pl.BlockSpec((n, tr, cols), lambda i: (0, i, 0))],
        out_specs=pl.BlockSpec((tr, cols), lambda i: (i, 0)), out_shape=_sds((rows, cols), F32),
        compiler_params=_params("parallel"))(a.reshape(n, rows, cols))
    return out.reshape(shape)


def _adamw(w, g, m, v):
    def body(w_ref, g_ref, m_ref, v_ref, go_ref, d_ref, nm_ref, nv_ref):
        gv = g_ref[...]
        go_ref[...] = gv
        nm = ADAM_B1 * m_ref[...] + (1.0 - ADAM_B1) * gv
        nv = ADAM_B2 * v_ref[...] + (1.0 - ADAM_B2) * jnp.square(gv)
        nm_ref[...] = nm
        nv_ref[...] = nv
        m_hat = nm / (1.0 - ADAM_B1 ** ADAM_STEP)
        v_hat = nv / (1.0 - ADAM_B2 ** ADAM_STEP)
        d_ref[...] = -ADAM_LR * (m_hat / (jnp.sqrt(v_hat) + ADAM_EPS) + ADAM_WD * w_ref[...])

    return _flat_call("adamw", body, (w, g, m, v), (F32, F32, F32, F32), block_bytes=1 << 21)


def _mesh_position():
    return lax.axis_index("x"), lax.axis_index("y"), lax.axis_index("c")


def _other_chips(x, y):
    return [(1 - x, y), (x, 1 - y), (1 - x, 1 - y)]


def _chip_index(chip):
    return 2 * chip[0] + chip[1]


def _remote(src, dst, send_sem, recv_sem, device):
    return pltpu.make_async_remote_copy(src_ref=src, dst_ref=dst, send_sem=send_sem, recv_sem=recv_sem,
                                        device_id=device, device_id_type=MESH)


def _pieces(view):
    shape = list(view.shape)
    counts = []
    for axis in range(len(shape) - 1):
        want = -(-(view.dtype.itemsize * functools.reduce(lambda a, b: a * b, shape)) // DMA_PIECE_BYTES)
        unit = 1 if axis < len(shape) - 2 else ROWS_PER_TILE
        n, k = shape[axis], 1
        for cand in range(1, n + 1):
            if n % cand == 0 and (n // cand) % unit == 0:
                k = cand
                if cand >= want:
                    break
        counts.append(k)
        shape[axis] = n // k
    grid = [()]
    for axis, k in enumerate(counts):
        grid = [idx + (pl.ds(i * shape[axis], shape[axis]),) for idx in grid for i in range(k)]
    return grid


def _start_remote(src, dst, send_sem, recv_sem, device):
    for idx in _pieces(src):
        _remote(src.at[idx], dst.at[idx], send_sem, recv_sem, device).start()
    return _remote(src, dst, send_sem, recv_sem, device)


class _Ride:
    def __init__(self, operands, out_shapes, aliased, scratch_shapes, first, mid, last):
        self.operands, self.out_shapes, self.aliased = list(operands), list(out_shapes), aliased
        self.scratch_shapes, self.first, self.mid, self.last = list(scratch_shapes), first, mid, last

    def aliases(self, first_in, first_out):
        return {first_in + w: first_out + w for w in range(len(self.operands))} if self.aliased else {}


def _run_ride(name, ride):
    n = len(ride.operands)

    def body(*refs):
        ins, outs, sems = refs[:n], refs[n:2 * n], refs[2 * n:]
        ride.first(ins, outs, sems)
        if ride.mid is not None:
            ride.mid(ins, outs, sems)
        ride.last(ins, outs, sems)

    return pl.pallas_call(
        body, name=name, in_specs=[ANY] * n, out_specs=tuple([ANY] * n), out_shape=tuple(ride.out_shapes),
        input_output_aliases=ride.aliases(0, 0), scratch_shapes=ride.scratch_shapes,
    )(*ride.operands)


def _gather_ride(blocks):
    n = len(blocks)
    halves = [a.shape[1] // 2 for a in blocks]
    assert all(a.shape[1] % 2 == 0 for a in blocks)
    pairs = [(w, j) for w in range(n) for j in range(3)]

    def plan(outs, sems):
        ici_send, ici_recv, d2d_send, d2d_recv = sems
        x, y, c = _mesh_position()
        chips = _other_chips(x, y)

        def half(w, chip, which):
            return outs[w].at[chip, pl.ds(which * halves[w], halves[w])]

        def over_ici(w, j, chip):
            there = half(w, chip, c)
            return there, there, ici_send.at[3 * w + j], ici_recv.at[3 * w + j], (*chips[j], c)

        def over_d2d(w, j, which):
            there = half(w, _chip_index(chips[j]), which)
            return there, there, d2d_send.at[3 * w + j], d2d_recv.at[3 * w + j], (x, y, 1 - c)

        return _chip_index((x, y)), c, chips, over_ici, over_d2d

    def first(ins, outs, sems):
        me, _, _, over_ici, _ = plan(outs, sems)
        for w, j in pairs:
            _start_remote(*over_ici(w, j, me))

    def mid(ins, outs, sems):
        _, c, chips, over_ici, over_d2d = plan(outs, sems)
        for w, j in pairs:
            _remote(*over_ici(w, j, _chip_index(chips[j]))).wait_recv()
            _start_remote(*over_d2d(w, j, c))

    def last(ins, outs, sems):
        me, c, _, over_ici, over_d2d = plan(outs, sems)
        for w, j in pairs:
            _remote(*over_d2d(w, j, 1 - c)).wait_recv()
        for w, j in pairs:
            _remote(*over_ici(w, j, me)).wait_send()
            _remote(*over_d2d(w, j, c)).wait_send()

    return _Ride(blocks, [_sds(a.shape, a.dtype) for a in blocks], True,
                 [pltpu.SemaphoreType.DMA((3 * n,))] * 4, first, mid, last)


def _swap_halves(grads):
    n = len(grads)

    def body(*refs):
        ins, got = refs[:n], refs[n:2 * n]
        send_sem, recv_sem = refs[2 * n:]
        x, y, c = _mesh_position()
        swaps = [_start_remote(ins[w].at[pl.ds(0, N_CHIPS), 1 - c], got[w], send_sem.at[w], recv_sem.at[w],
                               (x, y, 1 - c)) for w in range(n)]
        for cp in swaps:
            cp.wait()

    return pl.pallas_call(
        body, name="swap_halves", in_specs=[ANY] * n, out_specs=tuple([ANY] * n),
        out_shape=tuple(_sds((N_CHIPS,) + a.shape[2:], a.dtype) for a in grads),
        scratch_shapes=[pltpu.SemaphoreType.DMA((n,))] * 2,
    )(*grads)


def _scatter_ride(parts):
    n = len(parts)
    pairs = [(w, j) for w in range(n) for j in range(3)]

    def plan(ins, outs, sems):
        send_sem, recv_sem = sems
        x, y, c = _mesh_position()
        chips = _other_chips(x, y)

        def to_chip(w, j):
            return (ins[w].at[_chip_index(chips[j])], outs[w].at[j], send_sem.at[3 * w + j], recv_sem.at[3 * w + j],
                    (*chips[j], c))

        return to_chip

    def first(ins, outs, sems):
        to_chip = plan(ins, outs, sems)
        for w, j in pairs:
            _start_remote(*to_chip(w, j))

    def last(ins, outs, sems):
        to_chip = plan(ins, outs, sems)
        for w, j in pairs:
            _remote(*to_chip(w, j)).wait_recv()
        for w, j in pairs:
            _remote(*to_chip(w, j)).wait_send()

    return _Ride(parts, [_sds((N_CHIPS - 1,) + a.shape[1:], a.dtype) for a in parts], False,
                 [pltpu.SemaphoreType.DMA((3 * n,))] * 2, first, None, last)


def _join_ride(halves):
    n = len(halves)

    def plan(outs, sems):
        send_sem, recv_sem = sems
        x, y, c = _mesh_position()

        def half(w, which):
            return outs[w].at[which], outs[w].at[which], send_sem.at[w], recv_sem.at[w], (x, y, 1 - c)

        return c, half

    def first(ins, outs, sems):
        c, half = plan(outs, sems)
        for w in range(n):
            _start_remote(*half(w, c))

    def last(ins, outs, sems):
        c, half = plan(outs, sems)
        for w in range(n):
            _remote(*half(w, 1 - c)).wait_recv()
            _remote(*half(w, c)).wait_send()

    return _Ride(halves, [_sds(a.shape, a.dtype) for a in halves], True,
                 [pltpu.SemaphoreType.DMA((n,))] * 2, first, None, last)


def _gather_small(vec):
    length = vec.shape[1]
    flips = [(fx, fy, fc) for fx in (0, 1) for fy in (0, 1) for fc in (0, 1)][1:]

    def body(v_ref, o_ref, send_sem, recv_sem, local_sem):
        x, y, c = _mesh_position()
        me = 4 * x + 2 * y + c
        local = pltpu.make_async_copy(v_ref, o_ref.at[me], local_sem)
        local.start()
        peers = [(x ^ fx, y ^ fy, c ^ fc) for fx, fy, fc in flips]
        sends = [_remote(v_ref, o_ref.at[me], send_sem.at[k], recv_sem.at[k], peer) for k, peer in enumerate(peers)]
        for cp in sends:
            cp.start()
        for k, (px, py, pc) in enumerate(peers):
            there = o_ref.at[4 * px + 2 * py + pc]
            _remote(there, there, send_sem.at[k], recv_sem.at[k], peers[k]).wait_recv()
        for cp in sends:
            cp.wait_send()
        local.wait()

    return pl.pallas_call(
        body, name="gather_small", in_specs=[ANY], out_specs=ANY, out_shape=_sds((N_DEVICES, 1, length), F32),
        scratch_shapes=[pltpu.SemaphoreType.DMA((N_DEVICES - 1,))] * 2 + [pltpu.SemaphoreType.DMA],
    )(vec)


MATRICES = ("w_in", "w_attn_branch", "w_pool_group", "w_pool_branch", "w_out", "w_up", "w_down", "w_ple", "w_ple_gate")
VECTORS = ("norm_mix_pre", "pool_scale", "norm_mix_post", "norm_ffn_pre", "conv_b", "norm_ffn_post", "norm_ple_post")
WEIGHTS = ("norm_mix_pre", "w_in", "w_attn_branch", "w_pool_group", "pool_scale", "w_pool_branch", "w_out",
           "norm_mix_post", "norm_ffn_pre", "w_up", "conv_w", "conv_b", "w_down", "norm_ffn_post", "w_ple",
           "w_ple_gate", "norm_ple_post")


EARLY = ("w_in",)
LATE = tuple(k for k in MATRICES if k not in EARLY)
RIDES = (("w_ple_gate",), ("w_ple",), ("w_attn_branch", "w_pool_branch"), ("w_out", "w_pool_group"),
         ("w_up",), ("w_down",))
assert sorted(k for names in RIDES for k in names) == sorted(LATE)


def _halves_view(g):
    return g.reshape(N_CHIPS, 2, g.size // (2 * N_CHIPS * g.shape[-1]), g.shape[-1])


def _pair_sums(gmat, names, core):
    flat = [_halves_view(gmat[k]) for k in names]
    return [_add_halves(a, b, core) for a, b in zip(flat, _swap_halves(flat))]


def _local_step(x, p, target, w_in, late_blocks, vec, conv_w, chip, core):
    s, d = x.shape
    groups = len(POOL_WINDOWS)
    aw = w_in.shape[2] // 2
    assert w_in.shape[2] == d

    h1 = _rms_fwd(x, vec["norm_mix_pre"])
    full = {}

    def carrying(names):
        return _gather_ride([late_blocks[k] for k in names])

    def carried(names, outs):
        full.update(zip(names, outs))

    qk, got = _mm_nn("proj_qk", h1, w_in, BF16, j0=0, nj=1, ride=carrying(RIDES[0]))
    carried(RIDES[0], got)
    v = _mm_nn("proj_v", h1, w_in, BF16, j0=1, c0=0, cn=aw)
    u, got = _mm_nn("proj_u", h1, w_in, F32, j0=1, c0=aw, cn=aw, ride=carrying(RIDES[1]))
    carried(RIDES[1], got)
    ga, got = _mm_nn("proj_ga", h1, w_in, BF16, j0=2, nj=1, ride=carrying(RIDES[2]))
    carried(RIDES[2], got)
    gp, got = _mm_nn("proj_gp", h1, w_in, BF16, j0=3, nj=1, ride=carrying(RIDES[3]))
    carried(RIDES[3], got)
    attn, lsum, got = _attn_fwd(qk, v, carrying(RIDES[4]))
    carried(RIDES[4], got)
    w_ab, w_pb, w_up, w_ple = (full[k] for k in ("w_attn_branch", "w_pool_branch", "w_up", "w_ple"))
    w_pg = full["w_pool_group"].transpose(1, 0, 2, 3)
    w_pg = w_pg.reshape(groups, -1, w_pg.shape[-1])
    as_rows = lambda k: full[k].reshape(1, -1, full[k].shape[-1])
    w_out, w_pleg = as_rows("w_out"), as_rows("w_ple_gate")
    assert w_pb.shape[1] == aw == w_ab.shape[1]
    ya = _mm_nn("attn_branch", attn, w_ab, BF16)
    pooled = _pool_fwd(u)
    pg = _group_nn("pool_group", pooled, w_pg, F32)
    ps = _scale_cols(pg, vec["pool_scale"])
    yp = _mm_nn("pool_branch", ps, w_pb, BF16)
    mixed = _mix_fwd(ga, gp, ya, yp)
    mo = _mm_nn("mix_out", mixed, w_out, F32)
    x2, h2 = _post_pre(x, mo, vec["norm_mix_post"], vec["norm_ffn_pre"])
    up, got = _mm_nn("ffn_up", h2, w_up, F32, tn=1408, ride=carrying(RIDES[5]))
    carried(RIDES[5], got)
    w_down = as_rows("w_down")
    act = _conv_gelu_fwd(up, conv_w, vec["conv_b"])
    yf = _mm_nn("ffn_down", act, w_down, F32, tk=2816)
    x3 = _post(x2, yf, vec["norm_ffn_post"])
    e = _mm_nn("ple_embed", p, w_ple, F32)
    t = _mm_nn("ple_gate", x3, w_pleg, F32)
    dx4, dt, de, g_ple_post, loss_rows = _final(x3, t, e, vec["norm_ple_post"], target)

    gvec, gmat = {}, {}
    gvec["norm_ple_post"] = g_ple_post
    gmat["w_ple"] = _mm_tn("d_w_ple", p, de, N_CHIPS, BF16)
    gmat["w_ple_gate"] = _mm_tn("d_w_ple_gate", x3, dt, 1, BF16)
    dx3 = _mm_nt("d_x3", dt, w_pleg, F32, add=dx4)
    dyf, gvec["norm_ffn_post"] = _bwd_post(dx3, yf, vec["norm_ffn_post"])
    gmat["w_down"] = _mm_tn("d_w_down", act, dyf, 1, BF16)
    dact = _mm_nt("d_act", dyf, w_down, BF16, tn=1408)
    dup_g, dup_v, dcw_g, dcw_v, dcb_g, dcb_v = _conv_gelu_bwd(up, dact, conv_w, vec["conv_b"])
    dup = _concat_cols("concat_dup", (dup_g, dup_v))
    gconv_w = jnp.concatenate([dcw_g, dcw_v], axis=1)
    gvec["conv_b"] = jnp.concatenate([dcb_g, dcb_v], axis=1)
    gmat["w_up"] = _mm_tn("d_w_up", h2, dup, N_CHIPS, BF16, tn=1408)
    dh2 = _mm_nt("d_h2", dup, w_up, F32, tk=2816)
    dx2, dmo, gvec["norm_ffn_pre"], gvec["norm_mix_post"] = _bwd_mid(
        dx3, dh2, x2, vec["norm_ffn_pre"], mo, vec["norm_mix_post"])
    gmat["w_out"] = _mm_tn("d_w_out", mixed, dmo, 1, BF16)
    dmixed = _mm_nt("d_mixed", dmo, w_out, F32)
    dya, dyp, dga, dgp = _bwd_mix(dmixed, ga, gp, ya, yp)
    gmat["w_attn_branch"] = _mm_tn("d_w_attn_branch", attn, dya, N_CHIPS, BF16)
    dattn = _mm_nt("d_attn", dya, w_ab, BF16)
    gmat["w_pool_branch"] = _mm_tn("d_w_pool_branch", ps, dyp, N_CHIPS, BF16)
    dps = _mm_nt("d_ps", dyp, w_pb, F32)
    dpg, gvec["pool_scale"] = _bwd_pool_scale(dps, pg, vec["pool_scale"])
    g_pg = _group_tn("d_w_pool_group", pooled, dpg, groups, BF16)
    gmat["w_pool_group"] = g_pg.reshape(groups, N_CHIPS, -1, g_pg.shape[-1]).transpose(1, 0, 2, 3)
    dpooled = _group_nn("d_pooled", dpg, w_pg, F32, transpose_w=True)
    du = _pool_bwd(dpooled)
    pair = dict(zip(LATE, _pair_sums(gmat, LATE, core)))
    dq, dk, dv, others = _attn_bwd(qk, v, lsum, dattn, _scatter_ride([pair[k] for k in LATE]))
    halves = [_sum_parts(pair[k], part, chip, core) for k, part in zip(LATE, others)]
    dproj = _concat_cols("concat_dproj", (dq, dk, dv, du, dga, dgp))
    gmat["w_in"], joined = _mm_tn("d_w_in", h1, dproj, N_CHIPS, BF16, ride=_join_ride(halves))
    reduced = dict(zip(LATE, joined))
    early = _pair_sums(gmat, EARLY, core)
    dh1, got = _mm_nt("d_h1", dproj, w_in, F32, ride=_scatter_ride(early))
    halves = [_sum_parts(a, part, chip, core) for a, part in zip(early, got)]
    reduced.update(zip(EARLY, _run_ride("join_halves", _join_ride(halves))))
    grad_x, gvec["norm_mix_pre"] = _bwd_first(dx2, dh1, x, vec["norm_mix_pre"])
    return loss_rows[0, 0], grad_x, reduced, gvec, gconv_w


def kernel(x, p, norm_mix_pre, w_in, w_attn_branch, w_pool_group, pool_scale, w_pool_branch, w_out, norm_mix_post, norm_ffn_pre, w_up, conv_w, conv_b, w_down, norm_ffn_post, w_ple, w_ple_gate, norm_ple_post, loss_target, m_norm_mix_pre, m_w_in, m_w_attn_branch, m_w_pool_group, m_pool_scale, m_w_pool_branch, m_w_out, m_norm_mix_post, m_norm_ffn_pre, m_w_up, m_conv_w, m_conv_b, m_w_down, m_norm_ffn_post, m_w_ple, m_w_ple_gate, m_norm_ple_post, v_norm_mix_pre, v_w_in, v_w_attn_branch, v_w_pool_group, v_pool_scale, v_w_pool_branch, v_w_out, v_norm_mix_post, v_norm_ffn_pre, v_w_up, v_conv_w, v_conv_b, v_w_down, v_norm_ffn_post, v_w_ple, v_w_ple_gate, v_norm_ple_post):
    given = dict(locals())
    w = {k: given[k][0] for k in WEIGHTS}
    m = {k: given["m_" + k][0] for k in WEIGHTS}
    v = {k: given["v_" + k][0] for k in WEIGHTS}
    chip = 2 * lax.axis_index("x") + lax.axis_index("y")

    core = lax.axis_index("c")
    w_in_full = _run_ride("gather_weights", _gather_ride([_place_cast(w[k], chip) for k in EARLY]))[0]
    late_blocks = {k: _place_cast(w[k], chip) for k in LATE}
    vec = {k: w[k].reshape(1, -1) for k in VECTORS}
    taps = _gather_small(w["conv_w"].reshape(1, -1))[:, 0][0::2]
    f2q = w["conv_w"].shape[1]
    conv_w_full = taps.reshape(N_CHIPS, 3, f2q).transpose(1, 0, 2).reshape(3, N_CHIPS * f2q)

    loss_rows, grad_x, reduced, gvec, gconv_w = _local_step(
        x[0], p[0, 0], loss_target[0], w_in_full, late_blocks, vec, conv_w_full, chip, core)
    loss = lax.psum(loss_rows, ("x", "y", "c"))

    gw = {k: reduced[k].reshape(w[k].shape) for k in MATRICES}
    sizes = [gvec[k].shape[1] for k in VECTORS]
    small = jnp.concatenate([gvec[k] for k in VECTORS] + [gconv_w.reshape(1, -1)], axis=1)
    small = _sum_leading(_gather_small(small), "sum_small")
    offset = 0
    for k, n in zip(VECTORS, sizes):
        gw[k] = small[0, offset:offset + n]
        offset += n
    gconv_w = small[0, offset:].reshape(3, N_CHIPS, f2q)
    gw["conv_w"] = lax.dynamic_index_in_dim(gconv_w, chip, axis=1, keepdims=False)

    delta, new_m, new_v = {}, {}, {}
    for k in MATRICES:
        gw[k], delta[k], new_m[k], new_v[k] = _adamw(w[k], gw[k], m[k], v[k])
    tiny = VECTORS + ("conv_w",)
    pack = lambda tree: jnp.concatenate([tree[k].reshape(1, -1) for k in tiny], axis=1)
    _, d_s, m_s, v_s = _adamw(pack(w), pack(gw), pack(m), pack(v))
    offset = 0
    for k in tiny:
        n = w[k].size
        delta[k], new_m[k], new_v[k] = (a[0, offset:offset + n].reshape(w[k].shape) for a in (d_s, m_s, v_s))
        offset += n

    lead = lambda tree: [tree[k].reshape((1,) + w[k].shape) for k in WEIGHTS]
    return (loss, grad_x[None], *lead(gw), *lead(delta), *lead(new_m), *lead(new_v))
```

```python
import functools

import jax
import jax.numpy as jnp
from jax import lax
from jax.experimental import pallas as pl
from jax.experimental.pallas import tpu as pltpu

F32 = jnp.float32
BF16 = jnp.bfloat16

HEAD_DIM = 128
POOL_WINDOWS = (2, 4, 8, 16)
EPS = 1e-6
GELU_C = 0.7978845608028654
GELU_A = 0.044715

ADAM_LR = 0.001
ADAM_B1 = 0.9
ADAM_B2 = 0.999
ADAM_EPS = 1e-08
ADAM_WD = 0.01
ADAM_STEP = 10

N_CHIPS = 4
N_DEVICES = 8
VMEM_LIMIT_BYTES = 52 * 1024 * 1024
ATTN_BLOCK = 256
ATTN_Q_BLOCK = 2048
ATTN_HEADS = 1
POOL_TILE = 128
RIDE_MID_STEPS = 2
DMA_PIECE_BYTES = 512 * 1024
ROWS_PER_TILE = 16
MESH = pl.DeviceIdType.MESH
ANY = pl.BlockSpec(memory_space=pl.ANY)


def _sds(shape, dtype):
    return jax.ShapeDtypeStruct(tuple(shape), dtype)


def _tile(dim, pref, mult=128):
    t = min(pref, dim) // mult * mult
    while t >= mult:
        if dim % t == 0:
            return t
        t -= mult
    return dim


def _params(*semantics):
    return pltpu.CompilerParams(dimension_semantics=semantics or None, vmem_limit_bytes=VMEM_LIMIT_BYTES)


def _sigmoid(v):
    return 1.0 / (1.0 + jnp.exp(-v))


def _rstd(v):
    return lax.rsqrt(jnp.mean(v * v, axis=-1, keepdims=True) + EPS)


def _rms_bwd(dy, gain, v):
    r = _rstd(v)
    vh = v * r
    gy = dy * gain
    return r * (gy - vh * jnp.mean(gy * vh, axis=-1, keepdims=True)), dy * vh


def _ride_along(ride, grid, n_in, n_out, own_semantics):
    if ride is None:
        return [], [], [], {}, [], own_semantics, lambda refs: (lambda: None, lambda: None)
    n = len(ride.operands)
    steps = [g - 1 for g in grid]

    def at(step):
        hit = pl.program_id(0) == step[0]
        for axis in range(1, len(grid)):
            hit = jnp.logical_and(hit, pl.program_id(axis) == step[axis])
        return hit

    def hooks(refs):
        ins, outs = refs[n_in:n_in + n], refs[n_in + n + n_out:n_in + 2 * n + n_out]
        sems = refs[len(refs) - len(ride.scratch_shapes):]

        def before():
            pl.when(at([0] * len(grid)))(lambda: ride.first(ins, outs, sems))
            if ride.mid is not None:
                pl.when(at(steps[:-1] + [max(steps[-1] - RIDE_MID_STEPS, 0)]))(lambda: ride.mid(ins, outs, sems))

        def after():
            pl.when(at(steps))(lambda: ride.last(ins, outs, sems))

        return before, after

    return (ride.operands, [ANY] * n, ride.out_shapes, ride.aliases(n_in, n_out), ride.scratch_shapes,
            ("arbitrary",) * len(grid), hooks)


def _matmul(name, a, b, grid, a_spec, b_spec, o_spec, out, dims, acc_shape, add=None, add_spec=None, ride=None):
    nk = grid[2]
    n_in = 2 if add is None else 3
    extra, extra_specs, extra_out, aliases, ride_scratch, semantics, hooks = _ride_along(
        ride, grid, n_in, 1, ("parallel", "parallel", "arbitrary"))

    def body(*refs):
        before, after = hooks(refs)
        before()
        compute(refs)
        after()

    def compute(refs):
        a_ref, b_ref = refs[:2]
        c_ref = None if add is None else refs[2]
        o_ref = refs[n_in + len(extra)]
        part = lax.dot_general(a_ref[...].astype(BF16), b_ref[...].astype(BF16), (dims, ((), ())),
                               preferred_element_type=F32)

        def finish(r):
            if c_ref is not None:
                r = r + c_ref[...]
            o_ref[...] = r.astype(o_ref.dtype)

        if nk == 1:
            finish(part)
            return
        acc = refs[n_in + 2 * len(extra) + 1]
        kk = pl.program_id(2)

        @pl.when(kk == 0)
        def _():
            acc[...] = part

        @pl.when(jnp.logical_and(kk > 0, kk < nk - 1))
        def _():
            acc[...] += part

        @pl.when(kk == nk - 1)
        def _():
            finish(acc[...] + part)

    operands = (a, b) if add is None else (a, b, add)
    in_specs = [a_spec, b_spec] if add is None else [a_spec, b_spec, add_spec]
    outs = pl.pallas_call(
        body, name=name, grid=grid, in_specs=in_specs + extra_specs, out_specs=(o_spec, *extra_specs),
        out_shape=(out, *extra_out), input_output_aliases=aliases,
        scratch_shapes=([] if nk == 1 else [pltpu.VMEM(acc_shape, F32)]) + ride_scratch,
        compiler_params=_params(*semantics),
    )(*operands, *extra)
    return outs[0] if ride is None else (outs[0], list(outs[1:]))


def _mm_nn(name, a, b3, out_dtype, j0=0, nj=None, c0=0, cn=None, tm=1024, tn=1024, tk=2048, ride=None):
    m, k = a.shape
    nj_all, kb, width = b3.shape
    assert kb == k
    if cn is None:
        cn = width
        nj = nj_all - j0 if nj is None else nj
    else:
        nj = 1
    n = nj * cn
    tm, tn, tk = _tile(m, tm, 8), _tile(cn, tn), _tile(k, tk)
    assert c0 % tn == 0
    nb, cb = cn // tn, c0 // tn
    return _matmul(
        name, a, b3, (m // tm, n // tn, k // tk),
        pl.BlockSpec((tm, tk), lambda i, j, kk: (i, kk)),
        pl.BlockSpec((None, tk, tn), lambda i, j, kk: (j0 + j // nb, kk, cb + j % nb)),
        pl.BlockSpec((tm, tn), lambda i, j, kk: (i, j)),
        _sds((m, n), out_dtype), ((1,), (0,)), (tm, tn), ride=ride)


def _mm_nt(name, a, b3, out_dtype, add=None, tm=1024, tn=1024, tk=2048, ride=None):
    m, kc = a.shape
    nj, n, kj = b3.shape
    assert kc == nj * kj
    tm, tn, tk = _tile(m, tm, 8), _tile(n, tn), _tile(kj, tk)
    kb = kj // tk
    o_spec = pl.BlockSpec((tm, tn), lambda i, j, kk: (i, j))
    return _matmul(
        name, a, b3, (m // tm, n // tn, kc // tk),
        pl.BlockSpec((tm, tk), lambda i, j, kk: (i, kk)),
        pl.BlockSpec((None, tn, tk), lambda i, j, kk: (kk // kb, j, kk % kb)),
        o_spec, _sds((m, n), out_dtype), ((1,), (1,)), (tm, tn), add=add, add_spec=o_spec, ride=ride)


def _mm_tn(name, a, b, nj, out_dtype, tm=512, tn=1024, ts=4096, ride=None):
    s, m = a.shape
    s2, n = b.shape
    assert s == s2 and n % nj == 0
    width = n // nj
    tm, tn, ts = _tile(m, tm), _tile(width, tn), _tile(s, ts)
    nb = width // tn
    return _matmul(
        name, a, b, (m // tm, n // tn, s // ts),
        pl.BlockSpec((ts, tm), lambda i, j, kk: (kk, i)),
        pl.BlockSpec((ts, tn), lambda i, j, kk: (kk, j)),
        pl.BlockSpec((None, tm, tn), lambda i, j, kk: (j // nb, i, j % nb)),
        _sds((nj, m, width), out_dtype), ((0,), (0,)), (tm, tn), ride=ride)


def _group_nn(name, a, w, out_dtype, transpose_w=False, tm=1024):
    s, gc = a.shape
    g, c, _ = w.shape
    tm = _tile(s, tm, 8)
    return _matmul(
        name, a, w, (s // tm, g, 1),
        pl.BlockSpec((tm, c), lambda i, j, kk: (i, j)),
        pl.BlockSpec((None, c, c), lambda i, j, kk: (j, 0, 0)),
        pl.BlockSpec((tm, c), lambda i, j, kk: (i, j)),
        _sds((s, gc), out_dtype), ((1,), (1,)) if transpose_w else ((1,), (0,)), (tm, c))


def _group_tn(name, a, b, g, out_dtype, ts=4096):
    s, gc = a.shape
    c = gc // g
    ts = _tile(s, ts, 8)
    return _matmul(
        name, a, b, (g, 1, s // ts),
        pl.BlockSpec((ts, c), lambda i, j, kk: (kk, i)),
        pl.BlockSpec((ts, c), lambda i, j, kk: (kk, i)),
        pl.BlockSpec((None, c, c), lambda i, j, kk: (i, 0, 0)),
        _sds((g, c, c), out_dtype), ((0,), (0,)), (c, c))


def _row(tm, d):
    return pl.BlockSpec((tm, d), lambda i: (i, 0))


def _vec(d):
    return pl.BlockSpec((1, d), lambda i: (0, 0))


def _rows_call(name, body, ins, in_specs, outs, out_specs, steps, accumulates):
    return pl.pallas_call(
        body, name=name, grid=(steps,), in_specs=in_specs, out_specs=out_specs, out_shape=outs,
        compiler_params=_params("arbitrary" if accumulates else "parallel"),
    )(*ins)


def _accumulate(ref, value):
    @pl.when(pl.program_id(0) == 0)
    def _():
        ref[...] = value

    @pl.when(pl.program_id(0) > 0)
    def _():
        ref[...] += value


def _colsum(v):
    return jnp.sum(v, axis=0, keepdims=True)


def _rms_fwd(x, gain, tm=256):
    s, d = x.shape

    def body(x_ref, g_ref, h_ref):
        v = x_ref[...]
        h_ref[...] = (v * _rstd(v) * g_ref[...]).astype(BF16)

    return _rows_call("rms_fwd", body, (x, gain), [_row(tm, d), _vec(d)], _sds((s, d), BF16), _row(tm, d),
                      s // tm, False)


def _mix_fwd(ga, gp, ya, yp, tm=256):
    s, d = ga.shape

    def body(ga_ref, gp_ref, ya_ref, yp_ref, o_ref):
        ga, gp, ya, yp = (ref[...].astype(F32) for ref in (ga_ref, gp_ref, ya_ref, yp_ref))
        o_ref[...] = (_sigmoid(ga) * ya + _sigmoid(gp) * yp).astype(BF16)

    return _rows_call("mix_fwd", body, (ga, gp, ya, yp), [_row(tm, d)] * 4, _sds((s, d), BF16), _row(tm, d),
                      s // tm, False)


def _scale_cols(v, scale, tm=256):
    s, d = v.shape

    def body(v_ref, s_ref, o_ref):
        o_ref[...] = (v_ref[...] * s_ref[...]).astype(BF16)

    return _rows_call("pool_scale_fwd", body, (v, scale), [_row(tm, d), _vec(d)], _sds((s, d), BF16), _row(tm, d),
                      s // tm, False)


def _post_pre(x, y, gain_post, gain_pre, tm=256):
    s, d = x.shape

    def body(x_ref, y_ref, gp_ref, gn_ref, x2_ref, h_ref):
        y = y_ref[...]
        x2 = x_ref[...] + y * _rstd(y) * gp_ref[...]
        x2_ref[...] = x2
        h_ref[...] = (x2 * _rstd(x2) * gn_ref[...]).astype(BF16)

    return _rows_call("post_pre", body, (x, y, gain_post, gain_pre), [_row(tm, d), _row(tm, d), _vec(d), _vec(d)],
                      (_sds((s, d), F32), _sds((s, d), BF16)), (_row(tm, d), _row(tm, d)), s // tm, False)


def _post(x, y, gain_post, tm=256):
    s, d = x.shape

    def body(x_ref, y_ref, gp_ref, o_ref):
        y = y_ref[...]
        o_ref[...] = x_ref[...] + y * _rstd(y) * gp_ref[...]

    return _rows_call("post", body, (x, y, gain_post), [_row(tm, d), _row(tm, d), _vec(d)], _sds((s, d), F32),
                      _row(tm, d), s // tm, False)


def _final(x3, t, e, gain, target, tm=256):
    s, d = x3.shape

    def body(x_ref, t_ref, e_ref, g_ref, y_ref, dx_ref, dt_ref, de_ref, dg_ref, loss_ref):
        sg = _sigmoid(t_ref[...])
        ev = e_ref[...]
        pe = sg * ev
        diff = x_ref[...] + pe * _rstd(pe) * g_ref[...] - y_ref[...]
        dx = diff * (1.0 / d)
        dx_ref[...] = dx
        part = 0.5 * jnp.sum(jnp.mean(diff * diff, axis=-1, keepdims=True), axis=0, keepdims=True)
        _accumulate(loss_ref, jnp.broadcast_to(part, loss_ref.shape))
        dpe, dgain = _rms_bwd(dx, g_ref[...], pe)
        de_ref[...] = (dpe * sg).astype(BF16)
        dt_ref[...] = (dpe * ev * sg * (1.0 - sg)).astype(BF16)
        _accumulate(dg_ref, _colsum(dgain))

    return _rows_call("final", body, (x3, t, e, gain, target),
                      [_row(tm, d), _row(tm, d), _row(tm, d), _vec(d), _row(tm, d)],
                      (_sds((s, d), F32), _sds((s, d), BF16), _sds((s, d), BF16), _sds((1, d), F32),
                       _sds((8, 128), F32)),
                      (_row(tm, d), _row(tm, d), _row(tm, d), _vec(d), pl.BlockSpec((8, 128), lambda i: (0, 0))),
                      s // tm, True)


def _concat_cols(name, parts, tm=256):
    s = parts[0].shape[0]
    widths = [a.shape[1] for a in parts]

    def body(*refs):
        o_ref, offset = refs[-1], 0
        for ref, width in zip(refs[:-1], widths):
            o_ref[:, offset:offset + width] = ref[...].astype(BF16)
            offset += width

    return _rows_call(name, body, parts, [_row(tm, width) for width in widths], _sds((s, sum(widths)), BF16),
                      _row(tm, sum(widths)), s // tm, False)


def _bwd_post(dx, y, gain, tm=256):
    s, d = dx.shape

    def body(dx_ref, y_ref, g_ref, dy_ref, dg_ref):
        dy, dgain = _rms_bwd(dx_ref[...], g_ref[...], y_ref[...])
        dy_ref[...] = dy.astype(BF16)
        _accumulate(dg_ref, _colsum(dgain))

    return _rows_call("bwd_post", body, (dx, y, gain), [_row(tm, d), _row(tm, d), _vec(d)],
                      (_sds((s, d), BF16), _sds((1, d), F32)), (_row(tm, d), _vec(d)), s // tm, True)


def _bwd_mid(dx3, dh2, x2, gain_pre, mo, gain_post, tm=128):
    s, d = dx3.shape

    def body(dx3_ref, dh_ref, x2_ref, gn_ref, mo_ref, gp_ref, dx2_ref, dmo_ref, dgn_ref, dgp_ref):
        dv, dgn = _rms_bwd(dh_ref[...], gn_ref[...], x2_ref[...])
        dx2 = dx3_ref[...] + dv
        dx2_ref[...] = dx2
        dmo, dgp = _rms_bwd(dx2, gp_ref[...], mo_ref[...])
        dmo_ref[...] = dmo.astype(BF16)
        _accumulate(dgn_ref, _colsum(dgn))
        _accumulate(dgp_ref, _colsum(dgp))

    return _rows_call("bwd_mid", body, (dx3, dh2, x2, gain_pre, mo, gain_post),
                      [_row(tm, d), _row(tm, d), _row(tm, d), _vec(d), _row(tm, d), _vec(d)],
                      (_sds((s, d), F32), _sds((s, d), BF16), _sds((1, d), F32), _sds((1, d), F32)),
                      (_row(tm, d), _row(tm, d), _vec(d), _vec(d)), s // tm, True)


def _bwd_first(dx2, dh1, x, gain, tm=256):
    s, d = dx2.shape

    def body(dx2_ref, dh_ref, x_ref, g_ref, dx_ref, dg_ref):
        dv, dgain = _rms_bwd(dh_ref[...], g_ref[...], x_ref[...])
        dx_ref[...] = dx2_ref[...] + dv
        _accumulate(dg_ref, _colsum(dgain))

    return _rows_call("bwd_first", body, (dx2, dh1, x, gain), [_row(tm, d), _row(tm, d), _row(tm, d), _vec(d)],
                      (_sds((s, d), F32), _sds((1, d), F32)), (_row(tm, d), _vec(d)), s // tm, True)


def _bwd_mix(dmixed, ga, gp, ya, yp, tm=256):
    s, d = dmixed.shape

    def body(dm_ref, ga_ref, gp_ref, ya_ref, yp_ref, dya_ref, dyp_ref, dga_ref, dgp_ref):
        dm = dm_ref[...]
        ga, gp, ya, yp = (ref[...].astype(F32) for ref in (ga_ref, gp_ref, ya_ref, yp_ref))
        sa, sp = _sigmoid(ga), _sigmoid(gp)
        dya_ref[...] = (dm * sa).astype(BF16)
        dyp_ref[...] = (dm * sp).astype(BF16)
        dga_ref[...] = (dm * ya * sa * (1.0 - sa)).astype(BF16)
        dgp_ref[...] = (dm * yp * sp * (1.0 - sp)).astype(BF16)

    return _rows_call("bwd_mix", body, (dmixed, ga, gp, ya, yp), [_row(tm, d)] * 5,
                      (_sds((s, d), BF16),) * 4, (_row(tm, d),) * 4, s // tm, False)


def _bwd_pool_scale(dps, pg, scale, tm=256):
    s, d = dps.shape

    def body(d_ref, pg_ref, s_ref, dpg_ref, ds_ref):
        dv = d_ref[...]
        dpg_ref[...] = (dv * s_ref[...]).astype(BF16)
        _accumulate(ds_ref, _colsum(dv * pg_ref[...]))

    return _rows_call("bwd_pool_scale", body, (dps, pg, scale), [_row(tm, d), _row(tm, d), _vec(d)],
                      (_sds((s, d), BF16), _sds((1, d), F32)), (_row(tm, d), _vec(d)), s // tm, True)


def _shift_down(v, k, rows):
    return jnp.where(rows >= k, pltpu.roll(v, k, 0), 0.0)


def _shift_up(v, k, rows):
    s = v.shape[0]
    return jnp.where(rows < s - k, pltpu.roll(v, s - k, 0), 0.0)


def _window_pick(group, sums, rows):
    total = sums[-1]
    width = jnp.full((), POOL_WINDOWS[-1], jnp.int32)
    for g in range(len(POOL_WINDOWS) - 2, -1, -1):
        total = jnp.where(group == g, sums[g], total)
        width = jnp.where(group == g, POOL_WINDOWS[g], width)
    return total, jnp.minimum(rows + 1, width).astype(F32)


def _doubling(v, shift, rows):
    sums, k = [], 1
    for _ in POOL_WINDOWS:
        v = v + shift(v, k, rows)
        sums.append(v)
        k *= 2
    return sums


def _pool_fwd(u):
    s, width = u.shape
    per_group = width // len(POOL_WINDOWS) // POOL_TILE

    def body(u_ref, o_ref):
        v = u_ref[...]
        rows = lax.broadcasted_iota(jnp.int32, (s, 1), 0)
        total, count = _window_pick(pl.program_id(0), _doubling(v, _shift_down, rows), rows)
        o_ref[...] = (total / count - v).astype(BF16)

    spec = pl.BlockSpec((s, POOL_TILE), lambda g, j: (0, g * per_group + j))
    return pl.pallas_call(body, name="pool_fwd", grid=(len(POOL_WINDOWS), per_group), in_specs=[spec], out_specs=spec,
                          out_shape=_sds((s, width), BF16), compiler_params=_params("parallel", "parallel"))(u)


def _pool_bwd(dpooled):
    s, width = dpooled.shape
    per_group = width // len(POOL_WINDOWS) // POOL_TILE

    def body(d_ref, o_ref):
        dv = d_ref[...]
        rows = lax.broadcasted_iota(jnp.int32, (s, 1), 0)
        group = pl.program_id(0)
        _, count = _window_pick(group, [dv] * len(POOL_WINDOWS), rows)
        total, _ = _window_pick(group, _doubling(dv / count, _shift_up, rows), rows)
        o_ref[...] = (total - dv).astype(BF16)

    spec = pl.BlockSpec((s, POOL_TILE), lambda g, j: (0, g * per_group + j))
    return pl.pallas_call(body, name="pool_bwd", grid=(len(POOL_WINDOWS), per_group), in_specs=[spec], out_specs=spec,
                          out_shape=_sds((s, width), BF16), compiler_params=_params("parallel", "parallel"))(dpooled)


def _conv(v, w_ref, b_ref, rows):
    out = b_ref[...] + _shift_down(v, 2, rows) * w_ref[0:1, :]
    out = out + _shift_down(v, 1, rows) * w_ref[1:2, :]
    return out + v * w_ref[2:3, :]


def _gelu_parts(v):
    th = jnp.tanh(GELU_C * (v + GELU_A * v * v * v))
    return th, 0.5 * v * (1.0 + th)


def _conv_gelu_fwd(up, conv_w, conv_b, tc=128):
    s, f2 = up.shape
    f = f2 // 2
    nb = f // tc

    def body(g_ref, v_ref, wg_ref, wv_ref, bg_ref, bv_ref, o_ref, cg_ref, cv_ref):
        rows = lax.broadcasted_iota(jnp.int32, (s, 1), 0)
        cg = _conv(g_ref[...], wg_ref, bg_ref, rows)
        cv = _conv(v_ref[...], wv_ref, bv_ref, rows)
        _, gl = _gelu_parts(cg)
        o_ref[...] = (gl * cv).astype(BF16)
        cg_ref[...] = cg.astype(BF16)
        cv_ref[...] = cv.astype(BF16)

    lo = lambda r: pl.BlockSpec((r, tc), lambda j: (0, j))
    hi = lambda r: pl.BlockSpec((r, tc), lambda j: (0, j + nb))
    return pl.pallas_call(
        body, name="conv_gelu_fwd", grid=(nb,), in_specs=[lo(s), hi(s), lo(3), hi(3), lo(1), hi(1)],
        out_specs=(lo(s),) * 3, out_shape=(_sds((s, f), BF16),) * 3,
        compiler_params=_params("parallel"))(up, up, conv_w, conv_w, conv_b, conv_b)


def _conv_gelu_bwd(up, cg, cv, dact, conv_w, tc=128):
    s, f2 = up.shape
    f = f2 // 2
    nb = f // tc

    def body(g_ref, v_ref, cg_ref, cv_ref, d_ref, wg_ref, wv_ref, dg_ref, dv_ref, dwg_ref, dwv_ref, dbg_ref, dbv_ref):
        rows = lax.broadcasted_iota(jnp.int32, (s, 1), 0)
        dact = d_ref[...].astype(F32)

        def back(pre, dc, w_ref, dpre_ref, dw_ref, db_ref):
            dc1, dc2 = _shift_up(dc, 1, rows), _shift_up(dc, 2, rows)
            db_ref[...] = _colsum(dc)
            dw_ref[0:1, :] = _colsum(dc2 * pre)
            dw_ref[1:2, :] = _colsum(dc1 * pre)
            dw_ref[2:3, :] = _colsum(dc * pre)
            dpre_ref[...] = (dc * w_ref[2:3, :] + dc1 * w_ref[1:2, :] + dc2 * w_ref[0:1, :]).astype(BF16)

        gate, val = g_ref[...], v_ref[...]
        cg, cv = cg_ref[...].astype(F32), cv_ref[...].astype(F32)
        th, gl = _gelu_parts(cg)
        dgl = 0.5 * (1.0 + th) + 0.5 * cg * (1.0 - th * th) * GELU_C * (1.0 + 3.0 * GELU_A * cg * cg)
        back(val, dact * gl, wv_ref, dv_ref, dwv_ref, dbv_ref)
        back(gate, dact * cv * dgl, wg_ref, dg_ref, dwg_ref, dbg_ref)

    lo = lambda r: pl.BlockSpec((r, tc), lambda j: (0, j))
    hi = lambda r: pl.BlockSpec((r, tc), lambda j: (0, j + nb))
    return pl.pallas_call(
        body, name="conv_gelu_bwd", grid=(nb,),
        in_specs=[lo(s), hi(s), lo(s), lo(s), lo(s), lo(3), hi(3)],
        out_specs=(lo(s), lo(s), lo(3), lo(3), lo(1), lo(1)),
        out_shape=(_sds((s, f), BF16), _sds((s, f), BF16), _sds((3, f), F32), _sds((3, f), F32),
                   _sds((1, f), F32), _sds((1, f), F32)),
        compiler_params=_params("parallel"))(up, up, cg, cv, dact, conv_w, conv_w)


def _dot(a, b, dims):
    return lax.dot_general(a, b, (dims, ((), ())), preferred_element_type=F32)


def _running_sums(v, tri):
    hi = v.astype(BF16)
    lo = (v - hi.astype(F32)).astype(BF16)
    return _dot(hi, tri, ((1,), (0,))) + _dot(lo, tri, ((1,), (0,)))


def _attn_scores(q, kt, mask):
    z = _dot(q, kt, ((1,), (1,))) * (HEAD_DIM ** -0.5)
    e = jnp.exp(-jnp.abs(z))
    log_1m_beta = -(jnp.maximum(z, 0.0) + jnp.log(1.0 + e))
    return z, e, log_1m_beta if mask is None else jnp.where(mask, log_1m_beta, 0.0)


def _masked(v, mask):
    return v if mask is None else jnp.where(mask, v, 0.0)


def _attn_consts(t):
    rows = lax.broadcasted_iota(jnp.int32, (t, t), 0)
    cols = lax.broadcasted_iota(jnp.int32, (t, t), 1)
    return (rows >= cols).astype(BF16), (rows <= cols).astype(BF16)


def _causal(rows, t):
    return lax.broadcasted_iota(jnp.int32, (rows, t), 1) < lax.broadcasted_iota(jnp.int32, (rows, t), 0)


def _attn_specs(s, tq, heads):
    lanes = ATTN_HEADS * HEAD_DIM
    steps = heads // ATTN_HEADS
    q = pl.BlockSpec((tq, lanes), lambda h, i: (i, h))
    k = pl.BlockSpec((s, lanes), lambda h, i: (0, steps + h))
    whole = pl.BlockSpec((s, lanes), lambda h, i: (0, h))
    rows = pl.BlockSpec((ATTN_HEADS, tq, 1), lambda h, i: (h, i, 0))
    return q, k, whole, rows


def _attn_blocks(s):
    t = min(ATTN_BLOCK, s)
    tq = min(ATTN_Q_BLOCK, s)
    assert tq % t == 0 and s % tq == 0
    return t, tq, tq // t


def _head(p):
    return pl.ds(p * HEAD_DIM, HEAD_DIM)


def _attn_fwd(qk, v, ride=None):
    s, width = v.shape
    heads = width // HEAD_DIM
    t, tq, ratio = _attn_blocks(s)
    assert heads % ATTN_HEADS == 0
    grid = (heads // ATTN_HEADS, s // tq)
    extra, extra_specs, extra_out, aliases, scratch, semantics, hooks = _ride_along(
        ride, grid, 3, 2, ("parallel", "parallel"))

    def body(*refs):
        q_ref, k_ref, v_ref = refs[:3]
        o_ref, lsum_ref = refs[3 + len(extra):5 + len(extra)]
        before, after = hooks(refs)
        before()
        i = pl.program_id(1)
        tri, _ = _attn_consts(t)

        def block(p, rows, start, carry, mask):
            acc, log_rest = carry
            z, _, log_1m_beta = _attn_scores(q_ref[rows, _head(p)], k_ref[pl.ds(start, t), _head(p)], mask)
            total = _running_sums(log_1m_beta, tri) + log_rest
            a = _masked(jnp.exp(z + total), mask)
            return acc + _dot(a.astype(BF16), v_ref[pl.ds(start, t), _head(p)], ((1,), (0,))), total[:, 0:1]

        def own_rows(p):
            carry = (jnp.zeros((tq, HEAD_DIM), F32), jnp.zeros((tq, 1), F32))
            for r in range(ratio - 1, -1, -1):
                below = tuple(val[r * t:] for val in carry)
                below = block(p, pl.ds(r * t, tq - r * t), pl.multiple_of(i * tq + r * t, t), below,
                              _causal(tq - r * t, t))
                carry = tuple(jnp.concatenate([val[:r * t], new], axis=0) if r else new
                              for val, new in zip(carry, below))
            return carry

        def step(n, carry):
            start = pl.multiple_of((i * ratio - 1 - n) * t, t)
            return tuple(block(p, pl.ds(0, tq), start, carry[p], None) for p in range(ATTN_HEADS))

        carry = lax.fori_loop(0, i * ratio, step, tuple(own_rows(p) for p in range(ATTN_HEADS)))
        for p in range(ATTN_HEADS):
            o_ref[:, _head(p)] = carry[p][0].astype(BF16)
            lsum_ref[p] = carry[p][1]
        after()

    q_spec, k_spec, whole, rows = _attn_specs(s, tq, heads)
    outs = pl.pallas_call(
        body, name="attn_fwd", grid=grid, in_specs=[q_spec, k_spec, whole] + extra_specs,
        out_specs=(q_spec, rows, *extra_specs),
        out_shape=(_sds((s, width), BF16), _sds((heads, s, 1), F32), *extra_out),
        input_output_aliases=aliases, scratch_shapes=scratch,
        compiler_params=_params(*semantics))(qk, qk, v, *extra)
    return outs[0], outs[1], list(outs[2:])


def _attn_bwd(qk, v, lsum, do, ride=None):
    s, width = v.shape
    heads = width // HEAD_DIM
    t, tq, ratio = _attn_blocks(s)
    grid = (heads // ATTN_HEADS, s // tq)
    extra, extra_specs, extra_out, aliases, scratch, semantics, hooks = _ride_along(
        ride, grid, 5, 3, ("parallel", "arbitrary"))

    def body(*refs):
        q_ref, k_ref, v_ref, lsum_ref, do_ref = refs[:5]
        dq_ref, dk_ref, dv_ref = refs[5 + len(extra):8 + len(extra)]
        before, after = hooks(refs)
        before()
        i = pl.program_id(1)

        @pl.when(i == 0)
        def _():
            dk_ref[...] = jnp.zeros_like(dk_ref)
            dv_ref[...] = jnp.zeros_like(dv_ref)

        _, triu = _attn_consts(t)

        def block(p, rows, start, carry, mask):
            dq, log_rest, g_before = carry
            q, do_blk = q_ref[rows, _head(p)], do_ref[rows, _head(p)]
            kt, vt = k_ref[pl.ds(start, t), _head(p)], v_ref[pl.ds(start, t), _head(p)]
            z, e, log_1m_beta = _attn_scores(q, kt, mask)
            upto = _running_sums(log_1m_beta, triu)
            a = _masked(jnp.exp(z + log_rest - upto + log_1m_beta), mask)
            g = a * _dot(do_blk, vt, ((1,), (1,)))
            g_upto = _running_sums(g, triu) + g_before
            sig = jnp.where(z >= 0.0, 1.0, e) / (1.0 + e)
            dz = (_masked(g - sig * g_upto, mask) * (HEAD_DIM ** -0.5)).astype(BF16)
            dk_ref[pl.ds(start, t), _head(p)] += _dot(dz, q, ((0,), (0,)))
            dv_ref[pl.ds(start, t), _head(p)] += _dot(a.astype(BF16), do_blk, ((0,), (0,)))
            return dq + _dot(dz, kt, ((1,), (0,))), log_rest - upto[:, t - 1:t], g_upto[:, t - 1:t]

        def step(n, carry):
            start = pl.multiple_of(n * t, t)
            return tuple(block(p, pl.ds(0, tq), start, carry[p], None) for p in range(ATTN_HEADS))

        carry = tuple((jnp.zeros((tq, HEAD_DIM), F32), lsum_ref[p], jnp.zeros((tq, 1), F32))
                      for p in range(ATTN_HEADS))
        carry = lax.fori_loop(0, i * ratio, step, carry)
        for p in range(ATTN_HEADS):
            state = carry[p]
            for r in range(ratio):
                below = tuple(val[r * t:] for val in state)
                below = block(p, pl.ds(r * t, tq - r * t), pl.multiple_of(i * tq + r * t, t), below,
                              _causal(tq - r * t, t))
                state = tuple(jnp.concatenate([val[:r * t], new], axis=0) if r else new
                              for val, new in zip(state, below))
            dq_ref[:, _head(p)] = state[0].astype(BF16)
        after()

    q_spec, k_spec, whole, rows = _attn_specs(s, tq, heads)
    outs = pl.pallas_call(
        body, name="attn_bwd", grid=grid, in_specs=[q_spec, k_spec, whole, rows, q_spec] + extra_specs,
        out_specs=(q_spec, whole, whole, *extra_specs),
        out_shape=(_sds((s, width), BF16), _sds((s, width), F32), _sds((s, width), F32), *extra_out),
        input_output_aliases=aliases, scratch_shapes=scratch,
        compiler_params=_params(*semantics))(qk, qk, v, lsum, do, *extra)
    return outs[0], outs[1], outs[2], list(outs[3:])


def _as_rows(a):
    return a.reshape(-1, a.shape[-1])


def _flat_call(name, body, ins, out_dtypes, block_bytes=1 << 20):
    shape = ins[0].shape
    rows, cols = _as_rows(ins[0]).shape
    tr = _tile(rows, max(8, block_bytes // (4 * cols)), 8)
    spec = pl.BlockSpec((tr, cols), lambda i: (i, 0))
    outs = pl.pallas_call(
        body, name=name, grid=(rows // tr,), in_specs=[spec] * len(ins), out_specs=tuple([spec] * len(out_dtypes)),
        out_shape=tuple(_sds((rows, cols), dt) for dt in out_dtypes), compiler_params=_params("parallel"),
    )(*[_as_rows(a) for a in ins])
    return [o.reshape(shape) for o in outs]


def _scalars(*values):
    return jnp.stack([jnp.asarray(v, jnp.int32) for v in values])


def _row_tile(rows, cols):
    return _tile(rows, max(ROWS_PER_TILE, (1 << 22) // (4 * cols)), ROWS_PER_TILE)


def _place_cast(w, chip):
    cols = w.shape[-1]
    rows = w.size // cols
    tr = _row_tile(rows, cols)

    def body(s_ref, w_ref, o_ref):
        o_ref[...] = w_ref[...].astype(BF16)

    out = pl.pallas_call(
        body, name="place_cast",
        grid_spec=pltpu.PrefetchScalarGridSpec(
            num_scalar_prefetch=1, grid=(rows // tr,),
            in_specs=[pl.BlockSpec((tr, cols), lambda i, s: (i, 0))],
            out_specs=pl.BlockSpec((None, tr, cols), lambda i, s: (s[0], i, 0))),
        out_shape=_sds((N_CHIPS, rows, cols), BF16), compiler_params=_params("parallel"),
    )(_scalars(chip), w.reshape(rows, cols))
    return out.reshape((N_CHIPS,) + w.shape)


def _add_halves(g, got, core):
    _, rows, cols = got.shape
    tr = _row_tile(rows, cols)

    def body(s_ref, a_ref, b_ref, o_ref):
        o_ref[...] = (a_ref[...].astype(F32) + b_ref[...].astype(F32)).astype(BF16)

    spec = pl.BlockSpec((None, tr, cols), lambda j, i, s: (j, i, 0))
    return pl.pallas_call(
        body, name="add_halves",
        grid_spec=pltpu.PrefetchScalarGridSpec(
            num_scalar_prefetch=1, grid=(N_CHIPS, rows // tr),
            in_specs=[pl.BlockSpec((None, None, tr, cols), lambda j, i, s: (j, s[0], i, 0)), spec], out_specs=spec),
        out_shape=_sds(got.shape, BF16), compiler_params=_params("parallel", "parallel"),
    )(_scalars(core), g, got)


def _sum_parts(own, others, chip, core):
    _, rows, cols = own.shape
    tr = _row_tile(rows, cols)

    def body(s_ref, a_ref, t_ref, o_ref):
        total = a_ref[...].astype(F32)
        for j in range(N_CHIPS - 1):
            total = total + t_ref[j].astype(F32)
        o_ref[...] = total

    return pl.pallas_call(
        body, name="sum_parts",
        grid_spec=pltpu.PrefetchScalarGridSpec(
            num_scalar_prefetch=1, grid=(rows // tr,),
            in_specs=[pl.BlockSpec((None, tr, cols), lambda i, s: (s[0], i, 0)),
                      pl.BlockSpec((N_CHIPS - 1, tr, cols), lambda i, s: (0, i, 0))],
            out_specs=pl.BlockSpec((None, tr, cols), lambda i, s: (s[1], i, 0))),
        out_shape=_sds((2, rows, cols), F32), compiler_params=_params("parallel"),
    )(_scalars(chip, core), own, others)


def _sum_leading(a, name):
    n = a.shape[0]
    shape = a.shape[1:]
    cols = shape[-1]
    rows = a.size // (n * cols)
    tr = _tile(rows, max(8, (1 << 20) // (4 * cols)), 8)

    def body(a_ref, o_ref):
        total = a_ref[0].astype(F32)
        for j in range(1, n):
            total = total + a_ref[j].astype(F32)
        o_ref[...] = total

    out = pl.pallas_call(
        body, name=name, grid=(rows // tr,), in_specs=[pl.BlockSpec((n, tr, cols), lambda i: (0, i, 0))],
        out_specs=pl.BlockSpec((tr, cols), lambda i: (i, 0)), out_shape=_sds((rows, cols), F32),
        compiler_params=_params("parallel"))(a.reshape(n, rows, cols))
    return out.reshape(shape)


def _adamw(w, g, m, v):
    def body(w_ref, g_ref, m_ref, v_ref, go_ref, d_ref, nm_ref, nv_ref):
        gv = g_ref[...]
        go_ref[...] = gv
        nm = ADAM_B1 * m_ref[...] + (1.0 - ADAM_B1) * gv
        nv = ADAM_B2 * v_ref[...] + (1.0 - ADAM_B2) * jnp.square(gv)
        nm_ref[...] = nm
        nv_ref[...] = nv
        m_hat = nm / (1.0 - ADAM_B1 ** ADAM_STEP)
        v_hat = nv / (1.0 - ADAM_B2 ** ADAM_STEP)
        d_ref[...] = -ADAM_LR * (m_hat / (jnp.sqrt(v_hat) + ADAM_EPS) + ADAM_WD * w_ref[...])

    return _flat_call("adamw", body, (w, g, m, v), (F32, F32, F32, F32), block_bytes=1 << 21)


def _mesh_position():
    return lax.axis_index("x"), lax.axis_index("y"), lax.axis_index("c")


def _other_chips(x, y):
    return [(1 - x, y), (x, 1 - y), (1 - x, 1 - y)]


def _chip_index(chip):
    return 2 * chip[0] + chip[1]


def _remote(src, dst, send_sem, recv_sem, device):
    return pltpu.make_async_remote_copy(src_ref=src, dst_ref=dst, send_sem=send_sem, recv_sem=recv_sem,
                                        device_id=device, device_id_type=MESH)


def _pieces(view):
    shape = list(view.shape)
    counts = []
    for axis in range(len(shape) - 1):
        want = -(-(view.dtype.itemsize * functools.reduce(lambda a, b: a * b, shape)) // DMA_PIECE_BYTES)
        unit = 1 if axis < len(shape) - 2 else ROWS_PER_TILE
        n, k = shape[axis], 1
        for cand in range(1, n + 1):
            if n % cand == 0 and (n // cand) % unit == 0:
                k = cand
                if cand >= want:
                    break
        counts.append(k)
        shape[axis] = n // k
    grid = [()]
    for axis, k in enumerate(counts):
        grid = [idx + (pl.ds(i * shape[axis], shape[axis]),) for idx in grid for i in range(k)]
    return grid


def _start_remote(src, dst, send_sem, recv_sem, device):
    for idx in _pieces(src):
        _remote(src.at[idx], dst.at[idx], send_sem, recv_sem, device).start()
    return _remote(src, dst, send_sem, recv_sem, device)


class _Ride:
    def __init__(self, operands, out_shapes, aliased, scratch_shapes, first, mid, last):
        self.operands, self.out_shapes, self.aliased = list(operands), list(out_shapes), aliased
        self.scratch_shapes, self.first, self.mid, self.last = list(scratch_shapes), first, mid, last

    def aliases(self, first_in, first_out):
        return {first_in + w: first_out + w for w in range(len(self.operands))} if self.aliased else {}


def _run_ride(name, ride):
    n = len(ride.operands)

    def body(*refs):
        ins, outs, sems = refs[:n], refs[n:2 * n], refs[2 * n:]
        ride.first(ins, outs, sems)
        if ride.mid is not None:
            ride.mid(ins, outs, sems)
        ride.last(ins, outs, sems)

    return pl.pallas_call(
        body, name=name, in_specs=[ANY] * n, out_specs=tuple([ANY] * n), out_shape=tuple(ride.out_shapes),
        input_output_aliases=ride.aliases(0, 0), scratch_shapes=ride.scratch_shapes,
    )(*ride.operands)


def _gather_ride(blocks):
    n = len(blocks)
    halves = [a.shape[1] // 2 for a in blocks]
    assert all(a.shape[1] % 2 == 0 for a in blocks)
    pairs = [(w, j) for w in range(n) for j in range(3)]

    def plan(outs, sems):
        ici_send, ici_recv, d2d_send, d2d_recv = sems
        x, y, c = _mesh_position()
        chips = _other_chips(x, y)

        def half(w, chip, which):
            return outs[w].at[chip, pl.ds(which * halves[w], halves[w])]

        def over_ici(w, j, chip):
            there = half(w, chip, c)
            return there, there, ici_send.at[3 * w + j], ici_recv.at[3 * w + j], (*chips[j], c)

        def over_d2d(w, j, which):
            there = half(w, _chip_index(chips[j]), which)
            return there, there, d2d_send.at[3 * w + j], d2d_recv.at[3 * w + j], (x, y, 1 - c)

        return _chip_index((x, y)), c, chips, over_ici, over_d2d

    def first(ins, outs, sems):
        me, _, _, over_ici, _ = plan(outs, sems)
        for w, j in pairs:
            _start_remote(*over_ici(w, j, me))

    def mid(ins, outs, sems):
        _, c, chips, over_ici, over_d2d = plan(outs, sems)
        for w, j in pairs:
            _remote(*over_ici(w, j, _chip_index(chips[j]))).wait_recv()
            _start_remote(*over_d2d(w, j, c))

    def last(ins, outs, sems):
        me, c, _, over_ici, over_d2d = plan(outs, sems)
        for w, j in pairs:
            _remote(*over_d2d(w, j, 1 - c)).wait_recv()
        for w, j in pairs:
            _remote(*over_ici(w, j, me)).wait_send()
            _remote(*over_d2d(w, j, c)).wait_send()

    return _Ride(blocks, [_sds(a.shape, a.dtype) for a in blocks], True,
                 [pltpu.SemaphoreType.DMA((3 * n,))] * 4, first, mid, last)


def _swap_halves(grads):
    n = len(grads)

    def body(*refs):
        ins, got = refs[:n], refs[n:2 * n]
        send_sem, recv_sem = refs[2 * n:]
        x, y, c = _mesh_position()
        swaps = [_start_remote(ins[w].at[pl.ds(0, N_CHIPS), 1 - c], got[w], send_sem.at[w], recv_sem.at[w],
                               (x, y, 1 - c)) for w in range(n)]
        for cp in swaps:
            cp.wait()

    return pl.pallas_call(
        body, name="swap_halves", in_specs=[ANY] * n, out_specs=tuple([ANY] * n),
        out_shape=tuple(_sds((N_CHIPS,) + a.shape[2:], a.dtype) for a in grads),
        scratch_shapes=[pltpu.SemaphoreType.DMA((n,))] * 2,
    )(*grads)


def _scatter_ride(parts):
    n = len(parts)
    pairs = [(w, j) for w in range(n) for j in range(3)]

    def plan(ins, outs, sems):
        send_sem, recv_sem = sems
        x, y, c = _mesh_position()
        chips = _other_chips(x, y)

        def to_chip(w, j):
            return (ins[w].at[_chip_index(chips[j])], outs[w].at[j], send_sem.at[3 * w + j], recv_sem.at[3 * w + j],
                    (*chips[j], c))

        return to_chip

    def first(ins, outs, sems):
        to_chip = plan(ins, outs, sems)
        for w, j in pairs:
            _start_remote(*to_chip(w, j))

    def last(ins, outs, sems):
        to_chip = plan(ins, outs, sems)
        for w, j in pairs:
            _remote(*to_chip(w, j)).wait_recv()
        for w, j in pairs:
            _remote(*to_chip(w, j)).wait_send()

    return _Ride(parts, [_sds((N_CHIPS - 1,) + a.shape[1:], a.dtype) for a in parts], False,
                 [pltpu.SemaphoreType.DMA((3 * n,))] * 2, first, None, last)


def _join_ride(halves):
    n = len(halves)

    def plan(outs, sems):
        send_sem, recv_sem = sems
        x, y, c = _mesh_position()

        def half(w, which):
            return outs[w].at[which], outs[w].at[which], send_sem.at[w], recv_sem.at[w], (x, y, 1 - c)

        return c, half

    def first(ins, outs, sems):
        c, half = plan(outs, sems)
        for w in range(n):
            _start_remote(*half(w, c))

    def last(ins, outs, sems):
        c, half = plan(outs, sems)
        for w in range(n):
            _remote(*half(w, 1 - c)).wait_recv()
            _remote(*half(w, c)).wait_send()

    return _Ride(halves, [_sds(a.shape, a.dtype) for a in halves], True,
                 [pltpu.SemaphoreType.DMA((n,))] * 2, first, None, last)


def _gather_small(vec):
    length = vec.shape[1]
    flips = [(fx, fy, fc) for fx in (0, 1) for fy in (0, 1) for fc in (0, 1)][1:]

    def body(v_ref, o_ref, send_sem, recv_sem, local_sem):
        x, y, c = _mesh_position()
        me = 4 * x + 2 * y + c
        local = pltpu.make_async_copy(v_ref, o_ref.at[me], local_sem)
        local.start()
        peers = [(x ^ fx, y ^ fy, c ^ fc) for fx, fy, fc in flips]
        sends = [_remote(v_ref, o_ref.at[me], send_sem.at[k], recv_sem.at[k], peer) for k, peer in enumerate(peers)]
        for cp in sends:
            cp.start()
        for k, (px, py, pc) in enumerate(peers):
            there = o_ref.at[4 * px + 2 * py + pc]
            _remote(there, there, send_sem.at[k], recv_sem.at[k], peers[k]).wait_recv()
        for cp in sends:
            cp.wait_send()
        local.wait()

    return pl.pallas_call(
        body, name="gather_small", in_specs=[ANY], out_specs=ANY, out_shape=_sds((N_DEVICES, 1, length), F32),
        scratch_shapes=[pltpu.SemaphoreType.DMA((N_DEVICES - 1,))] * 2 + [pltpu.SemaphoreType.DMA],
    )(vec)


MATRICES = ("w_in", "w_attn_branch", "w_pool_group", "w_pool_branch", "w_out", "w_up", "w_down", "w_ple", "w_ple_gate")
VECTORS = ("norm_mix_pre", "pool_scale", "norm_mix_post", "norm_ffn_pre", "conv_b", "norm_ffn_post", "norm_ple_post")
WEIGHTS = ("norm_mix_pre", "w_in", "w_attn_branch", "w_pool_group", "pool_scale", "w_pool_branch", "w_out",
           "norm_mix_post", "norm_ffn_pre", "w_up", "conv_w", "conv_b", "w_down", "norm_ffn_post", "w_ple",
           "w_ple_gate", "norm_ple_post")


EARLY = ("w_in",)
LATE = tuple(k for k in MATRICES if k not in EARLY)
RIDES = (("w_ple_gate",), ("w_ple",), ("w_attn_branch", "w_pool_branch"), ("w_out", "w_pool_group"),
         ("w_up",), ("w_down",))
assert sorted(k for names in RIDES for k in names) == sorted(LATE)


def _halves_view(g):
    return g.reshape(N_CHIPS, 2, g.size // (2 * N_CHIPS * g.shape[-1]), g.shape[-1])


def _pair_sums(gmat, names, core):
    flat = [_halves_view(gmat[k]) for k in names]
    return [_add_halves(a, b, core) for a, b in zip(flat, _swap_halves(flat))]


def _local_step(x, p, target, w_in, late_blocks, vec, conv_w, chip, core):
    s, d = x.shape
    groups = len(POOL_WINDOWS)
    aw = w_in.shape[2] // 2
    assert w_in.shape[2] == d

    h1 = _rms_fwd(x, vec["norm_mix_pre"])
    full = {}

    def carrying(names):
        return _gather_ride([late_blocks[k] for k in names])

    def carried(names, outs):
        full.update(zip(names, outs))

    qk, got = _mm_nn("proj_qk", h1, w_in, BF16, j0=0, nj=1, ride=carrying(RIDES[0]))
    carried(RIDES[0], got)
    v = _mm_nn("proj_v", h1, w_in, BF16, j0=1, c0=0, cn=aw)
    u, got = _mm_nn("proj_u", h1, w_in, F32, j0=1, c0=aw, cn=aw, ride=carrying(RIDES[1]))
    carried(RIDES[1], got)
    ga, got = _mm_nn("proj_ga", h1, w_in, BF16, j0=2, nj=1, ride=carrying(RIDES[2]))
    carried(RIDES[2], got)
    gp, got = _mm_nn("proj_gp", h1, w_in, BF16, j0=3, nj=1, ride=carrying(RIDES[3]))
    carried(RIDES[3], got)
    attn, lsum, got = _attn_fwd(qk, v, carrying(RIDES[4]))
    carried(RIDES[4], got)
    w_ab, w_pb, w_up, w_ple = (full[k] for k in ("w_attn_branch", "w_pool_branch", "w_up", "w_ple"))
    w_pg = full["w_pool_group"].transpose(1, 0, 2, 3)
    w_pg = w_pg.reshape(groups, -1, w_pg.shape[-1])
    as_rows = lambda k: full[k].reshape(1, -1, full[k].shape[-1])
    w_out, w_pleg = as_rows("w_out"), as_rows("w_ple_gate")
    assert w_pb.shape[1] == aw == w_ab.shape[1]
    ya = _mm_nn("attn_branch", attn, w_ab, BF16)
    pooled = _pool_fwd(u)
    pg = _group_nn("pool_group", pooled, w_pg, F32)
    ps = _scale_cols(pg, vec["pool_scale"])
    yp = _mm_nn("pool_branch", ps, w_pb, BF16)
    mixed = _mix_fwd(ga, gp, ya, yp)
    mo = _mm_nn("mix_out", mixed, w_out, F32)
    x2, h2 = _post_pre(x, mo, vec["norm_mix_post"], vec["norm_ffn_pre"])
    up, got = _mm_nn("ffn_up", h2, w_up, F32, tn=1408, ride=carrying(RIDES[5]))
    carried(RIDES[5], got)
    w_down = as_rows("w_down")
    act, cg, cv = _conv_gelu_fwd(up, conv_w, vec["conv_b"])
    yf = _mm_nn("ffn_down", act, w_down, F32, tk=2816)
    x3 = _post(x2, yf, vec["norm_ffn_post"])
    e = _mm_nn("ple_embed", p, w_ple, F32)
    t = _mm_nn("ple_gate", x3, w_pleg, F32)
    dx4, dt, de, g_ple_post, loss_rows = _final(x3, t, e, vec["norm_ple_post"], target)

    gvec, gmat = {}, {}
    gvec["norm_ple_post"] = g_ple_post
    gmat["w_ple"] = _mm_tn("d_w_ple", p, de, N_CHIPS, BF16)
    gmat["w_ple_gate"] = _mm_tn("d_w_ple_gate", x3, dt, 1, BF16)
    dx3 = _mm_nt("d_x3", dt, w_pleg, F32, add=dx4)
    dyf, gvec["norm_ffn_post"] = _bwd_post(dx3, yf, vec["norm_ffn_post"])
    gmat["w_down"] = _mm_tn("d_w_down", act, dyf, 1, BF16)
    dact = _mm_nt("d_act", dyf, w_down, BF16, tn=1408)
    dup_g, dup_v, dcw_g, dcw_v, dcb_g, dcb_v = _conv_gelu_bwd(up, cg, cv, dact, conv_w)
    dup = _concat_cols("concat_dup", (dup_g, dup_v))
    gconv_w = jnp.concatenate([dcw_g, dcw_v], axis=1)
    gvec["conv_b"] = jnp.concatenate([dcb_g, dcb_v], axis=1)
    gmat["w_up"] = _mm_tn("d_w_up", h2, dup, N_CHIPS, BF16, tn=1408)
    dh2 = _mm_nt("d_h2", dup, w_up, F32, tk=2816)
    dx2, dmo, gvec["norm_ffn_pre"], gvec["norm_mix_post"] = _bwd_mid(
        dx3, dh2, x2, vec["norm_ffn_pre"], mo, vec["norm_mix_post"])
    gmat["w_out"] = _mm_tn("d_w_out", mixed, dmo, 1, BF16)
    dmixed = _mm_nt("d_mixed", dmo, w_out, F32)
    dya, dyp, dga, dgp = _bwd_mix(dmixed, ga, gp, ya, yp)
    gmat["w_attn_branch"] = _mm_tn("d_w_attn_branch", attn, dya, N_CHIPS, BF16)
    dattn = _mm_nt("d_attn", dya, w_ab, BF16)
    gmat["w_pool_branch"] = _mm_tn("d_w_pool_branch", ps, dyp, N_CHIPS, BF16)
    dps = _mm_nt("d_ps", dyp, w_pb, F32)
    dpg, gvec["pool_scale"] = _bwd_pool_scale(dps, pg, vec["pool_scale"])
    g_pg = _group_tn("d_w_pool_group", pooled, dpg, groups, BF16)
    gmat["w_pool_group"] = g_pg.reshape(groups, N_CHIPS, -1, g_pg.shape[-1]).transpose(1, 0, 2, 3)
    dpooled = _group_nn("d_pooled", dpg, w_pg, F32, transpose_w=True)
    du = _pool_bwd(dpooled)
    pair = dict(zip(LATE, _pair_sums(gmat, LATE, core)))
    dq, dk, dv, others = _attn_bwd(qk, v, lsum, dattn, _scatter_ride([pair[k] for k in LATE]))
    halves = [_sum_parts(pair[k], part, chip, core) for k, part in zip(LATE, others)]
    dproj = _concat_cols("concat_dproj", (dq, dk, dv, du, dga, dgp))
    gmat["w_in"], joined = _mm_tn("d_w_in", h1, dproj, N_CHIPS, BF16, ride=_join_ride(halves))
    reduced = dict(zip(LATE, joined))
    early = _pair_sums(gmat, EARLY, core)
    dh1, got = _mm_nt("d_h1", dproj, w_in, F32, ride=_scatter_ride(early))
    halves = [_sum_parts(a, part, chip, core) for a, part in zip(early, got)]
    reduced.update(zip(EARLY, _run_ride("join_halves", _join_ride(halves))))
    grad_x, gvec["norm_mix_pre"] = _bwd_first(dx2, dh1, x, vec["norm_mix_pre"])
    return loss_rows[0, 0], grad_x, reduced, gvec, gconv_w


def kernel(x, p, norm_mix_pre, w_in, w_attn_branch, w_pool_group, pool_scale, w_pool_branch, w_out, norm_mix_post, norm_ffn_pre, w_up, conv_w, conv_b, w_down, norm_ffn_post, w_ple, w_ple_gate, norm_ple_post, loss_target, m_norm_mix_pre, m_w_in, m_w_attn_branch, m_w_pool_group, m_pool_scale, m_w_pool_branch, m_w_out, m_norm_mix_post, m_norm_ffn_pre, m_w_up, m_conv_w, m_conv_b, m_w_down, m_norm_ffn_post, m_w_ple, m_w_ple_gate, m_norm_ple_post, v_norm_mix_pre, v_w_in, v_w_attn_branch, v_w_pool_group, v_pool_scale, v_w_pool_branch, v_w_out, v_norm_mix_post, v_norm_ffn_pre, v_w_up, v_conv_w, v_conv_b, v_w_down, v_norm_ffn_post, v_w_ple, v_w_ple_gate, v_norm_ple_post):
    given = dict(locals())
    w = {k: given[k][0] for k in WEIGHTS}
    m = {k: given["m_" + k][0] for k in WEIGHTS}
    v = {k: given["v_" + k][0] for k in WEIGHTS}
    chip = 2 * lax.axis_index("x") + lax.axis_index("y")

    core = lax.axis_index("c")
    w_in_full = _run_ride("gather_weights", _gather_ride([_place_cast(w[k], chip) for k in EARLY]))[0]
    late_blocks = {k: _place_cast(w[k], chip) for k in LATE}
    vec = {k: w[k].reshape(1, -1) for k in VECTORS}
    taps = _gather_small(w["conv_w"].reshape(1, -1))[:, 0][0::2]
    f2q = w["conv_w"].shape[1]
    conv_w_full = taps.reshape(N_CHIPS, 3, f2q).transpose(1, 0, 2).reshape(3, N_CHIPS * f2q)

    loss_rows, grad_x, reduced, gvec, gconv_w = _local_step(
        x[0], p[0, 0], loss_target[0], w_in_full, late_blocks, vec, conv_w_full, chip, core)
    loss = lax.psum(loss_rows, ("x", "y", "c"))

    gw = {k: reduced[k].reshape(w[k].shape) for k in MATRICES}
    sizes = [gvec[k].shape[1] for k in VECTORS]
    small = jnp.concatenate([gvec[k] for k in VECTORS] + [gconv_w.reshape(1, -1)], axis=1)
    small = _sum_leading(_gather_small(small), "sum_small")
    offset = 0
    for k, n in zip(VECTORS, sizes):
        gw[k] = small[0, offset:offset + n]
        offset += n
    gconv_w = small[0, offset:].reshape(3, N_CHIPS, f2q)
    gw["conv_w"] = lax.dynamic_index_in_dim(gconv_w, chip, axis=1, keepdims=False)

    delta, new_m, new_v = {}, {}, {}
    for k in MATRICES:
        gw[k], delta[k], new_m[k], new_v[k] = _adamw(w[k], gw[k], m[k], v[k])
    tiny = VECTORS + ("conv_w",)
    pack = lambda tree: jnp.concatenate([tree[k].reshape(1, -1) for k in tiny], axis=1)
    _, d_s, m_s, v_s = _adamw(pack(w), pack(gw), pack(m), pack(v))
    offset = 0
    for k in tiny:
        n = w[k].size
        delta[k], new_m[k], new_v[k] = (a[0, offset:offset + n].reshape(w[k].shape) for a in (d_s, m_s, v_s))
        offset += n

    lead = lambda tree: [tree[k].reshape((1,) + w[k].shape) for k in WEIGHTS]
    return (loss, grad_x[None], *lead(gw), *lead(delta), *lead(new_m), *lead(new_v))
```

```python
import functools

import jax
import jax.numpy as jnp
from jax import lax
from jax.experimental import pallas as pl
from jax.experimental.pallas import tpu as pltpu

F32 = jnp.float32
BF16 = jnp.bfloat16

HEAD_DIM = 128
POOL_WINDOWS = (2, 4, 8, 16)
EPS = 1e-6
GELU_C = 0.7978845608028654
GELU_A = 0.044715

ADAM_LR = 0.001
ADAM_B1 = 0.9
ADAM_B2 = 0.999
ADAM_EPS = 1e-08
ADAM_WD = 0.01
ADAM_STEP = 10

N_CHIPS = 4
N_DEVICES = 8
VMEM_LIMIT_BYTES = 52 * 1024 * 1024
ATTN_BLOCK = 256
ATTN_Q_BLOCK = 2048
ATTN_HEADS = 1
POOL_TILE = 128
RIDE_MID_STEPS = 2
DMA_PIECE_BYTES = 512 * 1024
ROWS_PER_TILE = 16
MESH = pl.DeviceIdType.MESH
ANY = pl.BlockSpec(memory_space=pl.ANY)


def _sds(shape, dtype):
    return jax.ShapeDtypeStruct(tuple(shape), dtype)


def _tile(dim, pref, mult=128):
    t = min(pref, dim) // mult * mult
    while t >= mult:
        if dim % t == 0:
            return t
        t -= mult
    return dim


def _params(*semantics):
    return pltpu.CompilerParams(dimension_semantics=semantics or None, vmem_limit_bytes=VMEM_LIMIT_BYTES)


def _sigmoid(v):
    return 1.0 / (1.0 + jnp.exp(-v))


def _rstd(v):
    return lax.rsqrt(jnp.mean(v * v, axis=-1, keepdims=True) + EPS)


def _rms_bwd(dy, gain, v):
    r = _rstd(v)
    vh = v * r
    gy = dy * gain
    return r * (gy - vh * jnp.mean(gy * vh, axis=-1, keepdims=True)), dy * vh


def _ride_along(ride, grid, n_in, n_out, own_semantics):
    if ride is None:
        return [], [], [], {}, [], own_semantics, lambda refs: (lambda: None, lambda: None)
    n = len(ride.operands)
    steps = [g - 1 for g in grid]

    def at(step):
        hit = pl.program_id(0) == step[0]
        for axis in range(1, len(grid)):
            hit = jnp.logical_and(hit, pl.program_id(axis) == step[axis])
        return hit

    def hooks(refs):
        ins, outs = refs[n_in:n_in + n], refs[n_in + n + n_out:n_in + 2 * n + n_out]
        sems = refs[len(refs) - len(ride.scratch_shapes):]

        def before():
            pl.when(at([0] * len(grid)))(lambda: ride.first(ins, outs, sems))
            if ride.mid is not None:
                pl.when(at(steps[:-1] + [max(steps[-1] - RIDE_MID_STEPS, 0)]))(lambda: ride.mid(ins, outs, sems))

        def after():
            pl.when(at(steps))(lambda: ride.last(ins, outs, sems))

        return before, after

    return (ride.operands, [ANY] * n, ride.out_shapes, ride.aliases(n_in, n_out), ride.scratch_shapes,
            ("arbitrary",) * len(grid), hooks)


def _matmul(name, a, b, grid, a_spec, b_spec, o_spec, out, dims, acc_shape, add=None, add_spec=None, ride=None):
    nk = grid[2]
    n_in = 2 if add is None else 3
    extra, extra_specs, extra_out, aliases, ride_scratch, semantics, hooks = _ride_along(
        ride, grid, n_in, 1, ("parallel", "parallel", "arbitrary"))

    def body(*refs):
        before, after = hooks(refs)
        before()
        compute(refs)
        after()

    def compute(refs):
        a_ref, b_ref = refs[:2]
        c_ref = None if add is None else refs[2]
        o_ref = refs[n_in + len(extra)]
        part = lax.dot_general(a_ref[...].astype(BF16), b_ref[...].astype(BF16), (dims, ((), ())),
                               preferred_element_type=F32)

        def finish(r):
            if c_ref is not None:
                r = r + c_ref[...]
            o_ref[...] = r.astype(o_ref.dtype)

        if nk == 1:
            finish(part)
            return
        acc = refs[n_in + 2 * len(extra) + 1]
        kk = pl.program_id(2)

        @pl.when(kk == 0)
        def _():
            acc[...] = part

        @pl.when(jnp.logical_and(kk > 0, kk < nk - 1))
        def _():
            acc[...] += part

        @pl.when(kk == nk - 1)
        def _():
            finish(acc[...] + part)

    operands = (a, b) if add is None else (a, b, add)
    in_specs = [a_spec, b_spec] if add is None else [a_spec, b_spec, add_spec]
    outs = pl.pallas_call(
        body, name=name, grid=grid, in_specs=in_specs + extra_specs, out_specs=(o_spec, *extra_specs),
        out_shape=(out, *extra_out), input_output_aliases=aliases,
        scratch_shapes=([] if nk == 1 else [pltpu.VMEM(acc_shape, F32)]) + ride_scratch,
        compiler_params=_params(*semantics),
    )(*operands, *extra)
    return outs[0] if ride is None else (outs[0], list(outs[1:]))


def _mm_nn(name, a, b3, out_dtype, j0=0, nj=None, c0=0, cn=None, tm=1024, tn=1024, tk=2048, ride=None):
    m, k = a.shape
    nj_all, kb, width = b3.shape
    assert kb == k
    if cn is None:
        cn = width
        nj = nj_all - j0 if nj is None else nj
    else:
        nj = 1
    n = nj * cn
    tm, tn, tk = _tile(m, tm, 8), _tile(cn, tn), _tile(k, tk)
    assert c0 % tn == 0
    nb, cb = cn // tn, c0 // tn
    return _matmul(
        name, a, b3, (m // tm, n // tn, k // tk),
        pl.BlockSpec((tm, tk), lambda i, j, kk: (i, kk)),
        pl.BlockSpec((None, tk, tn), lambda i, j, kk: (j0 + j // nb, kk, cb + j % nb)),
        pl.BlockSpec((tm, tn), lambda i, j, kk: (i, j)),
        _sds((m, n), out_dtype), ((1,), (0,)), (tm, tn), ride=ride)


def _mm_nt(name, a, b3, out_dtype, add=None, tm=1024, tn=1024, tk=2048, ride=None):
    m, kc = a.shape
    nj, n, kj = b3.shape
    assert kc == nj * kj
    tm, tn, tk = _tile(m, tm, 8), _tile(n, tn), _tile(kj, tk)
    kb = kj // tk
    o_spec = pl.BlockSpec((tm, tn), lambda i, j, kk: (i, j))
    return _matmul(
        name, a, b3, (m // tm, n // tn, kc // tk),
        pl.BlockSpec((tm, tk), lambda i, j, kk: (i, kk)),
        pl.BlockSpec((None, tn, tk), lambda i, j, kk: (kk // kb, j, kk % kb)),
        o_spec, _sds((m, n), out_dtype), ((1,), (1,)), (tm, tn), add=add, add_spec=o_spec, ride=ride)


def _mm_tn(name, a, b, nj, out_dtype, tm=512, tn=1024, ts=4096, ride=None):
    s, m = a.shape
    s2, n = b.shape
    assert s == s2 and n % nj == 0
    width = n // nj
    tm, tn, ts = _tile(m, tm), _tile(width, tn), _tile(s, ts)
    nb = width // tn
    return _matmul(
        name, a, b, (m // tm, n // tn, s // ts),
        pl.BlockSpec((ts, tm), lambda i, j, kk: (kk, i)),
        pl.BlockSpec((ts, tn), lambda i, j, kk: (kk, j)),
        pl.BlockSpec((None, tm, tn), lambda i, j, kk: (j // nb, i, j % nb)),
        _sds((nj, m, width), out_dtype), ((0,), (0,)), (tm, tn), ride=ride)


def _group_nn(name, a, w, out_dtype, transpose_w=False, tm=1024):
    s, gc = a.shape
    g, c, _ = w.shape
    tm = _tile(s, tm, 8)
    return _matmul(
        name, a, w, (s // tm, g, 1),
        pl.BlockSpec((tm, c), lambda i, j, kk: (i, j)),
        pl.BlockSpec((None, c, c), lambda i, j, kk: (j, 0, 0)),
        pl.BlockSpec((tm, c), lambda i, j, kk: (i, j)),
        _sds((s, gc), out_dtype), ((1,), (1,)) if transpose_w else ((1,), (0,)), (tm, c))


def _group_tn(name, a, b, g, out_dtype, ts=4096):
    s, gc = a.shape
    c = gc // g
    ts = _tile(s, ts, 8)
    return _matmul(
        name, a, b, (g, 1, s // ts),
        pl.BlockSpec((ts, c), lambda i, j, kk: (kk, i)),
        pl.BlockSpec((ts, c), lambda i, j, kk: (kk, i)),
        pl.BlockSpec((None, c, c), lambda i, j, kk: (i, 0, 0)),
        _sds((g, c, c), out_dtype), ((0,), (0,)), (c, c))


def _row(tm, d):
    return pl.BlockSpec((tm, d), lambda i: (i, 0))


def _vec(d):
    return pl.BlockSpec((1, d), lambda i: (0, 0))


def _rows_call(name, body, ins, in_specs, outs, out_specs, steps, accumulates):
    return pl.pallas_call(
        body, name=name, grid=(steps,), in_specs=in_specs, out_specs=out_specs, out_shape=outs,
        compiler_params=_params("arbitrary" if accumulates else "parallel"),
    )(*ins)


def _accumulate(ref, value):
    @pl.when(pl.program_id(0) == 0)
    def _():
        ref[...] = value

    @pl.when(pl.program_id(0) > 0)
    def _():
        ref[...] += value


def _colsum(v):
    return jnp.sum(v, axis=0, keepdims=True)


def _rms_fwd(x, gain, tm=256):
    s, d = x.shape

    def body(x_ref, g_ref, h_ref):
        v = x_ref[...]
        h_ref[...] = (v * _rstd(v) * g_ref[...]).astype(BF16)

    return _rows_call("rms_fwd", body, (x, gain), [_row(tm, d), _vec(d)], _sds((s, d), BF16), _row(tm, d),
                      s // tm, False)


def _mix_fwd(ga, gp, ya, yp, tm=256):
    s, d = ga.shape

    def body(ga_ref, gp_ref, ya_ref, yp_ref, o_ref):
        ga, gp, ya, yp = (ref[...].astype(F32) for ref in (ga_ref, gp_ref, ya_ref, yp_ref))
        o_ref[...] = (_sigmoid(ga) * ya + _sigmoid(gp) * yp).astype(BF16)

    return _rows_call("mix_fwd", body, (ga, gp, ya, yp), [_row(tm, d)] * 4, _sds((s, d), BF16), _row(tm, d),
                      s // tm, False)


def _scale_cols(v, scale, tm=256):
    s, d = v.shape

    def body(v_ref, s_ref, o_ref):
        o_ref[...] = (v_ref[...] * s_ref[...]).astype(BF16)

    return _rows_call("pool_scale_fwd", body, (v, scale), [_row(tm, d), _vec(d)], _sds((s, d), BF16), _row(tm, d),
                      s // tm, False)


def _post_pre(x, y, gain_post, gain_pre, tm=256):
    s, d = x.shape

    def body(x_ref, y_ref, gp_ref, gn_ref, x2_ref, h_ref):
        y = y_ref[...]
        x2 = x_ref[...] + y * _rstd(y) * gp_ref[...]
        x2_ref[...] = x2
        h_ref[...] = (x2 * _rstd(x2) * gn_ref[...]).astype(BF16)

    return _rows_call("post_pre", body, (x, y, gain_post, gain_pre), [_row(tm, d), _row(tm, d), _vec(d), _vec(d)],
                      (_sds((s, d), F32), _sds((s, d), BF16)), (_row(tm, d), _row(tm, d)), s // tm, False)


def _post(x, y, gain_post, tm=256):
    s, d = x.shape

    def body(x_ref, y_ref, gp_ref, o_ref):
        y = y_ref[...]
        o_ref[...] = x_ref[...] + y * _rstd(y) * gp_ref[...]

    return _rows_call("post", body, (x, y, gain_post), [_row(tm, d), _row(tm, d), _vec(d)], _sds((s, d), F32),
                      _row(tm, d), s // tm, False)


def _final(x3, t, e, gain, target, tm=256):
    s, d = x3.shape

    def body(x_ref, t_ref, e_ref, g_ref, y_ref, dx_ref, dt_ref, de_ref, dg_ref, loss_ref):
        sg = _sigmoid(t_ref[...])
        ev = e_ref[...]
        pe = sg * ev
        diff = x_ref[...] + pe * _rstd(pe) * g_ref[...] - y_ref[...]
        dx = diff * (1.0 / d)
        dx_ref[...] = dx
        part = 0.5 * jnp.sum(jnp.mean(diff * diff, axis=-1, keepdims=True), axis=0, keepdims=True)
        _accumulate(loss_ref, jnp.broadcast_to(part, loss_ref.shape))
        dpe, dgain = _rms_bwd(dx, g_ref[...], pe)
        de_ref[...] = (dpe * sg).astype(BF16)
        dt_ref[...] = (dpe * ev * sg * (1.0 - sg)).astype(BF16)
        _accumulate(dg_ref, _colsum(dgain))

    return _rows_call("final", body, (x3, t, e, gain, target),
                      [_row(tm, d), _row(tm, d), _row(tm, d), _vec(d), _row(tm, d)],
                      (_sds((s, d), F32), _sds((s, d), BF16), _sds((s, d), BF16), _sds((1, d), F32),
                       _sds((8, 128), F32)),
                      (_row(tm, d), _row(tm, d), _row(tm, d), _vec(d), pl.BlockSpec((8, 128), lambda i: (0, 0))),
                      s // tm, True)


def _concat_cols(name, parts, tm=256):
    s = parts[0].shape[0]
    widths = [a.shape[1] for a in parts]

    def body(*refs):
        o_ref, offset = refs[-1], 0
        for ref, width in zip(refs[:-1], widths):
            o_ref[:, offset:offset + width] = ref[...].astype(BF16)
            offset += width

    return _rows_call(name, body, parts, [_row(tm, width) for width in widths], _sds((s, sum(widths)), BF16),
                      _row(tm, sum(widths)), s // tm, False)


def _bwd_post(dx, y, gain, tm=256):
    s, d = dx.shape

    def body(dx_ref, y_ref, g_ref, dy_ref, dg_ref):
        dy, dgain = _rms_bwd(dx_ref[...], g_ref[...], y_ref[...])
        dy_ref[...] = dy.astype(BF16)
        _accumulate(dg_ref, _colsum(dgain))

    return _rows_call("bwd_post", body, (dx, y, gain), [_row(tm, d), _row(tm, d), _vec(d)],
                      (_sds((s, d), BF16), _sds((1, d), F32)), (_row(tm, d), _vec(d)), s // tm, True)


def _bwd_mid(dx3, dh2, x2, gain_pre, mo, gain_post, tm=128):
    s, d = dx3.shape

    def body(dx3_ref, dh_ref, x2_ref, gn_ref, mo_ref, gp_ref, dx2_ref, dmo_ref, dgn_ref, dgp_ref):
        dv, dgn = _rms_bwd(dh_ref[...], gn_ref[...], x2_ref[...])
        dx2 = dx3_ref[...] + dv
        dx2_ref[...] = dx2
        dmo, dgp = _rms_bwd(dx2, gp_ref[...], mo_ref[...])
        dmo_ref[...] = dmo.astype(BF16)
        _accumulate(dgn_ref, _colsum(dgn))
        _accumulate(dgp_ref, _colsum(dgp))

    return _rows_call("bwd_mid", body, (dx3, dh2, x2, gain_pre, mo, gain_post),
                      [_row(tm, d), _row(tm, d), _row(tm, d), _vec(d), _row(tm, d), _vec(d)],
                      (_sds((s, d), F32), _sds((s, d), BF16), _sds((1, d), F32), _sds((1, d), F32)),
                      (_row(tm, d), _row(tm, d), _vec(d), _vec(d)), s // tm, True)


def _bwd_first(dx2, dh1, x, gain, tm=256):
    s, d = dx2.shape

    def body(dx2_ref, dh_ref, x_ref, g_ref, dx_ref, dg_ref):
        dv, dgain = _rms_bwd(dh_ref[...], g_ref[...], x_ref[...])
        dx_ref[...] = dx2_ref[...] + dv
        _accumulate(dg_ref, _colsum(dgain))

    return _rows_call("bwd_first", body, (dx2, dh1, x, gain), [_row(tm, d), _row(tm, d), _row(tm, d), _vec(d)],
                      (_sds((s, d), F32), _sds((1, d), F32)), (_row(tm, d), _vec(d)), s // tm, True)


def _bwd_mix(dmixed, ga, gp, ya, yp, tm=256):
    s, d = dmixed.shape

    def body(dm_ref, ga_ref, gp_ref, ya_ref, yp_ref, dya_ref, dyp_ref, dga_ref, dgp_ref):
        dm = dm_ref[...]
        ga, gp, ya, yp = (ref[...].astype(F32) for ref in (ga_ref, gp_ref, ya_ref, yp_ref))
        sa, sp = _sigmoid(ga), _sigmoid(gp)
        dya_ref[...] = (dm * sa).astype(BF16)
        dyp_ref[...] = (dm * sp).astype(BF16)
        dga_ref[...] = (dm * ya * sa * (1.0 - sa)).astype(BF16)
        dgp_ref[...] = (dm * yp * sp * (1.0 - sp)).astype(BF16)

    return _rows_call("bwd_mix", body, (dmixed, ga, gp, ya, yp), [_row(tm, d)] * 5,
                      (_sds((s, d), BF16),) * 4, (_row(tm, d),) * 4, s // tm, False)


def _bwd_pool_scale(dps, pg, scale, tm=256):
    s, d = dps.shape

    def body(d_ref, pg_ref, s_ref, dpg_ref, ds_ref):
        dv = d_ref[...]
        dpg_ref[...] = (dv * s_ref[...]).astype(BF16)
        _accumulate(ds_ref, _colsum(dv * pg_ref[...]))

    return _rows_call("bwd_pool_scale", body, (dps, pg, scale), [_row(tm, d), _row(tm, d), _vec(d)],
                      (_sds((s, d), BF16), _sds((1, d), F32)), (_row(tm, d), _vec(d)), s // tm, True)


def _shift_down(v, k, rows):
    return jnp.where(rows >= k, pltpu.roll(v, k, 0), 0.0)


def _shift_up(v, k, rows):
    s = v.shape[0]
    return jnp.where(rows < s - k, pltpu.roll(v, s - k, 0), 0.0)


def _window_pick(group, sums, rows):
    total = sums[-1]
    width = jnp.full((), POOL_WINDOWS[-1], jnp.int32)
    for g in range(len(POOL_WINDOWS) - 2, -1, -1):
        total = jnp.where(group == g, sums[g], total)
        width = jnp.where(group == g, POOL_WINDOWS[g], width)
    return total, jnp.minimum(rows + 1, width).astype(F32)


def _doubling(v, shift, rows):
    sums, k = [], 1
    for _ in POOL_WINDOWS:
        v = v + shift(v, k, rows)
        sums.append(v)
        k *= 2
    return sums


def _pool_fwd(u):
    s, width = u.shape
    per_group = width // len(POOL_WINDOWS) // POOL_TILE

    def body(u_ref, o_ref):
        v = u_ref[...]
        rows = lax.broadcasted_iota(jnp.int32, (s, 1), 0)
        total, count = _window_pick(pl.program_id(0), _doubling(v, _shift_down, rows), rows)
        o_ref[...] = (total / count - v).astype(BF16)

    spec = pl.BlockSpec((s, POOL_TILE), lambda g, j: (0, g * per_group + j))
    return pl.pallas_call(body, name="pool_fwd", grid=(len(POOL_WINDOWS), per_group), in_specs=[spec], out_specs=spec,
                          out_shape=_sds((s, width), BF16), compiler_params=_params("parallel", "parallel"))(u)


def _pool_bwd(dpooled):
    s, width = dpooled.shape
    per_group = width // len(POOL_WINDOWS) // POOL_TILE

    def body(d_ref, o_ref):
        dv = d_ref[...]
        rows = lax.broadcasted_iota(jnp.int32, (s, 1), 0)
        group = pl.program_id(0)
        _, count = _window_pick(group, [dv] * len(POOL_WINDOWS), rows)
        total, _ = _window_pick(group, _doubling(dv / count, _shift_up, rows), rows)
        o_ref[...] = (total - dv).astype(BF16)

    spec = pl.BlockSpec((s, POOL_TILE), lambda g, j: (0, g * per_group + j))
    return pl.pallas_call(body, name="pool_bwd", grid=(len(POOL_WINDOWS), per_group), in_specs=[spec], out_specs=spec,
                          out_shape=_sds((s, width), BF16), compiler_params=_params("parallel", "parallel"))(dpooled)


def _conv(v, w_ref, b_ref, rows):
    out = b_ref[...] + _shift_down(v, 2, rows) * w_ref[0:1, :]
    out = out + _shift_down(v, 1, rows) * w_ref[1:2, :]
    return out + v * w_ref[2:3, :]


def _gelu_parts(v):
    th = jnp.tanh(GELU_C * (v + GELU_A * v * v * v))
    return th, 0.5 * v * (1.0 + th)


def _conv_gelu_fwd(up, conv_w, conv_b, tc=128):
    s, f2 = up.shape
    f = f2 // 2
    nb = f // tc

    def body(g_ref, v_ref, wg_ref, wv_ref, bg_ref, bv_ref, o_ref, cg_ref, cv_ref):
        rows = lax.broadcasted_iota(jnp.int32, (s, 1), 0)
        cg = _conv(g_ref[...], wg_ref, bg_ref, rows)
        cv = _conv(v_ref[...], wv_ref, bv_ref, rows)
        _, gl = _gelu_parts(cg)
        o_ref[...] = (gl * cv).astype(BF16)
        cg_ref[...] = cg.astype(BF16)
        cv_ref[...] = cv.astype(BF16)

    lo = lambda r: pl.BlockSpec((r, tc), lambda j: (0, j))
    hi = lambda r: pl.BlockSpec((r, tc), lambda j: (0, j + nb))
    return pl.pallas_call(
        body, name="conv_gelu_fwd", grid=(nb,), in_specs=[lo(s), hi(s), lo(3), hi(3), lo(1), hi(1)],
        out_specs=(lo(s),) * 3, out_shape=(_sds((s, f), BF16),) * 3,
        compiler_params=_params("parallel"))(up, up, conv_w, conv_w, conv_b, conv_b)


def _conv_gelu_bwd(up, cg, cv, dact, conv_w, tc=128):
    s, f2 = up.shape
    f = f2 // 2
    nb = f // tc

    def body(g_ref, v_ref, cg_ref, cv_ref, d_ref, wg_ref, wv_ref, dg_ref, dv_ref, dwg_ref, dwv_ref, dbg_ref, dbv_ref):
        rows = lax.broadcasted_iota(jnp.int32, (s, 1), 0)
        dact = d_ref[...].astype(F32)

        def back(pre, dc, w_ref, dpre_ref, dw_ref, db_ref):
            dc1, dc2 = _shift_up(dc, 1, rows), _shift_up(dc, 2, rows)
            db_ref[...] = _colsum(dc)
            dw_ref[0:1, :] = _colsum(dc2 * pre)
            dw_ref[1:2, :] = _colsum(dc1 * pre)
            dw_ref[2:3, :] = _colsum(dc * pre)
            dpre_ref[...] = (dc * w_ref[2:3, :] + dc1 * w_ref[1:2, :] + dc2 * w_ref[0:1, :]).astype(BF16)

        gate, val = g_ref[...], v_ref[...]
        cg, cv = cg_ref[...].astype(F32), cv_ref[...].astype(F32)
        th, gl = _gelu_parts(cg)
        dgl = 0.5 * (1.0 + th) + 0.5 * cg * (1.0 - th * th) * GELU_C * (1.0 + 3.0 * GELU_A * cg * cg)
        back(val, dact * gl, wv_ref, dv_ref, dwv_ref, dbv_ref)
        back(gate, dact * cv * dgl, wg_ref, dg_ref, dwg_ref, dbg_ref)

    lo = lambda r: pl.BlockSpec((r, tc), lambda j: (0, j))
    hi = lambda r: pl.BlockSpec((r, tc), lambda j: (0, j + nb))
    return pl.pallas_call(
        body, name="conv_gelu_bwd", grid=(nb,),
        in_specs=[lo(s), hi(s), lo(s), lo(s), lo(s), lo(3), hi(3)],
        out_specs=(lo(s), lo(s), lo(3), lo(3), lo(1), lo(1)),
        out_shape=(_sds((s, f), BF16), _sds((s, f), BF16), _sds((3, f), F32), _sds((3, f), F32),
                   _sds((1, f), F32), _sds((1, f), F32)),
        compiler_params=_params("parallel"))(up, up, cg, cv, dact, conv_w, conv_w)


def _dot(a, b, dims):
    return lax.dot_general(a, b, (dims, ((), ())), preferred_element_type=F32)


def _running_sums(v, tri):
    hi = v.astype(BF16)
    lo = (v - hi.astype(F32)).astype(BF16)
    return _dot(hi, tri, ((1,), (0,))) + _dot(lo, tri, ((1,), (0,)))


def _attn_scores(q, kt, mask):
    z = _dot(q, kt, ((1,), (1,))) * (HEAD_DIM ** -0.5)
    e = jnp.exp(-jnp.abs(z))
    log_1m_beta = -(jnp.maximum(z, 0.0) + jnp.log(1.0 + e))
    return z, e, log_1m_beta if mask is None else jnp.where(mask, log_1m_beta, 0.0)


def _masked(v, mask):
    return v if mask is None else jnp.where(mask, v, 0.0)


def _attn_consts(t):
    rows = lax.broadcasted_iota(jnp.int32, (t, t), 0)
    cols = lax.broadcasted_iota(jnp.int32, (t, t), 1)
    return (rows >= cols).astype(BF16), (rows <= cols).astype(BF16)


def _causal(rows, t):
    return lax.broadcasted_iota(jnp.int32, (rows, t), 1) < lax.broadcasted_iota(jnp.int32, (rows, t), 0)


def _attn_specs(s, tq, heads):
    lanes = ATTN_HEADS * HEAD_DIM
    steps = heads // ATTN_HEADS
    q = pl.BlockSpec((tq, lanes), lambda h, i: (i, h))
    k = pl.BlockSpec((s, lanes), lambda h, i: (0, steps + h))
    whole = pl.BlockSpec((s, lanes), lambda h, i: (0, h))
    rows = pl.BlockSpec((ATTN_HEADS, tq, 1), lambda h, i: (h, i, 0))
    return q, k, whole, rows


def _attn_blocks(s):
    t = min(ATTN_BLOCK, s)
    tq = min(ATTN_Q_BLOCK, s)
    assert tq % t == 0 and s % tq == 0
    return t, tq, tq // t


def _head(p):
    return pl.ds(p * HEAD_DIM, HEAD_DIM)


def _attn_fwd(qk, v, ride=None):
    s, width = v.shape
    heads = width // HEAD_DIM
    t, tq, ratio = _attn_blocks(s)
    assert heads % ATTN_HEADS == 0
    grid = (heads // ATTN_HEADS, s // tq)
    extra, extra_specs, extra_out, aliases, scratch, semantics, hooks = _ride_along(
        ride, grid, 3, 2, ("parallel", "parallel"))

    def body(*refs):
        q_ref, k_ref, v_ref = refs[:3]
        o_ref, lsum_ref = refs[3 + len(extra):5 + len(extra)]
        before, after = hooks(refs)
        before()
        i = pl.program_id(1)
        tri, _ = _attn_consts(t)

        def block(p, rows, start, carry, mask):
            acc, log_rest = carry
            z, _, log_1m_beta = _attn_scores(q_ref[rows, _head(p)], k_ref[pl.ds(start, t), _head(p)], mask)
            total = _running_sums(log_1m_beta, tri) + log_rest
            a = _masked(jnp.exp(z + total), mask)
            return acc + _dot(a.astype(BF16), v_ref[pl.ds(start, t), _head(p)], ((1,), (0,))), total[:, 0:1]

        def own_rows(p):
            carry = (jnp.zeros((tq, HEAD_DIM), F32), jnp.zeros((tq, 1), F32))
            for r in range(ratio - 1, -1, -1):
                below = tuple(val[r * t:] for val in carry)
                below = block(p, pl.ds(r * t, tq - r * t), pl.multiple_of(i * tq + r * t, t), below,
                              _causal(tq - r * t, t))
                carry = tuple(jnp.concatenate([val[:r * t], new], axis=0) if r else new
                              for val, new in zip(carry, below))
            return carry

        def step(n, carry):
            start = pl.multiple_of((i * ratio - 1 - n) * t, t)
            return tuple(block(p, pl.ds(0, tq), start, carry[p], None) for p in range(ATTN_HEADS))

        carry = lax.fori_loop(0, i * ratio, step, tuple(own_rows(p) for p in range(ATTN_HEADS)))
        for p in range(ATTN_HEADS):
            o_ref[:, _head(p)] = carry[p][0].astype(BF16)
            lsum_ref[p] = carry[p][1]
        after()

    q_spec, k_spec, whole, rows = _attn_specs(s, tq, heads)
    outs = pl.pallas_call(
        body, name="attn_fwd", grid=grid, in_specs=[q_spec, k_spec, whole] + extra_specs,
        out_specs=(q_spec, rows, *extra_specs),
        out_shape=(_sds((s, width), BF16), _sds((heads, s, 1), F32), *extra_out),
        input_output_aliases=aliases, scratch_shapes=scratch,
        compiler_params=_params(*semantics))(qk, qk, v, *extra)
    return outs[0], outs[1], list(outs[2:])


def _attn_bwd(qk, v, lsum, do, ride=None):
    s, width = v.shape
    heads = width // HEAD_DIM
    t, tq, ratio = _attn_blocks(s)
    grid = (heads // ATTN_HEADS, s // tq)
    extra, extra_specs, extra_out, aliases, scratch, semantics, hooks = _ride_along(
        ride, grid, 5, 3, ("parallel", "arbitrary"))

    def body(*refs):
        q_ref, k_ref, v_ref, lsum_ref, do_ref = refs[:5]
        dq_ref, dk_ref, dv_ref = refs[5 + len(extra):8 + len(extra)]
        before, after = hooks(refs)
        before()
        i = pl.program_id(1)

        @pl.when(i == 0)
        def _():
            dk_ref[...] = jnp.zeros_like(dk_ref)
            dv_ref[...] = jnp.zeros_like(dv_ref)

        _, triu = _attn_consts(t)

        def block(p, rows, start, carry, mask):
            dq, log_rest, g_before = carry
            q, do_blk = q_ref[rows, _head(p)], do_ref[rows, _head(p)]
            kt, vt = k_ref[pl.ds(start, t), _head(p)], v_ref[pl.ds(start, t), _head(p)]
            z, e, log_1m_beta = _attn_scores(q, kt, mask)
            upto = _running_sums(log_1m_beta, triu)
            a = _masked(jnp.exp(z + log_rest - upto + log_1m_beta), mask)
            g = a * _dot(do_blk, vt, ((1,), (1,)))
            g_upto = _running_sums(g, triu) + g_before
            sig = jnp.where(z >= 0.0, 1.0, e) / (1.0 + e)
            dz = (_masked(g - sig * g_upto, mask) * (HEAD_DIM ** -0.5)).astype(BF16)
            dk_ref[pl.ds(start, t), _head(p)] += _dot(dz, q, ((0,), (0,)))
            dv_ref[pl.ds(start, t), _head(p)] += _dot(a.astype(BF16), do_blk, ((0,), (0,)))
            return dq + _dot(dz, kt, ((1,), (0,))), log_rest - upto[:, t - 1:t], g_upto[:, t - 1:t]

        def step(n, carry):
            start = pl.multiple_of(n * t, t)
            return tuple(block(p, pl.ds(0, tq), start, carry[p], None) for p in range(ATTN_HEADS))

        carry = tuple((jnp.zeros((tq, HEAD_DIM), F32), lsum_ref[p], jnp.zeros((tq, 1), F32))
                      for p in range(ATTN_HEADS))
        carry = lax.fori_loop(0, i * ratio, step, carry)
        for p in range(ATTN_HEADS):
            state = carry[p]
            for r in range(ratio):
                below = tuple(val[r * t:] for val in state)
                below = block(p, pl.ds(r * t, tq - r * t), pl.multiple_of(i * tq + r * t, t), below,
                              _causal(tq - r * t, t))
                state = tuple(jnp.concatenate([val[:r * t], new], axis=0) if r else new
                              for val, new in zip(state, below))
            dq_ref[:, _head(p)] = state[0].astype(BF16)
        after()

    q_spec, k_spec, whole, rows = _attn_specs(s, tq, heads)
    outs = pl.pallas_call(
        body, name="attn_bwd", grid=grid, in_specs=[q_spec, k_spec, whole, rows, q_spec] + extra_specs,
        out_specs=(q_spec, whole, whole, *extra_specs),
        out_shape=(_sds((s, width), BF16), _sds((s, width), F32), _sds((s, width), F32), *extra_out),
        input_output_aliases=aliases, scratch_shapes=scratch,
        compiler_params=_params(*semantics))(qk, qk, v, lsum, do, *extra)
    return outs[0], outs[1], outs[2], list(outs[3:])


def _as_rows(a):
    return a.reshape(-1, a.shape[-1])


def _flat_call(name, body, ins, out_dtypes, block_bytes=1 << 20):
    shape = ins[0].shape
    rows, cols = _as_rows(ins[0]).shape
    tr = _tile(rows, max(8, block_bytes // (4 * cols)), 8)
    spec = pl.BlockSpec((tr, cols), lambda i: (i, 0))
    outs = pl.pallas_call(
        body, name=name, grid=(rows // tr,), in_specs=[spec] * len(ins), out_specs=tuple([spec] * len(out_dtypes)),
        out_shape=tuple(_sds((rows, cols), dt) for dt in out_dtypes), compiler_params=_params("parallel"),
    )(*[_as_rows(a) for a in ins])
    return [o.reshape(shape) for o in outs]


def _scalars(*values):
    return jnp.stack([jnp.asarray(v, jnp.int32) for v in values])


def _row_tile(rows, cols):
    return _tile(rows, max(ROWS_PER_TILE, (1 << 22) // (4 * cols)), ROWS_PER_TILE)


def _place_cast(w, chip):
    cols = w.shape[-1]
    rows = w.size // cols
    tr = _row_tile(rows, cols)

    def body(s_ref, w_ref, o_ref):
        o_ref[...] = w_ref[...].astype(BF16)

    out = pl.pallas_call(
        body, name="place_cast",
        grid_spec=pltpu.PrefetchScalarGridSpec(
            num_scalar_prefetch=1, grid=(rows // tr,),
            in_specs=[pl.BlockSpec((tr, cols), lambda i, s: (i, 0))],
            out_specs=pl.BlockSpec((None, tr, cols), lambda i, s: (s[0], i, 0))),
        out_shape=_sds((N_CHIPS, rows, cols), BF16), compiler_params=_params("parallel"),
    )(_scalars(chip), w.reshape(rows, cols))
    return out.reshape((N_CHIPS,) + w.shape)


def _add_halves(g, got, core):
    _, rows, cols = got.shape
    tr = _row_tile(rows, cols)

    def body(s_ref, a_ref, b_ref, o_ref):
        o_ref[...] = (a_ref[...].astype(F32) + b_ref[...].astype(F32)).astype(BF16)

    spec = pl.BlockSpec((None, tr, cols), lambda j, i, s: (j, i, 0))
    return pl.pallas_call(
        body, name="add_halves",
        grid_spec=pltpu.PrefetchScalarGridSpec(
            num_scalar_prefetch=1, grid=(N_CHIPS, rows // tr),
            in_specs=[pl.BlockSpec((None, None, tr, cols), lambda j, i, s: (j, s[0], i, 0)), spec], out_specs=spec),
        out_shape=_sds(got.shape, BF16), compiler_params=_params("parallel", "parallel"),
    )(_scalars(core), g, got)


def _sum_parts(own, others, chip, core):
    _, rows, cols = own.shape
    tr = _row_tile(rows, cols)

    def body(s_ref, a_ref, t_ref, o_ref):
        total = a_ref[...].astype(F32)
        for j in range(N_CHIPS - 1):
            total = total + t_ref[j].astype(F32)
        o_ref[...] = total

    return pl.pallas_call(
        body, name="sum_parts",
        grid_spec=pltpu.PrefetchScalarGridSpec(
            num_scalar_prefetch=1, grid=(rows // tr,),
            in_specs=[pl.BlockSpec((None, tr, cols), lambda i, s: (s[0], i, 0)),
                      pl.BlockSpec((N_CHIPS - 1, tr, cols), lambda i, s: (0, i, 0))],
            out_specs=pl.BlockSpec((None, tr, cols), lambda i, s: (s[1], i, 0))),
        out_shape=_sds((2, rows, cols), F32), compiler_params=_params("parallel"),
    )(_scalars(chip, core), own, others)


def _sum_leading(a, name):
    n = a.shape[0]
    shape = a.shape[1:]
    cols = shape[-1]
    rows = a.size // (n * cols)
    tr = _tile(rows, max(8, (1 << 20) // (4 * cols)), 8)

    def body(a_ref, o_ref):
        total = a_ref[0].astype(F32)
        for j in range(1, n):
            total = total + a_ref[j].astype(F32)
        o_ref[...] = total

    out = pl.pallas_call(
        body, name=name, grid=(rows // tr,), in_specs=[pl.BlockSpec((n, tr, cols), lambda i: (0, i, 0))],
        out_specs=pl.BlockSpec((tr, cols), lambda i: (i, 0)), out_shape=_sds((rows, cols), F32),
        compiler_params=_params("parallel"))(a.reshape(n, rows, cols))
    return out.reshape(shape)


def _adamw(w, g, m, v):
    def body(w_ref, g_ref, m_ref, v_ref, go_ref, d_ref, nm_ref, nv_ref):
        gv = g_ref[...]
        go_ref[...] = gv
        nm = ADAM_B1 * m_ref[...] + (1.0 - ADAM_B1) * gv
        nv = ADAM_B2 * v_ref[...] + (1.0 - ADAM_B2) * jnp.square(gv)
        nm_ref[...] = nm
        nv_ref[...] = nv
        m_hat = nm / (1.0 - ADAM_B1 ** ADAM_STEP)
        v_hat = nv / (1.0 - ADAM_B2 ** ADAM_STEP)
        d_ref[...] = -ADAM_LR * (m_hat / (jnp.sqrt(v_hat) + ADAM_EPS) + ADAM_WD * w_ref[...])

    return _flat_call("adamw", body, (w, g, m, v), (F32, F32, F32, F32), block_bytes=1 << 21)


def _mesh_position():
    return lax.axis_index("x"), lax.axis_index("y"), lax.axis_index("c")


def _other_chips(x, y):
    return [(1 - x, y), (x, 1 - y), (1 - x, 1 - y)]


def _chip_index(chip):
    return 2 * chip[0] + chip[1]


def _remote(src, dst, send_sem, recv_sem, device):
    return pltpu.make_async_remote_copy(src_ref=src, dst_ref=dst, send_sem=send_sem, recv_sem=recv_sem,
                                        device_id=device, device_id_type=MESH)


def _pieces(view):
    shape = list(view.shape)
    counts = []
    for axis in range(len(shape) - 1):
        want = -(-(view.dtype.itemsize * functools.reduce(lambda a, b: a * b, shape)) // DMA_PIECE_BYTES)
        unit = 1 if axis < len(shape) - 2 else ROWS_PER_TILE
        n, k = shape[axis], 1
        for cand in range(1, n + 1):
            if n % cand == 0 and (n // cand) % unit == 0:
                k = cand
                if cand >= want:
                    break
        counts.append(k)
        shape[axis] = n // k
    grid = [()]
    for axis, k in enumerate(counts):
        grid = [idx + (pl.ds(i * shape[axis], shape[axis]),) for idx in grid for i in range(k)]
    return grid


def _start_remote(src, dst, send_sem, recv_sem, device):
    for idx in _pieces(src):
        _remote(src.at[idx], dst.at[idx], send_sem, recv_sem, device).start()
    return _remote(src, dst, send_sem, recv_sem, device)


class _Ride:
    def __init__(self, operands, out_shapes, aliased, scratch_shapes, first, mid, last):
        self.operands, self.out_shapes, self.aliased = list(operands), list(out_shapes), aliased
        self.scratch_shapes, self.first, self.mid, self.last = list(scratch_shapes), first, mid, last

    def aliases(self, first_in, first_out):
        return {first_in + w: first_out + w for w in range(len(self.operands))} if self.aliased else {}


def _run_ride(name, ride):
    n = len(ride.operands)

    def body(*refs):
        ins, outs, sems = refs[:n], refs[n:2 * n], refs[2 * n:]
        ride.first(ins, outs, sems)
        if ride.mid is not None:
            ride.mid(ins, outs, sems)
        ride.last(ins, outs, sems)

    return pl.pallas_call(
        body, name=name, in_specs=[ANY] * n, out_specs=tuple([ANY] * n), out_shape=tuple(ride.out_shapes),
        input_output_aliases=ride.aliases(0, 0), scratch_shapes=ride.scratch_shapes,
    )(*ride.operands)


def _gather_ride(blocks):
    n = len(blocks)
    halves = [a.shape[1] // 2 for a in blocks]
    assert all(a.shape[1] % 2 == 0 for a in blocks)
    pairs = [(w, j) for w in range(n) for j in range(3)]

    def plan(outs, sems):
        ici_send, ici_recv, d2d_send, d2d_recv = sems
        x, y, c = _mesh_position()
        chips = _other_chips(x, y)

        def half(w, chip, which):
            return outs[w].at[chip, pl.ds(which * halves[w], halves[w])]

        def over_ici(w, j, chip):
            there = half(w, chip, c)
            return there, there, ici_send.at[3 * w + j], ici_recv.at[3 * w + j], (*chips[j], c)

        def over_d2d(w, j, which):
            there = half(w, _chip_index(chips[j]), which)
            return there, there, d2d_send.at[3 * w + j], d2d_recv.at[3 * w + j], (x, y, 1 - c)

        return _chip_index((x, y)), c, chips, over_ici, over_d2d

    def first(ins, outs, sems):
        me, _, _, over_ici, _ = plan(outs, sems)
        for w, j in pairs:
            _start_remote(*over_ici(w, j, me))

    def mid(ins, outs, sems):
        _, c, chips, over_ici, over_d2d = plan(outs, sems)
        for w, j in pairs:
            _remote(*over_ici(w, j, _chip_index(chips[j]))).wait_recv()
            _start_remote(*over_d2d(w, j, c))

    def last(ins, outs, sems):
        me, c, _, over_ici, over_d2d = plan(outs, sems)
        for w, j in pairs:
            _remote(*over_d2d(w, j, 1 - c)).wait_recv()
        for w, j in pairs:
            _remote(*over_ici(w, j, me)).wait_send()
            _remote(*over_d2d(w, j, c)).wait_send()

    return _Ride(blocks, [_sds(a.shape, a.dtype) for a in blocks], True,
                 [pltpu.SemaphoreType.DMA((3 * n,))] * 4, first, mid, last)


def _swap_halves(grads):
    n = len(grads)

    def body(*refs):
        ins, got = refs[:n], refs[n:2 * n]
        send_sem, recv_sem = refs[2 * n:]
        x, y, c = _mesh_position()
        swaps = [_start_remote(ins[w].at[pl.ds(0, N_CHIPS), 1 - c], got[w], send_sem.at[w], recv_sem.at[w],
                               (x, y, 1 - c)) for w in range(n)]
        for cp in swaps:
            cp.wait()

    return pl.pallas_call(
        body, name="swap_halves", in_specs=[ANY] * n, out_specs=tuple([ANY] * n),
        out_shape=tuple(_sds((N_CHIPS,) + a.shape[2:], a.dtype) for a in grads),
        scratch_shapes=[pltpu.SemaphoreType.DMA((n,))] * 2,
    )(*grads)


def _scatter_ride(parts):
    n = len(parts)
    pairs = [(w, j) for w in range(n) for j in range(3)]

    def plan(ins, outs, sems):
        send_sem, recv_sem = sems
        x, y, c = _mesh_position()
        chips = _other_chips(x, y)

        def to_chip(w, j):
            return (ins[w].at[_chip_index(chips[j])], outs[w].at[j], send_sem.at[3 * w + j], recv_sem.at[3 * w + j],
                    (*chips[j], c))

        return to_chip

    def first(ins, outs, sems):
        to_chip = plan(ins, outs, sems)
        for w, j in pairs:
            _start_remote(*to_chip(w, j))

    def last(ins, outs, sems):
        to_chip = plan(ins, outs, sems)
        for w, j in pairs:
            _remote(*to_chip(w, j)).wait_recv()
        for w, j in pairs:
            _remote(*to_chip(w, j)).wait_send()

    return _Ride(parts, [_sds((N_CHIPS - 1,) + a.shape[1:], a.dtype) for a in parts], False,
                 [pltpu.SemaphoreType.DMA((3 * n,))] * 2, first, None, last)


def _join_ride(halves):
    n = len(halves)

    def plan(outs, sems):
        send_sem, recv_sem = sems
        x, y, c = _mesh_position()

        def half(w, which):
            return outs[w].at[which], outs[w].at[which], send_sem.at[w], recv_sem.at[w], (x, y, 1 - c)

        return c, half

    def first(ins, outs, sems):
        c, half = plan(outs, sems)
        for w in range(n):
            _start_remote(*half(w, c))

    def last(ins, outs, sems):
        c, half = plan(outs, sems)
        for w in range(n):
            _remote(*half(w, 1 - c)).wait_recv()
            _remote(*half(w, c)).wait_send()

    return _Ride(halves, [_sds(a.shape, a.dtype) for a in halves], True,
                 [pltpu.SemaphoreType.DMA((n,))] * 2, first, None, last)


def _gather_small(vec):
    length = vec.shape[1]
    flips = [(fx, fy, fc) for fx in (0, 1) for fy in (0, 1) for fc in (0, 1)][1:]

    def body(v_ref, o_ref, send_sem, recv_sem, local_sem):
        x, y, c = _mesh_position()
        me = 4 * x + 2 * y + c
        local = pltpu.make_async_copy(v_ref, o_ref.at[me], local_sem)
        local.start()
        peers = [(x ^ fx, y ^ fy, c ^ fc) for fx, fy, fc in flips]
        sends = [_remote(v_ref, o_ref.at[me], send_sem.at[k], recv_sem.at[k], peer) for k, peer in enumerate(peers)]
        for cp in sends:
            cp.start()
        for k, (px, py, pc) in enumerate(peers):
            there = o_ref.at[4 * px + 2 * py + pc]
            _remote(there, there, send_sem.at[k], recv_sem.at[k], peers[k]).wait_recv()
        for cp in sends:
            cp.wait_send()
        local.wait()

    return pl.pallas_call(
        body, name="gather_small", in_specs=[ANY], out_specs=ANY, out_shape=_sds((N_DEVICES, 1, length), F32),
        scratch_shapes=[pltpu.SemaphoreType.DMA((N_DEVICES - 1,))] * 2 + [pltpu.SemaphoreType.DMA],
    )(vec)


MATRICES = ("w_in", "w_attn_branch", "w_pool_group", "w_pool_branch", "w_out", "w_up", "w_down", "w_ple", "w_ple_gate")
VECTORS = ("norm_mix_pre", "pool_scale", "norm_mix_post", "norm_ffn_pre", "conv_b", "norm_ffn_post", "norm_ple_post")
WEIGHTS = ("norm_mix_pre", "w_in", "w_attn_branch", "w_pool_group", "pool_scale", "w_pool_branch", "w_out",
           "norm_mix_post", "norm_ffn_pre", "w_up", "conv_w", "conv_b", "w_down", "norm_ffn_post", "w_ple",
           "w_ple_gate", "norm_ple_post")


EARLY = ("w_in",)
LATE = tuple(k for k in MATRICES if k not in EARLY)
RIDES = (("w_attn_branch", "w_pool_branch"), ("w_out", "w_pool_group"), ("w_up",),
         ("w_down", "w_ple_gate", "w_ple"))
assert sorted(k for names in RIDES for k in names) == sorted(LATE)


def _halves_view(g):
    return g.reshape(N_CHIPS, 2, g.size // (2 * N_CHIPS * g.shape[-1]), g.shape[-1])


def _pair_sums(gmat, names, core):
    flat = [_halves_view(gmat[k]) for k in names]
    return [_add_halves(a, b, core) for a, b in zip(flat, _swap_halves(flat))]


def _local_step(x, p, target, w_in, late_blocks, vec, conv_w, chip, core):
    s, d = x.shape
    groups = len(POOL_WINDOWS)
    aw = w_in.shape[2] // 2
    assert w_in.shape[2] == d

    h1 = _rms_fwd(x, vec["norm_mix_pre"])
    full = {}

    def carrying(names):
        return _gather_ride([late_blocks[k] for k in names])

    def carried(names, outs):
        full.update(zip(names, outs))

    qk = _mm_nn("proj_qk", h1, w_in, BF16, j0=0, nj=1)
    v = _mm_nn("proj_v", h1, w_in, BF16, j0=1, c0=0, cn=aw)
    u = _mm_nn("proj_u", h1, w_in, F32, j0=1, c0=aw, cn=aw)
    ga, got = _mm_nn("proj_ga", h1, w_in, BF16, j0=2, nj=1, ride=carrying(RIDES[0]))
    carried(RIDES[0], got)
    gp, got = _mm_nn("proj_gp", h1, w_in, BF16, j0=3, nj=1, ride=carrying(RIDES[1]))
    carried(RIDES[1], got)
    attn, lsum, got = _attn_fwd(qk, v, carrying(RIDES[2]))
    carried(RIDES[2], got)
    w_ab, w_pb, w_up = (full[k] for k in ("w_attn_branch", "w_pool_branch", "w_up"))
    w_pg = full["w_pool_group"].transpose(1, 0, 2, 3)
    w_pg = w_pg.reshape(groups, -1, w_pg.shape[-1])
    as_rows = lambda k: full[k].reshape(1, -1, full[k].shape[-1])
    w_out = as_rows("w_out")
    assert w_pb.shape[1] == aw == w_ab.shape[1]
    ya = _mm_nn("attn_branch", attn, w_ab, BF16)
    pooled = _pool_fwd(u)
    pg = _group_nn("pool_group", pooled, w_pg, F32)
    ps = _scale_cols(pg, vec["pool_scale"])
    yp = _mm_nn("pool_branch", ps, w_pb, BF16)
    mixed = _mix_fwd(ga, gp, ya, yp)
    mo = _mm_nn("mix_out", mixed, w_out, F32)
    x2, h2 = _post_pre(x, mo, vec["norm_mix_post"], vec["norm_ffn_pre"])
    up, got = _mm_nn("ffn_up", h2, w_up, F32, tn=1408, ride=carrying(RIDES[3]))
    carried(RIDES[3], got)
    w_down, w_pleg, w_ple = as_rows("w_down"), as_rows("w_ple_gate"), full["w_ple"]
    act, cg, cv = _conv_gelu_fwd(up, conv_w, vec["conv_b"])
    yf = _mm_nn("ffn_down", act, w_down, F32, tk=2816)
    x3 = _post(x2, yf, vec["norm_ffn_post"])
    e = _mm_nn("ple_embed", p, w_ple, F32)
    t = _mm_nn("ple_gate", x3, w_pleg, F32)
    dx4, dt, de, g_ple_post, loss_rows = _final(x3, t, e, vec["norm_ple_post"], target)

    gvec, gmat = {}, {}
    gvec["norm_ple_post"] = g_ple_post
    gmat["w_ple"] = _mm_tn("d_w_ple", p, de, N_CHIPS, BF16)
    gmat["w_ple_gate"] = _mm_tn("d_w_ple_gate", x3, dt, 1, BF16)
    dx3 = _mm_nt("d_x3", dt, w_pleg, F32, add=dx4)
    dyf, gvec["norm_ffn_post"] = _bwd_post(dx3, yf, vec["norm_ffn_post"])
    gmat["w_down"] = _mm_tn("d_w_down", act, dyf, 1, BF16)
    dact = _mm_nt("d_act", dyf, w_down, BF16, tn=1408)
    dup_g, dup_v, dcw_g, dcw_v, dcb_g, dcb_v = _conv_gelu_bwd(up, cg, cv, dact, conv_w)
    dup = _concat_cols("concat_dup", (dup_g, dup_v))
    gconv_w = jnp.concatenate([dcw_g, dcw_v], axis=1)
    gvec["conv_b"] = jnp.concatenate([dcb_g, dcb_v], axis=1)
    gmat["w_up"] = _mm_tn("d_w_up", h2, dup, N_CHIPS, BF16, tn=1408)
    dh2 = _mm_nt("d_h2", dup, w_up, F32, tk=2816)
    dx2, dmo, gvec["norm_ffn_pre"], gvec["norm_mix_post"] = _bwd_mid(
        dx3, dh2, x2, vec["norm_ffn_pre"], mo, vec["norm_mix_post"])
    gmat["w_out"] = _mm_tn("d_w_out", mixed, dmo, 1, BF16)
    dmixed = _mm_nt("d_mixed", dmo, w_out, F32)
    dya, dyp, dga, dgp = _bwd_mix(dmixed, ga, gp, ya, yp)
    gmat["w_attn_branch"] = _mm_tn("d_w_attn_branch", attn, dya, N_CHIPS, BF16)
    dattn = _mm_nt("d_attn", dya, w_ab, BF16)
    gmat["w_pool_branch"] = _mm_tn("d_w_pool_branch", ps, dyp, N_CHIPS, BF16)
    dps = _mm_nt("d_ps", dyp, w_pb, F32)
    dpg, gvec["pool_scale"] = _bwd_pool_scale(dps, pg, vec["pool_scale"])
    g_pg = _group_tn("d_w_pool_group", pooled, dpg, groups, BF16)
    gmat["w_pool_group"] = g_pg.reshape(groups, N_CHIPS, -1, g_pg.shape[-1]).transpose(1, 0, 2, 3)
    dpooled = _group_nn("d_pooled", dpg, w_pg, F32, transpose_w=True)
    du = _pool_bwd(dpooled)
    pair = dict(zip(LATE, _pair_sums(gmat, LATE, core)))
    dq, dk, dv, others = _attn_bwd(qk, v, lsum, dattn, _scatter_ride([pair[k] for k in LATE]))
    halves = [_sum_parts(pair[k], part, chip, core) for k, part in zip(LATE, others)]
    dproj = _concat_cols("concat_dproj", (dq, dk, dv, du, dga, dgp))
    gmat["w_in"], joined = _mm_tn("d_w_in", h1, dproj, N_CHIPS, BF16, ride=_join_ride(halves))
    reduced = dict(zip(LATE, joined))
    early = _pair_sums(gmat, EARLY, core)
    dh1, got = _mm_nt("d_h1", dproj, w_in, F32, ride=_scatter_ride(early))
    halves = [_sum_parts(a, part, chip, core) for a, part in zip(early, got)]
    reduced.update(zip(EARLY, _run_ride("join_halves", _join_ride(halves))))
    grad_x, gvec["norm_mix_pre"] = _bwd_first(dx2, dh1, x, vec["norm_mix_pre"])
    return loss_rows[0, 0], grad_x, reduced, gvec, gconv_w


def kernel(x, p, norm_mix_pre, w_in, w_attn_branch, w_pool_group, pool_scale, w_pool_branch, w_out, norm_mix_post, norm_ffn_pre, w_up, conv_w, conv_b, w_down, norm_ffn_post, w_ple, w_ple_gate, norm_ple_post, loss_target, m_norm_mix_pre, m_w_in, m_w_attn_branch, m_w_pool_group, m_pool_scale, m_w_pool_branch, m_w_out, m_norm_mix_post, m_norm_ffn_pre, m_w_up, m_conv_w, m_conv_b, m_w_down, m_norm_ffn_post, m_w_ple, m_w_ple_gate, m_norm_ple_post, v_norm_mix_pre, v_w_in, v_w_attn_branch, v_w_pool_group, v_pool_scale, v_w_pool_branch, v_w_out, v_norm_mix_post, v_norm_ffn_pre, v_w_up, v_conv_w, v_conv_b, v_w_down, v_norm_ffn_post, v_w_ple, v_w_ple_gate, v_norm_ple_post):
    given = dict(locals())
    w = {k: given[k][0] for k in WEIGHTS}
    m = {k: given["m_" + k][0] for k in WEIGHTS}
    v = {k: given["v_" + k][0] for k in WEIGHTS}
    chip = 2 * lax.axis_index("x") + lax.axis_index("y")

    core = lax.axis_index("c")
    w_in_full = _run_ride("gather_weights", _gather_ride([_place_cast(w[k], chip) for k in EARLY]))[0]
    late_blocks = {k: _place_cast(w[k], chip) for k in LATE}
    vec = {k: w[k].reshape(1, -1) for k in VECTORS}
    taps = _gather_small(w["conv_w"].reshape(1, -1))[:, 0][0::2]
    f2q = w["conv_w"].shape[1]
    conv_w_full = taps.reshape(N_CHIPS, 3, f2q).transpose(1, 0, 2).reshape(3, N_CHIPS * f2q)

    loss_rows, grad_x, reduced, gvec, gconv_w = _local_step(
        x[0], p[0, 0], loss_target[0], w_in_full, late_blocks, vec, conv_w_full, chip, core)
    loss = lax.psum(loss_rows, ("x", "y", "c"))

    gw = {k: reduced[k].reshape(w[k].shape) for k in MATRICES}
    sizes = [gvec[k].shape[1] for k in VECTORS]
    small = jnp.concatenate([gvec[k] for k in VECTORS] + [gconv_w.reshape(1, -1)], axis=1)
    small = _sum_leading(_gather_small(small), "sum_small")
    offset = 0
    for k, n in zip(VECTORS, sizes):
        gw[k] = small[0, offset:offset + n]
        offset += n
    gconv_w = small[0, offset:].reshape(3, N_CHIPS, f2q)
    gw["conv_w"] = lax.dynamic_index_in_dim(gconv_w, chip, axis=1, keepdims=False)

    delta, new_m, new_v = {}, {}, {}
    for k in MATRICES:
        gw[k], delta[k], new_m[k], new_v[k] = _adamw(w[k], gw[k], m[k], v[k])
    tiny = VECTORS + ("conv_w",)
    pack = lambda tree: jnp.concatenate([tree[k].reshape(1, -1) for k in tiny], axis=1)
    _, d_s, m_s, v_s = _adamw(pack(w), pack(gw), pack(m), pack(v))
    offset = 0
    for k in tiny:
        n = w[k].size
        delta[k], new_m[k], new_v[k] = (a[0, offset:offset + n].reshape(w[k].shape) for a in (d_s, m_s, v_s))
        offset += n

    lead = lambda tree: [tree[k].reshape((1,) + w[k].shape) for k in WEIGHTS]
    return (loss, grad_x[None], *lead(gw), *lead(delta), *lead(new_m), *lead(new_v))
```
